```python
import jax, jax.numpy as jnp
from jax import lax
import numpy as np

D_MODEL = 1024
BATCH = 8
SEQ = 2048
DEPTH = 4
DEC_BATCH = 32
DEC_SEQ = 8
PAST_LEN = 8192
PAGE_SIZE = 128

N_EVEN = (DEPTH + 1) // 2
N_ODD = DEPTH // 2
HEAD_DIM = 64
NSA_WIDTH = D_MODEL // 2
N_HEADS_A = NSA_WIDTH // HEAD_DIM
N_KV_A = 2
CMP_BLOCK = 32
SEL_BLOCK = 64
SEL_RATIO = SEL_BLOCK // CMP_BLOCK
TOP_N = 16
WINDOW = 512
Q_BLOCK = 64
ROPE_THETA = 10000.0
RG_WIDTH = D_MODEL // 2
RG_BLOCKS = 8
RG_BLOCK_W = RG_WIDTH // RG_BLOCKS
RG_CONV = 4
RG_C = 8.0
MIX_WIDTH = NSA_WIDTH + RG_WIDTH
Q_COLS = NSA_WIDTH
KV_COLS = 6 * N_KV_A * HEAD_DIM
GATE_COLS = 3 * N_HEADS_A
IN_COLS = Q_COLS + KV_COLS + GATE_COLS + 2 * RG_WIDTH
CF_WIDTH = D_MODEL
CF_KERNEL = 31
FF_DENSE = 2816
N_EXPERTS = 8
TOP_K = 2
FF_EXPERT = 3584
MOE_BLOCK = 128
ALPHA = (2.0 * DEPTH) ** 0.25
BETA = (8.0 * DEPTH) ** -0.25
LN_EPS = 1e-5
NEG = -1e30
FORCE_SCORE = 1e4

kernel_name = 'nsa_rglru_conformer_moe_decoder_step'


def layer_norm(x, g, b):
    xf = x.astype(jnp.float32)
    mu = jnp.mean(xf, axis=-1, keepdims=True)
    var = jnp.mean(jnp.square(xf - mu), axis=-1, keepdims=True)
    return ((xf - mu) * lax.rsqrt(var + LN_EPS) * g + b).astype(x.dtype)


def rope(x, pos):
    hd = x.shape[-1]
    half = hd // 2
    inv = 1.0 / (ROPE_THETA ** (jnp.arange(half, dtype=jnp.float32) * (2.0 / hd)))
    ang = pos.astype(jnp.float32)[:, None] * inv[None, :]
    cos = jnp.cos(ang)[:, None, :]
    sin = jnp.sin(ang)[:, None, :]
    xf = x.astype(jnp.float32)
    x1, x2 = xf[..., :half], xf[..., half:]
    return jnp.concatenate([x1 * cos - x2 * sin, x1 * sin + x2 * cos], axis=-1).astype(x.dtype)


def masked_softmax(s, mask):
    s = jnp.where(mask, s.astype(jnp.float32), NEG)
    e = jnp.exp(s - jnp.max(s, axis=-1, keepdims=True)) * mask
    return e / jnp.maximum(jnp.sum(e, axis=-1, keepdims=True), 1e-30)


def causal_dwconv(x, buf, w, b):
    xp = jnp.concatenate([buf.astype(x.dtype), x], axis=1)
    y = lax.conv_general_dilated(xp, w.astype(x.dtype)[:, None, :], window_strides=(1,), padding='VALID',
                                 dimension_numbers=('NWC', 'WIO', 'NWC'), feature_group_count=x.shape[-1])
    return y + b, xp[:, xp.shape[1] - (w.shape[0] - 1):]


def swiglu(x, w1, w3, w2):
    return (jax.nn.silu(x @ w1) * (x @ w3)) @ w2


def compress_blocks(k, v, pe_k, pe_v, w_ck, w_cv):
    bsz, L, G, hd = k.shape
    nc = L // CMP_BLOCK
    kb = k[:, :nc * CMP_BLOCK].reshape(bsz, nc, CMP_BLOCK, G, hd) + pe_k[None, None, :, None, :]
    vb = v[:, :nc * CMP_BLOCK].reshape(bsz, nc, CMP_BLOCK, G, hd) + pe_v[None, None, :, None, :]
    ck = jnp.einsum('bnlgd,lde->bnge', kb, w_ck.reshape(CMP_BLOCK, hd, hd))
    cv = jnp.einsum('bnlgd,lde->bnge', vb, w_cv.reshape(CMP_BLOCK, hd, hd))
    cpos = jnp.arange(nc) * CMP_BLOCK + (CMP_BLOCK - 1)
    return rope(ck, cpos), cv, cpos


def selection_blocks(k):
    bsz, L, G, hd = k.shape
    nsel = -(-L // SEL_BLOCK)
    kp = jnp.pad(k, ((0, 0), (0, nsel * SEL_BLOCK - L), (0, 0), (0, 0)))
    return kp.reshape(bsz, nsel, SEL_BLOCK, G, hd).transpose(0, 3, 1, 2, 4)


_gather_blocks = jax.vmap(jax.vmap(lambda blocks, ids: blocks[ids]))


def nsa_attend(q, qpos, gate_logits, kc, vc, cpos, kblk, vblk, kw, vw, wpos):
    bsz, tq, nh, hd = q.shape
    G = kc.shape[2]
    R = nh // G
    scale = hd ** -0.5
    qg = q.reshape(bsz, tq, G, R, hd)
    m_c = cpos[None, :] <= qpos[:, None]
    p_c = masked_softmax(jnp.einsum('bqgrd,bngd->bqgrn', qg, kc) * scale, m_c[None, :, None, None, :])
    o_c = jnp.einsum('bqgrn,bngd->bqgrd', p_c.astype(vc.dtype), vc)
    nsel = kblk.shape[2]
    nc = kc.shape[1]
    imp = jnp.pad(p_c.sum(axis=3), ((0, 0), (0, 0), (0, 0), (0, nsel * SEL_RATIO - nc)))
    imp = imp.reshape(bsz, tq, G, nsel, SEL_RATIO).sum(-1)
    blk = jnp.arange(nsel)[None, :]
    cur = (qpos // SEL_BLOCK)[:, None]
    forced = ((blk == cur) | (blk == 0))[None, :, None, :]
    future = (blk > cur)[None, :, None, :]
    imp = jnp.where(forced, FORCE_SCORE, jnp.where(future, -1.0, imp))
    kk = min(TOP_N, nsel)
    _, idx = lax.top_k(imp, kk)
    idx = idx.transpose(0, 2, 1, 3)
    ks = _gather_blocks(kblk, idx).reshape(bsz, G, tq, kk * SEL_BLOCK, hd)
    vs = _gather_blocks(vblk, idx).reshape(bsz, G, tq, kk * SEL_BLOCK, hd)
    kpos = (idx[..., None] * SEL_BLOCK + jnp.arange(SEL_BLOCK)).reshape(bsz, G, tq, kk * SEL_BLOCK)
    m_s = kpos <= qpos[None, None, :, None]
    p_s = masked_softmax(jnp.einsum('bqgrd,bgqjd->bgqrj', qg, ks) * scale, m_s[:, :, :, None, :])
    o_s = jnp.einsum('bgqrj,bgqjd->bqgrd', p_s.astype(vs.dtype), vs)
    dpos = qpos[:, None] - wpos[None, :]
    m_w = (dpos >= 0) & (dpos < WINDOW) & (wpos[None, :] >= 0)
    p_w = masked_softmax(jnp.einsum('bqgrd,bkgd->bqgrk', qg, kw) * scale, m_w[None, :, None, None, :])
    o_w = jnp.einsum('bqgrk,bkgd->bqgrd', p_w.astype(vw.dtype), vw)
    g = jax.nn.sigmoid(gate_logits.astype(jnp.float32)).reshape(bsz, tq, G, R, 3).astype(q.dtype)
    o = g[..., 0:1] * o_c + g[..., 1:2] * o_s + g[..., 2:3] * o_w
    return o.reshape(bsz, tq, nh, hd)


def _lin_combine(left, right):
    a1, b1 = left
    a2, b2 = right
    return a1 * a2, a2 * b1 + b2


def rg_lru(x, h0, buf, conv_w, conv_b, wa, ba, wx, bx, lam):
    bsz, T, C = x.shape
    xc, new_buf = causal_dwconv(x, buf, conv_w, conv_b)
    xb = xc.reshape(bsz, T, RG_BLOCKS, RG_BLOCK_W)
    r = jax.nn.sigmoid((jnp.einsum('btnc,ncd->btnd', xb, wa).reshape(bsz, T, C) + ba).astype(jnp.float32))
    i = jax.nn.sigmoid((jnp.einsum('btnc,ncd->btnd', xb, wx).reshape(bsz, T, C) + bx).astype(jnp.float32))
    log_a = -RG_C * r * jax.nn.softplus(-lam.astype(jnp.float32))
    a = jnp.exp(log_a)
    b = jnp.sqrt(-jnp.expm1(2.0 * log_a)) * (i * xc.astype(jnp.float32))
    b = b.at[:, 0].add(a[:, 0] * h0.astype(jnp.float32))
    _, hs = lax.associative_scan(_lin_combine, (a, b), axis=1)
    return hs.astype(x.dtype), hs[:, -1].astype(h0.dtype), new_buf


def nsa_rglru_mixer(h, pos, past_kv, win_buf, rg_h0, rg_buf, w_in, w_out, w_ck, w_cv, pe_k, pe_v,
                    conv_w, conv_b, wa, ba, wx, bx, lam):
    bsz, T, _ = h.shape
    u = h @ w_in
    o1 = Q_COLS
    o2 = o1 + KV_COLS
    o3 = o2 + GATE_COLS
    o4 = o3 + RG_WIDTH
    q = rope(u[..., :o1].reshape(bsz, T, N_HEADS_A, HEAD_DIM), pos)
    kv = u[..., o1:o2].reshape(bsz, T, 6, N_KV_A, HEAD_DIM)
    gl = u[..., o2:o3].reshape(bsz, T, N_HEADS_A, 3)
    rx = u[..., o3:o4]
    ry = u[..., o4:]
    rows = jnp.stack([kv[:, :, 0], kv[:, :, 1], rope(kv[:, :, 2], pos), kv[:, :, 3]], axis=2)
    wrows = jnp.stack([rope(kv[:, :, 4], pos), kv[:, :, 5]], axis=2)
    full = rows if past_kv is None else jnp.concatenate([past_kv.astype(rows.dtype), rows], axis=1)
    kc, vc, cpos = compress_blocks(full[:, :, 0], full[:, :, 1], pe_k, pe_v, w_ck, w_cv)
    kblk = selection_blocks(full[:, :, 2])
    vblk = selection_blocks(full[:, :, 3])
    if win_buf is None:
        kw_pad = jnp.pad(wrows[:, :, 0], ((0, 0), (WINDOW, 0), (0, 0), (0, 0)))
        vw_pad = jnp.pad(wrows[:, :, 1], ((0, 0), (WINDOW, 0), (0, 0), (0, 0)))

        def one_block(bi):
            s = bi * Q_BLOCK
            qb = lax.dynamic_slice_in_dim(q, s, Q_BLOCK, axis=1)
            gb = lax.dynamic_slice_in_dim(gl, s, Q_BLOCK, axis=1)
            qpos = lax.dynamic_slice_in_dim(pos, s, Q_BLOCK)
            kwb = lax.dynamic_slice_in_dim(kw_pad, s, WINDOW + Q_BLOCK, axis=1)
            vwb = lax.dynamic_slice_in_dim(vw_pad, s, WINDOW + Q_BLOCK, axis=1)
            wpos = qpos[0] - WINDOW + jnp.arange(WINDOW + Q_BLOCK)
            return nsa_attend(qb, qpos, gb, kc, vc, cpos, kblk, vblk, kwb, vwb, wpos)

        o = lax.map(one_block, jnp.arange(T // Q_BLOCK))
        o = o.transpose(1, 0, 2, 3, 4).reshape(bsz, T, N_HEADS_A, HEAD_DIM)
        new_win = wrows[:, T - min(WINDOW, T):]
    else:
        wb = jnp.concatenate([win_buf.astype(wrows.dtype), wrows], axis=1)
        wpos = pos[0] - win_buf.shape[1] + jnp.arange(wb.shape[1])
        o = nsa_attend(q, pos, gl, kc, vc, cpos, kblk, vblk, wb[:, :, 0], wb[:, :, 1], wpos)
        new_win = wb[:, wb.shape[1] - min(WINDOW, wb.shape[1]):]
    rg, h_last, new_rg_buf = rg_lru(rx, rg_h0, rg_buf, conv_w, conv_b, wa, ba, wx, bx, lam)
    mix = jnp.concatenate([o.reshape(bsz, T, NSA_WIDTH), rg * jax.nn.gelu(ry)], axis=-1)
    return mix @ w_out, (rows, new_win, h_last, new_rg_buf)


def conformer_conv_module(h, buf, w1, b1, dw, db, g, b, w2, b2):
    u = h @ w1 + b1
    glu = u[..., :CF_WIDTH] * jax.nn.sigmoid(u[..., CF_WIDTH:])
    y, new_buf = causal_dwconv(glu, buf, dw, db)
    y = jax.nn.silu(layer_norm(y, g, b))
    return y @ w2 + b2, new_buf


def moe_swiglu(x, router, w1, w3, w2):
    bsz, T, d = x.shape
    xt = x.reshape(bsz * T, d)
    n = xt.shape[0]
    logits = (xt @ router).astype(jnp.float32)
    top_v, top_i = lax.top_k(logits, TOP_K)
    gates = jax.nn.softmax(top_v, axis=-1)
    fe = top_i.reshape(-1)
    ftok = jnp.repeat(jnp.arange(n, dtype=jnp.int32), TOP_K)
    fg = gates.reshape(-1)
    order = jnp.argsort(fe)
    se = fe[order]
    counts = jnp.bincount(fe, length=N_EXPERTS)
    padded = (counts + MOE_BLOCK - 1) // MOE_BLOCK * MOE_BLOCK
    pend = jnp.cumsum(padded)
    pstart = pend - padded
    ustart = jnp.cumsum(counts) - counts
    dest = pstart[se] + jnp.arange(n * TOP_K) - ustart[se]
    n_blk = -(-(n * TOP_K) // MOE_BLOCK) + N_EXPERTS
    n_rows = n_blk * MOE_BLOCK
    row_tok = jnp.full((n_rows,), n, jnp.int32).at[dest].set(ftok[order])
    row_gate = jnp.zeros((n_rows,), jnp.float32).at[dest].set(fg[order])
    blk_e = jnp.minimum(jnp.searchsorted(pend, jnp.arange(n_blk) * MOE_BLOCK, side='right'), N_EXPERTS - 1)
    xs = jnp.concatenate([xt, jnp.zeros((1, d), xt.dtype)], axis=0)[row_tok].reshape(n_blk, MOE_BLOCK, d)
    ys = lax.map(lambda a: swiglu(a[0], w1[a[1]], w3[a[1]], w2[a[1]]), (xs, blk_e))
    out = jnp.zeros((n + 1, d), jnp.float32).at[row_tok].add(
        ys.reshape(n_rows, d).astype(jnp.float32) * row_gate[:, None])
    return out[:n].astype(x.dtype).reshape(bsz, T, d)


def adaln_sublayer(x, c, wm, bm, g, b, fn):
    mod = jax.nn.silu(c) @ wm + bm
    shift, scale, gate = jnp.split(mod, 3, axis=-1)
    out, aux = fn(x * (1 + scale[:, None, :]) + shift[:, None, :])
    return layer_norm(ALPHA * x + (1 + gate[:, None, :]) * out, g, b), aux


def setup_inputs(seed: int = 0) -> dict:
    key = jax.random.key(seed)
    keys = iter(jax.random.split(key, 64))

    def nrm(shape, scale):
        return scale * jax.random.normal(next(keys), shape, jnp.float32)

    d = D_MODEL
    n_pages = PAST_LEN // PAGE_SIZE
    n_pool = (5 * DEC_BATCH * n_pages + 3) // 4
    win_len = min(WINDOW, PAST_LEN)
    x_prompt = nrm((BATCH, SEQ, d), 1.0)
    x_sample = nrm((DEC_BATCH, DEC_SEQ, d), 1.0)
    c_prompt = nrm((BATCH, d), 1.0)
    c_sample = nrm((DEC_BATCH, d), 1.0)
    page_table = jax.random.permutation(next(keys), n_pool)[:DEC_BATCH * n_pages].reshape(
        DEC_BATCH, n_pages).astype(jnp.int32)
    cache_nsa_kv = nrm((N_EVEN, n_pool, PAGE_SIZE, 4, N_KV_A, HEAD_DIM), 1.0)
    state_nsa_win = nrm((N_EVEN, DEC_BATCH, win_len, 2, N_KV_A, HEAD_DIM), 1.0)
    state_rglru_h = nrm((N_EVEN, DEC_BATCH, RG_WIDTH), 0.5)
    state_rglru_conv = nrm((N_EVEN, DEC_BATCH, RG_CONV - 1, RG_WIDTH), 1.0)
    state_conformer_conv = nrm((N_ODD, DEC_BATCH, CF_KERNEL - 1, CF_WIDTH), 1.0)
    a_target = jax.random.uniform(next(keys), (N_EVEN, RG_WIDTH), jnp.float32, 0.9, 0.999)
    s_lam = a_target ** (1.0 / RG_C)
    rg_lam = jnp.log(s_lam) - jnp.log1p(-s_lam)
    return {
        'x_prompt': x_prompt, 'x_sample': x_sample, 'c_prompt': c_prompt, 'c_sample': c_sample,
        'page_table': page_table, 'cache_nsa_kv': cache_nsa_kv, 'state_nsa_win': state_nsa_win,
        'state_rglru_h': state_rglru_h, 'state_rglru_conv': state_rglru_conv,
        'state_conformer_conv': state_conformer_conv,
        'w_mod': nrm((DEPTH, 2, d, 3 * d), 0.1 * d ** -0.5),
        'b_mod': nrm((DEPTH, 2, 3 * d), 0.01),
        'ln_g': 1.0 + nrm((DEPTH, 2, d), 0.01),
        'ln_b': nrm((DEPTH, 2, d), 0.01),
        'w_in_even': nrm((N_EVEN, d, IN_COLS), d ** -0.5),
        'w_out_even': nrm((N_EVEN, MIX_WIDTH, d), BETA * MIX_WIDTH ** -0.5),
        'w_cmp_k': nrm((N_EVEN, CMP_BLOCK * HEAD_DIM, HEAD_DIM), (CMP_BLOCK * HEAD_DIM) ** -0.5),
        'w_cmp_v': nrm((N_EVEN, CMP_BLOCK * HEAD_DIM, HEAD_DIM), (CMP_BLOCK * HEAD_DIM) ** -0.5),
        'pe_cmp_k': nrm((N_EVEN, CMP_BLOCK, HEAD_DIM), 0.02),
        'pe_cmp_v': nrm((N_EVEN, CMP_BLOCK, HEAD_DIM), 0.02),
        'rg_conv_w': nrm((N_EVEN, RG_CONV, RG_WIDTH), RG_CONV ** -0.5),
        'rg_conv_b': nrm((N_EVEN, RG_WIDTH), 0.01),
        'rg_wa': nrm((N_EVEN, RG_BLOCKS, RG_BLOCK_W, RG_BLOCK_W), RG_BLOCK_W ** -0.5),
        'rg_ba': nrm((N_EVEN, RG_WIDTH), 0.01),
        'rg_wx': nrm((N_EVEN, RG_BLOCKS, RG_BLOCK_W, RG_BLOCK_W), RG_BLOCK_W ** -0.5),
        'rg_bx': nrm((N_EVEN, RG_WIDTH), 0.01),
        'rg_lam': rg_lam,
        'cf_w1': nrm((N_ODD, d, 2 * CF_WIDTH), d ** -0.5),
        'cf_b1': nrm((N_ODD, 2 * CF_WIDTH), 0.01),
        'cf_dw': nrm((N_ODD, CF_KERNEL, CF_WIDTH), CF_KERNEL ** -0.5),
        'cf_db': nrm((N_ODD, CF_WIDTH), 0.01),
        'cf_ln_g': 1.0 + nrm((N_ODD, CF_WIDTH), 0.01),
        'cf_ln_b': nrm((N_ODD, CF_WIDTH), 0.01),
        'cf_w2': nrm((N_ODD, CF_WIDTH, d), BETA * CF_WIDTH ** -0.5),
        'cf_b2': nrm((N_ODD, d), 0.01),
        'ff_w1': nrm((N_EVEN, d, FF_DENSE), d ** -0.5),
        'ff_w3': nrm((N_EVEN, d, FF_DENSE), d ** -0.5),
        'ff_w2': nrm((N_EVEN, FF_DENSE, d), BETA * FF_DENSE ** -0.5),
        'moe_router': nrm((N_ODD, d, N_EXPERTS), d ** -0.5),
        'moe_w1': nrm((N_ODD, N_EXPERTS, d, FF_EXPERT), d ** -0.5),
        'moe_w3': nrm((N_ODD, N_EXPERTS, d, FF_EXPERT), d ** -0.5),
        'moe_w2': nrm((N_ODD, N_EXPERTS, FF_EXPERT, d), BETA * FF_EXPERT ** -0.5),
    }


def reference(x_prompt, x_sample, c_prompt, c_sample, page_table, cache_nsa_kv, state_nsa_win,
              state_rglru_h, state_rglru_conv, state_conformer_conv, w_mod, b_mod, ln_g, ln_b,
              w_in_even, w_out_even, w_cmp_k, w_cmp_v, pe_cmp_k, pe_cmp_v, rg_conv_w, rg_conv_b,
              rg_wa, rg_ba, rg_wx, rg_bx, rg_lam, cf_w1, cf_b1, cf_dw, cf_db, cf_ln_g, cf_ln_b,
              cf_w2, cf_b2, ff_w1, ff_w3, ff_w2, moe_router, moe_w1, moe_w3, moe_w2):
    bp, tp, _ = x_prompt.shape
    bs, ts, _ = x_sample.shape
    past = page_table.shape[1] * PAGE_SIZE
    pos_p = jnp.arange(tp)
    pos_s = past + jnp.arange(ts)
    hp, hs = x_prompt, x_sample
    kv_p, kv_s, win_p, win_s, rh_p, rh_s, rc_p, rc_s, cc_p, cc_s = ([] for _ in range(10))
    for l in range(DEPTH):
        i = l // 2
        if l % 2 == 0:
            ew = (w_in_even[i], w_out_even[i], w_cmp_k[i], w_cmp_v[i], pe_cmp_k[i], pe_cmp_v[i],
                  rg_conv_w[i], rg_conv_b[i], rg_wa[i], rg_ba[i], rg_wx[i], rg_bx[i], rg_lam[i])
            past_kv = cache_nsa_kv[i, page_table].reshape(bs, past, 4, N_KV_A, HEAD_DIM)
            hp, (r, w, hh, hc) = adaln_sublayer(
                hp, c_prompt, w_mod[l, 0], b_mod[l, 0], ln_g[l, 0], ln_b[l, 0],
                lambda h: nsa_rglru_mixer(h, pos_p, None, None, jnp.zeros((bp, RG_WIDTH), h.dtype),
                                          jnp.zeros((bp, RG_CONV - 1, RG_WIDTH), h.dtype), *ew))
            kv_p.append(r)
            win_p.append(w)
            rh_p.append(hh)
            rc_p.append(hc)
            hs, (r, w, hh, hc) = adaln_sublayer(
                hs, c_sample, w_mod[l, 0], b_mod[l, 0], ln_g[l, 0], ln_b[l, 0],
                lambda h: nsa_rglru_mixer(h, pos_s, past_kv, state_nsa_win[i], state_rglru_h[i],
                                          state_rglru_conv[i], *ew))
            kv_s.append(r)
            win_s.append(w)
            rh_s.append(hh)
            rc_s.append(hc)
            ffn = lambda h: (swiglu(h, ff_w1[i], ff_w3[i], ff_w2[i]), None)
        else:
            cw = (cf_w1[i], cf_b1[i], cf_dw[i], cf_db[i], cf_ln_g[i], cf_ln_b[i], cf_w2[i], cf_b2[i])
            hp, buf = adaln_sublayer(
                hp, c_prompt, w_mod[l, 0], b_mod[l, 0], ln_g[l, 0], ln_b[l, 0],
                lambda h: conformer_conv_module(h, jnp.zeros((bp, CF_KERNEL - 1, CF_WIDTH), h.dtype), *cw))
            cc_p.append(buf)
            hs, buf = adaln_sublayer(
                hs, c_sample, w_mod[l, 0], b_mod[l, 0], ln_g[l, 0], ln_b[l, 0],
                lambda h: conformer_conv_module(h, state_conformer_conv[i], *cw))
            cc_s.append(buf)
            ffn = lambda h: (moe_swiglu(h, moe_router[i], moe_w1[i], moe_w3[i], moe_w2[i]), None)
        hp, _ = adaln_sublayer(hp, c_prompt, w_mod[l, 1], b_mod[l, 1], ln_g[l, 1], ln_b[l, 1], ffn)
        hs, _ = adaln_sublayer(hs, c_sample, w_mod[l, 1], b_mod[l, 1], ln_g[l, 1], ln_b[l, 1], ffn)
    return (hp, hs, jnp.stack(kv_p), jnp.stack(kv_s), jnp.stack(win_p), jnp.stack(win_s),
            jnp.stack(rh_p), jnp.stack(rh_s), jnp.stack(rc_p), jnp.stack(rc_s),
            jnp.stack(cc_p), jnp.stack(cc_s))
```

```python
import functools

import jax
import jax.numpy as jnp
from jax import lax
from jax.experimental import pallas as pl
from jax.experimental.pallas import tpu as pltpu

F32 = jnp.float32
BF16 = jnp.bfloat16
I32 = jnp.int32

D_MODEL = 1024
HEAD_DIM = 64
N_HEADS_A = 8
N_KV_A = 2
HEADS_PER_GROUP = N_HEADS_A // N_KV_A
GROUP_COLS = N_KV_A * HEAD_DIM
CMP_BLOCK = 32
SEL_BLOCK = 64
TOP_N = 16
WINDOW = 512
ROPE_THETA = 10000.0
PAGE_SIZE = 128
RG_WIDTH = 512
RG_BLOCKS = 8
RG_CONV = 4
RG_C = 8.0
NSA_WIDTH = 512
Q_COLS = NSA_WIDTH
KV_COLS = 6 * GROUP_COLS
GATE_COLS = 3 * N_HEADS_A
CF_KERNEL = 31
CF_HALO = 32
N_EXPERTS = 8
DEPTH = 4
ALPHA = (2.0 * DEPTH) ** 0.25
LN_EPS = 1e-5
NEG = -1e30
FORCE_SCORE = 1e4
ATTN_SCALE = HEAD_DIM ** -0.5

LANES = 128
SUBLANES = 8
VMEM_LIMIT = 56 * 1024 * 1024

ROW_TILE = 512
Q_TILE = 256
CONV_TILE = 256
CONV_SUB = 32
MOE_TILE = 512
FF_CHUNK = 256
MOE_FF_CHUNK = 512


def _cparams(n_axes, **kw):
    return pltpu.CompilerParams(dimension_semantics=("arbitrary",) * n_axes,
                                vmem_limit_bytes=VMEM_LIMIT, **kw)


def _const_spec(shape, single_buffer=False):
    n = len(shape)
    if single_buffer:
        return pl.BlockSpec(shape, lambda *a: (0,) * n, pipeline_mode=pl.Buffered(1))
    return pl.BlockSpec(shape, lambda *a: (0,) * n)


def _mod_spec(mod3, tm, tps):
    if mod3.shape[1] == 1:
        return pl.BlockSpec((1, 1, mod3.shape[2]), lambda i: (i // tps, 0, 0))
    return pl.BlockSpec((1, tm, mod3.shape[2]), lambda i: (0, i, 0))


def _dot(a, b):
    return jnp.dot(a, b, preferred_element_type=F32)


def _dot_nt(a, b):
    return lax.dot_general(a, b, (((1,), (1,)), ((), ())), preferred_element_type=F32)


def _layer_norm(y, g, b):
    mu = jnp.mean(y, axis=-1, keepdims=True)
    yc = y - mu
    var = jnp.mean(yc * yc, axis=-1, keepdims=True)
    return yc * lax.rsqrt(var + LN_EPS) * g + b


def _post_norm(x, mod, out, g, b):
    gate = mod[:, 2 * D_MODEL:]
    return _layer_norm(ALPHA * x + (1.0 + gate) * out, g, b)


def _modulate(x, mod):
    return x * (1.0 + mod[:, D_MODEL:2 * D_MODEL]) + mod[:, :D_MODEL]


def _silu(x):
    return x * jax.nn.sigmoid(x)


def _rope128(v, cos, sin_signed):
    lane = lax.broadcasted_iota(I32, v.shape, 1)
    from_hi = pltpu.roll(v, LANES - HEAD_DIM // 2, 1)
    from_lo = pltpu.roll(v, HEAD_DIM // 2, 1)
    swapped = jnp.where((lane & (HEAD_DIM - 1)) < HEAD_DIM // 2, from_hi, from_lo)
    return v * cos + swapped * sin_signed


def _rope(v, cos, sin_signed):
    k = v.shape[1] // LANES
    parts = [_rope128(v[:, i * LANES:(i + 1) * LANES], cos, sin_signed) for i in range(k)]
    return parts[0] if k == 1 else jnp.concatenate(parts, axis=1)


def _softmax_parts(parts):
    masked = [jnp.where(m, s, NEG) for s, m in parts]
    mx = functools.reduce(jnp.maximum, [jnp.max(s, axis=-1, keepdims=True) for s in masked])
    es = [jnp.where(m, jnp.exp(s - mx), 0.0) for s, (_, m) in zip(masked, parts)]
    den = functools.reduce(jnp.add, [jnp.sum(e, axis=-1, keepdims=True) for e in es])
    return es, 1.0 / jnp.maximum(den, 1e-30)


def _attend(qb, pieces):
    es, inv = _softmax_parts([(_dot_nt(qb, k), m) for k, _, m in pieces])
    o = functools.reduce(jnp.add, [_dot(e.astype(BF16), v) for e, (_, v, _) in zip(es, pieces)])
    return o * inv


def _topk_mask(vals, n_valid, kk):
    lane = lax.broadcasted_iota(I32, vals.shape, 1)
    rank = jnp.zeros(vals.shape, I32)
    for m in range(n_valid):
        col = vals[:, m:m + 1]
        later = jnp.where(lane > m, 1, 0)
        rank = rank + jnp.where(col > vals, 1, jnp.where(col == vals, later, 0))
    return (rank < kk) & (lane < n_valid)


def _mod_kernel(c_ref, w_ref, b_ref, o_ref):
    s = _silu(c_ref[...]).astype(BF16)
    o_ref[0] = _dot(s, w_ref[0].astype(BF16)) + b_ref[0]


def _mod_all(c_all, w_mod, b_mod):
    n = c_all.shape[0]
    nl = w_mod.shape[0] * w_mod.shape[1]
    w = w_mod.reshape(nl, D_MODEL, 3 * D_MODEL)
    b = b_mod.reshape(nl, 1, 3 * D_MODEL)
    return pl.pallas_call(
        _mod_kernel, grid=(nl, 3),
        in_specs=[pl.BlockSpec((n, D_MODEL), lambda l, j: (0, 0)),
                  pl.BlockSpec((1, D_MODEL, D_MODEL), lambda l, j: (l, 0, j)),
                  pl.BlockSpec((1, 1, D_MODEL), lambda l, j: (l, 0, j))],
        out_specs=pl.BlockSpec((1, n, D_MODEL), lambda l, j: (l, 0, j)),
        out_shape=jax.ShapeDtypeStruct((nl, n, 3 * D_MODEL), F32),
        compiler_params=_cparams(2), name="mod_all")(c_all, w, b)


_O_KV = Q_COLS
_O_RX = _O_KV + KV_COLS
_O_RY = _O_RX + RG_WIDTH
_O_GL = _O_RY + RG_WIDTH
W_IN_COLS = _O_GL + LANES


def _win_kernel(x_ref, mod_ref, w_ref, cos_ref, sin_ref, q_ref, kv_ref, wr_ref, rx_ref, ry_ref, gl_ref):
    h = _modulate(x_ref[...], mod_ref[0]).astype(BF16)
    u = _dot(h, w_ref[...])
    cos = cos_ref[...]
    sin = sin_ref[...]
    q_ref[...] = _rope(u[:, :Q_COLS], cos, sin)
    c = _O_KV
    kv_ref[:, 0:2 * LANES] = u[:, c:c + 2 * LANES]
    kv_ref[:, 2 * LANES:3 * LANES] = _rope128(u[:, c + 2 * LANES:c + 3 * LANES], cos, sin)
    kv_ref[:, 3 * LANES:4 * LANES] = u[:, c + 3 * LANES:c + 4 * LANES]
    wr_ref[:, 0:LANES] = _rope128(u[:, c + 4 * LANES:c + 5 * LANES], cos, sin)
    wr_ref[:, LANES:2 * LANES] = u[:, c + 5 * LANES:c + 6 * LANES]
    rx_ref[...] = u[:, _O_RX:_O_RY]
    ry_ref[...] = u[:, _O_RY:_O_GL]
    gl_ref[...] = u[:, _O_GL:]


def _win_call(x2d, mod3, w, cos, sin, tm, tps):
    m = x2d.shape[0]
    row = lambda width: pl.BlockSpec((tm, width), lambda i: (i, 0))
    widths = (Q_COLS, 4 * LANES, 2 * LANES, RG_WIDTH, RG_WIDTH, LANES)
    return pl.pallas_call(
        _win_kernel, grid=(m // tm,),
        in_specs=[row(D_MODEL), _mod_spec(mod3, tm, tps), _const_spec(w.shape),
                  pl.BlockSpec((tm, LANES), lambda i: (i % tps, 0)),
                  pl.BlockSpec((tm, LANES), lambda i: (i % tps, 0))],
        out_specs=[row(wd) for wd in widths],
        out_shape=[jax.ShapeDtypeStruct((m, wd), F32) for wd in widths],
        compiler_params=_cparams(1), name="mixer_in")(x2d, mod3, w, cos, sin)


def _compress_core(k_ref, v_ref, w4_ref, pe4_ref, cos_ref, sin_ref, out_ref, nb):
    half = nb // 2
    acc = jnp.zeros((nb, 2 * LANES), F32)
    for l in range(CMP_BLOCK):
        even = pl.ds(l, half, stride=2 * CMP_BLOCK)
        odd = pl.ds(CMP_BLOCK + l, half, stride=2 * CMP_BLOCK)
        xk = jnp.concatenate([k_ref[even, :], k_ref[odd, :]], axis=0)
        xv = jnp.concatenate([v_ref[even, :], v_ref[odd, :]], axis=0)
        x = (jnp.concatenate([xk, xv], axis=1) + pe4_ref[pl.ds(l, 1), :]).astype(BF16)
        acc = acc + _dot(x, w4_ref[l])
    out_ref[:, 0:LANES] = _rope128(acc[:, 0:LANES], cos_ref[...], sin_ref[...])
    out_ref[:, LANES:2 * LANES] = acc[:, LANES:2 * LANES]


def _compress_prompt_kernel(k_ref, v_ref, w4_ref, pe4_ref, cos_ref, sin_ref, out_ref, *, nb):
    _compress_core(k_ref, v_ref, w4_ref, pe4_ref, cos_ref, sin_ref, out_ref, nb)


def _compress_prompt(kv4, bsz, t, w4, pe4, ccos, csin):
    nb = t // CMP_BLOCK
    return pl.pallas_call(
        functools.partial(_compress_prompt_kernel, nb=nb), grid=(bsz,),
        in_specs=[pl.BlockSpec((t, LANES), lambda b: (b, 0)), pl.BlockSpec((t, LANES), lambda b: (b, 1)),
                  _const_spec(w4.shape), _const_spec(pe4.shape),
                  _const_spec(ccos.shape), _const_spec(csin.shape)],
        out_specs=pl.BlockSpec((nb, 2 * LANES), lambda b: (b, 0)),
        out_shape=jax.ShapeDtypeStruct((bsz * nb, 2 * LANES), F32),
        compiler_params=_cparams(1), name="compress_prompt")(kv4, kv4, w4, pe4, ccos, csin)


def _page_copy(cache_ref, buf_ref, sem, layer, page, slot, col0, width):
    return pltpu.make_async_copy(
        cache_ref.at[layer, page, :, pl.ds(col0, width)],
        buf_ref.at[pl.ds(pl.multiple_of(slot * PAGE_SIZE, PAGE_SIZE), PAGE_SIZE), :], sem)


def _gather_pages(pt_ref, cache_ref, bufs, sem, b, layer, col0, npages):
    def copies(j):
        page = pt_ref[b * npages + j]
        out = []
        c0 = col0
        for buf in bufs:
            out.append(_page_copy(cache_ref, buf, sem, layer, page, j, c0, buf.shape[1]))
            c0 += buf.shape[1]
        return out

    def start(j, c):
        for cp in copies(j):
            cp.start()
        return c
    lax.fori_loop(0, npages, start, 0)

    def wait(j, c):
        for cp in copies(j):
            cp.wait()
        return c
    lax.fori_loop(0, npages, wait, 0)


def _compress_sample_kernel(pt_ref, cache_ref, w4_ref, pe4_ref, cos_ref, sin_ref, out_ref, kbuf, vbuf, sem,
                            *, layer, npages):
    _gather_pages(pt_ref, cache_ref, (kbuf, vbuf), sem, pl.program_id(0), layer, 0, npages)
    _compress_core(kbuf, vbuf, w4_ref, pe4_ref, cos_ref, sin_ref, out_ref, npages * PAGE_SIZE // CMP_BLOCK)


def _compress_sample(pt_flat, cache4, layer, bsz, npages, w4, pe4, ccos, csin):
    past = npages * PAGE_SIZE
    nb = past // CMP_BLOCK
    grid_spec = pltpu.PrefetchScalarGridSpec(
        num_scalar_prefetch=1, grid=(bsz,),
        in_specs=[pl.BlockSpec(memory_space=pl.ANY),
                  _const_spec(w4.shape), _const_spec(pe4.shape),
                  _const_spec(ccos.shape), _const_spec(csin.shape)],
        out_specs=pl.BlockSpec((nb, 2 * LANES), lambda b, pt: (b, 0)),
        scratch_shapes=[pltpu.VMEM((past, LANES), F32), pltpu.VMEM((past, LANES), F32),
                        pltpu.SemaphoreType.DMA(())])
    return pl.pallas_call(
        functools.partial(_compress_sample_kernel, layer=layer, npages=npages),
        grid_spec=grid_spec,
        out_shape=jax.ShapeDtypeStruct((bsz * nb, 2 * LANES), F32),
        compiler_params=_cparams(1), name="compress_sample")(pt_flat, cache4, w4, pe4, ccos, csin)


def _cmp_positions(nc):
    n = lax.broadcasted_iota(I32, (1, nc), 1)
    half = nc // 2
    blk = jnp.where(n < half, 2 * n, 2 * (n - half) + 1)
    return blk * CMP_BLOCK + (CMP_BLOCK - 1)


def _pattn_kernel(q_ref, gl_ref, cmp_ref, ksv_ref, w0_ref, w1_ref, w2_ref, e_ref, o_ref, *, tq, t_len):
    ti = pl.program_id(1)
    nc = t_len // CMP_BLOCK
    nsel = t_len // SEL_BLOCK
    q0 = ti * tq
    qpos = q0 + lax.broadcasted_iota(I32, (tq, 1), 0)
    qs = (q_ref[...] * ATTN_SCALE).astype(BF16)
    gates = jax.nn.sigmoid(gl_ref[...])
    m_c = _cmp_positions(nc) <= qpos
    causal = lax.broadcasted_iota(I32, (1, t_len), 1) <= qpos
    wpos = q0 - 2 * tq + lax.broadcasted_iota(I32, (1, 3 * tq), 1)
    dpos = qpos - wpos
    m_w = (dpos >= 0) & (dpos < WINDOW) & (wpos >= 0)
    blk = lax.broadcasted_iota(I32, (1, nsel), 1)
    cur = qpos // SEL_BLOCK
    forced = (blk == cur) | (blk == 0)
    future = blk > cur
    for g in range(N_KV_A):
        ck = slice(g * HEAD_DIM, (g + 1) * HEAD_DIM)
        cv = slice(LANES + g * HEAD_DIM, LANES + (g + 1) * HEAD_DIM)
        kc = cmp_ref[:, ck].astype(BF16)
        vc = cmp_ref[:, cv].astype(BF16)
        o_cmp = []
        imp = jnp.zeros((tq, nc), F32)
        for r in range(HEADS_PER_GROUP):
            h = g * HEADS_PER_GROUP + r
            qh = qs[:, h * HEAD_DIM:(h + 1) * HEAD_DIM]
            (e,), inv = _softmax_parts([(_dot_nt(qh, kc), m_c)])
            p = e * inv
            o_cmp.append(_dot(p.astype(BF16), vc))
            imp = imp + p
        imp = imp[:, :nc // 2] + imp[:, nc // 2:]
        vals = jnp.where(forced, FORCE_SCORE, jnp.where(future, -1.0, imp))
        sel = _topk_mask(vals, nsel, min(TOP_N, nsel))
        sel_keys = _dot(jnp.where(sel, 1.0, 0.0).astype(BF16), e_ref[...])
        m_s = (sel_keys > 0.5) & causal
        ks = ksv_ref[:, ck].astype(BF16)
        vs = ksv_ref[:, cv].astype(BF16)
        kw = jnp.concatenate([w0_ref[:, ck], w1_ref[:, ck], w2_ref[:, ck]], axis=0).astype(BF16)
        vw = jnp.concatenate([w0_ref[:, cv], w1_ref[:, cv], w2_ref[:, cv]], axis=0).astype(BF16)
        for r in range(HEADS_PER_GROUP):
            h = g * HEADS_PER_GROUP + r
            qh = qs[:, h * HEAD_DIM:(h + 1) * HEAD_DIM]
            o_s = _attend(qh, [(ks, vs, m_s)])
            o_w = _attend(qh, [(kw, vw, m_w)])
            o_ref[:, h * HEAD_DIM:(h + 1) * HEAD_DIM] = (
                gates[:, 3 * h:3 * h + 1] * o_cmp[r] + gates[:, 3 * h + 1:3 * h + 2] * o_s
                + gates[:, 3 * h + 2:3 * h + 3] * o_w)


def _block_expand(nblk, nkeys):
    return (jnp.arange(nkeys)[None, :] // SEL_BLOCK == jnp.arange(nblk)[:, None]).astype(BF16)


def _pattn(q, gl, cmp, kv4, wr, bsz, t):
    tq = Q_TILE
    nt = t // tq
    nc = t // CMP_BLOCK
    expand = _block_expand(t // SEL_BLOCK, t)
    row = lambda width: pl.BlockSpec((tq, width), lambda b, i: (b * nt + i, 0))
    wspec = lambda back: pl.BlockSpec((tq, 2 * LANES), lambda b, i: (b * nt + jnp.maximum(i - back, 0), 0))
    return pl.pallas_call(
        functools.partial(_pattn_kernel, tq=tq, t_len=t), grid=(bsz, nt),
        in_specs=[row(Q_COLS), row(LANES),
                  pl.BlockSpec((nc, 2 * LANES), lambda b, i: (b, 0)),
                  pl.BlockSpec((t, 2 * LANES), lambda b, i: (b, 1)),
                  wspec(2), wspec(1), wspec(0), _const_spec(expand.shape)],
        out_specs=row(NSA_WIDTH),
        out_shape=jax.ShapeDtypeStruct((bsz * t, NSA_WIDTH), F32),
        compiler_params=_cparams(2), name="nsa_prompt")(q, gl, cmp, kv4, wr, wr, wr, expand)


def _sattn_kernel(pt_ref, cache_ref, q_ref, gl_ref, cmp_ref, kvn_ref, wrn_ref, win_ref, e_ref, o_ref,
                  buf, sem, *, layer, npages, ts):
    b = pl.program_id(0)
    past = npages * PAGE_SIZE
    _gather_pages(pt_ref, cache_ref, (buf,), sem, b, layer, 2 * LANES, npages)
    nc = past // CMP_BLOCK
    nblk_past = past // SEL_BLOCK
    nsel = nblk_past + 1
    lanes_sel = 2 * nblk_past
    rows = HEADS_PER_GROUP * ts
    tok1 = lax.broadcasted_iota(I32, (ts, 1), 0)
    tok = jnp.concatenate([tok1] * HEADS_PER_GROUP, axis=0)
    qpos1 = past + tok1
    qpos = past + tok
    qs = q_ref[...] * ATTN_SCALE
    gates = jax.nn.sigmoid(gl_ref[...])
    m_c = _cmp_positions(nc) <= qpos
    blk = lax.broadcasted_iota(I32, (1, lanes_sel), 1)
    cur = qpos1 // SEL_BLOCK
    forced = (blk == cur) | (blk == 0)
    future = blk > cur
    m_past = lax.broadcasted_iota(I32, (1, past), 1) <= qpos
    tkey = lax.broadcasted_iota(I32, (1, ts), 1)
    m_new = tkey <= tok
    win_len = win_ref.shape[0]
    wpos = past - win_len + lax.broadcasted_iota(I32, (1, win_len), 1)
    dpos = qpos - wpos
    m_wstate = (dpos >= 0) & (dpos < WINDOW) & (wpos >= 0)
    dnew = tok - tkey
    m_wnew = (dnew >= 0) & (dnew < WINDOW)
    for g in range(N_KV_A):
        ck = slice(g * HEAD_DIM, (g + 1) * HEAD_DIM)
        cv = slice(LANES + g * HEAD_DIM, LANES + (g + 1) * HEAD_DIM)
        heads = [g * HEADS_PER_GROUP + r for r in range(HEADS_PER_GROUP)]
        qg = jnp.concatenate([qs[:, h * HEAD_DIM:(h + 1) * HEAD_DIM] for h in heads], axis=0).astype(BF16)
        kc = cmp_ref[:, ck].astype(BF16)
        vc = cmp_ref[:, cv].astype(BF16)
        (e,), inv = _softmax_parts([(_dot_nt(qg, kc), m_c)])
        p = e * inv
        o_c = _dot(p.astype(BF16), vc)
        imp = p[0:ts]
        for r in range(1, HEADS_PER_GROUP):
            imp = imp + p[r * ts:(r + 1) * ts]
        imp = imp[:, :nc // 2] + imp[:, nc // 2:]
        imp = jnp.concatenate([imp, jnp.zeros((ts, lanes_sel - nc // 2), F32)], axis=1)
        vals = jnp.where(forced, FORCE_SCORE, jnp.where(future, -1.0, imp))
        sel = jnp.where(_topk_mask(vals, nsel, min(TOP_N, nsel)), 1.0, 0.0)
        sel = jnp.concatenate([sel] * HEADS_PER_GROUP, axis=0)
        sel_keys = _dot(sel[:, :nblk_past].astype(BF16), e_ref[...])
        m_s_past = (sel_keys > 0.5) & m_past
        m_s_new = (sel[:, nblk_past:nblk_past + 1] > 0.5) & m_new
        kn = kvn_ref[:, 2 * LANES:4 * LANES]
        o_s = _attend(qg, [(buf[:, ck].astype(BF16), buf[:, cv].astype(BF16), m_s_past),
                           (kn[:, ck].astype(BF16), kn[:, cv].astype(BF16), m_s_new)])
        o_w = _attend(qg, [(win_ref[:, ck].astype(BF16), win_ref[:, cv].astype(BF16), m_wstate),
                           (wrn_ref[:, ck].astype(BF16), wrn_ref[:, cv].astype(BF16), m_wnew)])
        for r, h in enumerate(heads):
            rs = slice(r * ts, (r + 1) * ts)
            o_ref[:, h * HEAD_DIM:(h + 1) * HEAD_DIM] = (
                gates[:, 3 * h:3 * h + 1] * o_c[rs] + gates[:, 3 * h + 1:3 * h + 2] * o_s[rs]
                + gates[:, 3 * h + 2:3 * h + 3] * o_w[rs])


def _sattn(pt_flat, cache4, layer, q, gl, cmp, kv4, wr, win4, bsz, ts, npages):
    past = npages * PAGE_SIZE
    nc = past // CMP_BLOCK
    win_len = win4.shape[2]
    expand = _block_expand(past // SEL_BLOCK, past)
    row = lambda width: pl.BlockSpec((ts, width), lambda b, pt: (b, 0))
    grid_spec = pltpu.PrefetchScalarGridSpec(
        num_scalar_prefetch=1, grid=(bsz,),
        in_specs=[pl.BlockSpec(memory_space=pl.ANY), row(Q_COLS), row(LANES),
                  pl.BlockSpec((nc, 2 * LANES), lambda b, pt: (b, 0)),
                  row(4 * LANES), row(2 * LANES),
                  pl.BlockSpec((None, None, win_len, 2 * LANES), lambda b, pt: (layer, b, 0, 0)),
                  _const_spec(expand.shape)],
        out_specs=row(NSA_WIDTH),
        scratch_shapes=[pltpu.VMEM((past, 2 * LANES), F32), pltpu.SemaphoreType.DMA(())])
    return pl.pallas_call(
        functools.partial(_sattn_kernel, layer=layer, npages=npages, ts=ts),
        grid_spec=grid_spec,
        out_shape=jax.ShapeDtypeStruct((bsz * ts, NSA_WIDTH), F32),
        compiler_params=_cparams(1), name="nsa_sample")(pt_flat, cache4, q, gl, cmp, kv4, wr, win4, expand)


RG_STATE_ROWS = SUBLANES
RG_ROWS = 256


def _rglru_kernel(rx_ref, ry_ref, st_ref, cw_ref, cb_ref, wa_ref, ba_ref, wx_ref, bx_ref, lam_ref,
                  out_ref, hl_ref, nb_ref, xs, a_s, b_s, *, t_len):
    ch = min(RG_ROWS, t_len)
    xs[0:RG_STATE_ROWS, :] = st_ref[0]
    xs[RG_STATE_ROWS:RG_STATE_ROWS + t_len, :] = rx_ref[...]
    lam = lam_ref[...]
    softplus_neg_lam = jnp.maximum(-lam, 0.0) + jnp.log1p(jnp.exp(-jnp.abs(lam)))
    sub = lax.broadcasted_iota(I32, (ch, RG_WIDTH), 0) & (SUBLANES - 1)
    for c in range(t_len // ch):
        r0 = c * ch
        xc = cb_ref[...]
        for j in range(RG_CONV):
            xc = xc + cw_ref[j:j + 1, :] * xs[r0 + RG_STATE_ROWS - (RG_CONV - 1) + j:
                                              r0 + RG_STATE_ROWS - (RG_CONV - 1) + j + ch, :]
        xb = xc.astype(BF16)
        r = jax.nn.sigmoid(_dot(xb, wa_ref[...]) + ba_ref[...])
        i = jax.nn.sigmoid(_dot(xb, wx_ref[...]) + bx_ref[...])
        log_a = -RG_C * r * softplus_neg_lam
        a = jnp.exp(log_a)
        one_minus_a2 = -jnp.tanh(log_a) * (jnp.exp(2.0 * log_a) + 1.0)
        bb = jnp.sqrt(one_minus_a2) * (i * xc)
        for s in (1, 2, 4):
            ok = sub >= s
            a_prev = pltpu.roll(a, s, 0)
            b_prev = pltpu.roll(bb, s, 0)
            bb = jnp.where(ok, a * b_prev + bb, bb)
            a = jnp.where(ok, a * a_prev, a)
        a_s[r0:r0 + ch, :] = a
        b_s[r0:r0 + ch, :] = bb

    def step(k, h):
        r0 = pl.multiple_of(k * SUBLANES, SUBLANES)
        hk = b_s[pl.ds(r0, SUBLANES), :] + a_s[pl.ds(r0, SUBLANES), :] * h
        b_s[pl.ds(r0, SUBLANES), :] = hk
        return jnp.broadcast_to(hk[SUBLANES - 1:SUBLANES, :], (SUBLANES, RG_WIDTH))

    h0 = jnp.broadcast_to(st_ref[0, 0:1, :], (SUBLANES, RG_WIDTH))
    h_fin = lax.fori_loop(0, t_len // SUBLANES, step, h0)
    hl_ref[0] = h_fin[0:1, :]
    nb_ref[0] = xs[RG_STATE_ROWS + t_len - (RG_CONV - 1):RG_STATE_ROWS + t_len, :]
    for c in range(t_len // ch):
        r0 = c * ch
        out_ref[r0:r0 + ch, :] = b_s[r0:r0 + ch, :] * jax.nn.gelu(ry_ref[r0:r0 + ch, :])


def _rglru(rx, ry, st, cw, cb, wa, ba, wx, bx, lam, bsz, t):
    row = pl.BlockSpec((t, RG_WIDTH), lambda b: (b, 0))
    vec = _const_spec((1, RG_WIDTH))
    return pl.pallas_call(
        functools.partial(_rglru_kernel, t_len=t), grid=(bsz,),
        in_specs=[row, row, pl.BlockSpec((1, RG_STATE_ROWS, RG_WIDTH), lambda b: (b, 0, 0)),
                  _const_spec(cw.shape), vec, _const_spec(wa.shape), vec, _const_spec(wx.shape), vec, vec],
        out_specs=[row, pl.BlockSpec((1, 1, RG_WIDTH), lambda b: (b, 0, 0)),
                   pl.BlockSpec((1, RG_CONV - 1, RG_WIDTH), lambda b: (b, 0, 0))],
        out_shape=[jax.ShapeDtypeStruct((bsz * t, RG_WIDTH), F32),
                   jax.ShapeDtypeStruct((bsz, 1, RG_WIDTH), F32),
                   jax.ShapeDtypeStruct((bsz, RG_CONV - 1, RG_WIDTH), F32)],
        scratch_shapes=[pltpu.VMEM((RG_STATE_ROWS + t, RG_WIDTH), F32),
                        pltpu.VMEM((t, RG_WIDTH), F32), pltpu.VMEM((t, RG_WIDTH), F32)],
        compiler_params=_cparams(1), name="rglru")(rx, ry, st, cw, cb, wa, ba, wx, bx, lam)


def _proj_ln_kernel(*refs, n_in):
    a_refs = refs[:n_in]
    w_refs = refs[n_in:2 * n_in]
    x_ref, mod_ref, bias_ref, g_ref, b_ref, o_ref = refs[2 * n_in:]
    out = bias_ref[...]
    for a_ref, w_ref in zip(a_refs, w_refs):
        out = out + _dot(a_ref[...].astype(BF16), w_ref[...])
    o_ref[...] = _post_norm(x_ref[...], mod_ref[0], out, g_ref[...], b_ref[...])


def _proj_ln(a_list, w_list, x2d, mod3, bias, g, b, tm, tps):
    m = x2d.shape[0]
    n_in = len(a_list)
    row = lambda width: pl.BlockSpec((tm, width), lambda i: (i, 0))
    vec = _const_spec((1, D_MODEL))
    return pl.pallas_call(
        functools.partial(_proj_ln_kernel, n_in=n_in), grid=(m // tm,),
        in_specs=[row(a.shape[1]) for a in a_list] + [_const_spec(w.shape) for w in w_list]
        + [row(D_MODEL), _mod_spec(mod3, tm, tps), vec, vec, vec],
        out_specs=row(D_MODEL),
        out_shape=jax.ShapeDtypeStruct((m, D_MODEL), F32),
        compiler_params=_cparams(1), name="proj_postnorm")(*a_list, *w_list, x2d, mod3, bias, g, b)


def _ffn_kernel(x_ref, mod_ref, w1_ref, w3_ref, w2_ref, g_ref, b_ref, o_ref, *, ff):
    x = x_ref[...]
    mod = mod_ref[0]
    h = _modulate(x, mod).astype(BF16)
    acc = jnp.zeros(x.shape, F32)
    for c in range(ff // FF_CHUNK):
        cs = slice(c * FF_CHUNK, (c + 1) * FF_CHUNK)
        z = _silu(_dot(h, w1_ref[:, cs])) * _dot(h, w3_ref[:, cs])
        acc = acc + _dot(z.astype(BF16), w2_ref[cs, :])
    o_ref[...] = _post_norm(x, mod, acc, g_ref[...], b_ref[...])


def _ffn(x2d, mod3, w1, w3, w2, g, b, tm, tps):
    m = x2d.shape[0]
    row = pl.BlockSpec((tm, D_MODEL), lambda i: (i, 0))
    vec = _const_spec((1, D_MODEL))
    return pl.pallas_call(
        functools.partial(_ffn_kernel, ff=w1.shape[1]), grid=(m // tm,),
        in_specs=[row, _mod_spec(mod3, tm, tps), _const_spec(w1.shape, True), _const_spec(w3.shape, True),
                  _const_spec(w2.shape, True), vec, vec],
        out_specs=row,
        out_shape=jax.ShapeDtypeStruct((m, D_MODEL), F32),
        compiler_params=_cparams(1), name="dense_ffn")(x2d, mod3, w1, w3, w2, g, b)


def _cf_in_kernel(x_ref, mod_ref, w_ref, b_ref, o_ref):
    h = _modulate(x_ref[...], mod_ref[0]).astype(BF16)
    u = _dot(h, w_ref[...]) + b_ref[...]
    o_ref[...] = u[:, :D_MODEL] * jax.nn.sigmoid(u[:, D_MODEL:])


def _cf_in(x2d, mod3, w, bias, tm, tps):
    m = x2d.shape[0]
    row = pl.BlockSpec((tm, D_MODEL), lambda i: (i, 0))
    return pl.pallas_call(
        _cf_in_kernel, grid=(m // tm,),
        in_specs=[row, _mod_spec(mod3, tm, tps), _const_spec(w.shape), _const_spec(bias.shape)],
        out_specs=row,
        out_shape=jax.ShapeDtypeStruct((m, D_MODEL), F32),
        compiler_params=_cparams(1), name="conformer_in")(x2d, mod3, w, bias)


def _cf_conv_kernel(x_ref, halo_ref, dw_ref, db_ref, g_ref, b_ref, z_ref, s_ref, *, tt, zero_first):
    halo = halo_ref[...]
    if zero_first:
        halo = jnp.where(pl.program_id(1) == 0, 0.0, halo)
    s_ref[0:CF_HALO, :] = halo
    s_ref[CF_HALO:CF_HALO + tt, :] = x_ref[...]
    sub = min(CONV_SUB, tt)
    first = CF_HALO - (CF_KERNEL - 1)
    for c in range(tt // sub):
        r0 = c * sub
        y = jnp.broadcast_to(db_ref[...], (sub, D_MODEL))
        for j in range(CF_KERNEL):
            y = y + dw_ref[j:j + 1, :] * s_ref[r0 + first + j:r0 + first + j + sub, :]
        z_ref[r0:r0 + sub, :] = _silu(_layer_norm(y, g_ref[...], b_ref[...]))


def _cf_conv(glu, halo_src, halo_map, dw, db, g, b, bsz, t, tt, zero_first):
    nt = t // tt
    vec = _const_spec((1, D_MODEL))
    return pl.pallas_call(
        functools.partial(_cf_conv_kernel, tt=tt, zero_first=zero_first), grid=(bsz, nt),
        in_specs=[pl.BlockSpec((tt, D_MODEL), lambda bi, ti: (bi * nt + ti, 0)),
                  pl.BlockSpec((CF_HALO, D_MODEL), halo_map),
                  _const_spec(dw.shape), vec, vec, vec],
        out_specs=pl.BlockSpec((tt, D_MODEL), lambda bi, ti: (bi * nt + ti, 0)),
        out_shape=jax.ShapeDtypeStruct((bsz * t, D_MODEL), F32),
        scratch_shapes=[pltpu.VMEM((CF_HALO + tt, D_MODEL), F32)],
        compiler_params=_cparams(2), name="conformer_conv")(glu, halo_src, dw, db, g, b)


def _router_kernel(x_ref, mod_ref, rw_ref, h_ref, ids_ref, gates_ref):
    h = _modulate(x_ref[...], mod_ref[0])
    h_ref[...] = h
    logits = lax.dot_general(h, rw_ref[...], (((1,), (0,)), ((), ())), precision=lax.Precision.HIGHEST,
                             preferred_element_type=F32)
    lane = lax.broadcasted_iota(I32, logits.shape, 1)
    logits = jnp.where(lane < N_EXPERTS, logits, -jnp.inf)
    m1 = jnp.max(logits, axis=-1, keepdims=True)
    i1 = jnp.min(jnp.where(logits == m1, lane, LANES), axis=-1, keepdims=True)
    rest = jnp.where(lane == i1, -jnp.inf, logits)
    m2 = jnp.max(rest, axis=-1, keepdims=True)
    i2 = jnp.min(jnp.where(rest == m2, lane, LANES), axis=-1, keepdims=True)
    e2 = jnp.exp(m2 - m1)
    inv = 1.0 / (1.0 + e2)
    col = lax.broadcasted_iota(I32, ids_ref.shape, 1)
    ids_ref[...] = jnp.where(col == 0, i1, jnp.where(col == 1, i2, 0))
    gates_ref[...] = jnp.where(col == 0, inv, jnp.where(col == 1, e2 * inv, 0.0))


def _router(x2d, mod3, rw, tm, tps, n_total, row_off, prev):
    m = x2d.shape[0]
    off = row_off // tm
    outs = [jax.ShapeDtypeStruct((n_total, D_MODEL), F32), jax.ShapeDtypeStruct((n_total, SUBLANES), I32),
            jax.ShapeDtypeStruct((n_total, SUBLANES), F32)]
    ospec = lambda width: pl.BlockSpec((tm, width), lambda i: (i + off, 0))
    in_specs = [pl.BlockSpec((tm, D_MODEL), lambda i: (i, 0)), _mod_spec(mod3, tm, tps), _const_spec(rw.shape)]
    args = [x2d, mod3, rw]
    aliases = {}
    kern = _router_kernel
    if prev is not None:
        in_specs += [pl.BlockSpec(memory_space=pl.ANY)] * 3
        args += list(prev)
        aliases = {3: 0, 4: 1, 5: 2}
        kern = lambda x, md, rw_, p0, p1, p2, h, ids, gt: _router_kernel(x, md, rw_, h, ids, gt)
    return pl.pallas_call(
        kern, grid=(m // tm,), in_specs=in_specs,
        out_specs=[ospec(D_MODEL), ospec(SUBLANES), ospec(SUBLANES)],
        out_shape=outs, input_output_aliases=aliases,
        compiler_params=_cparams(1), name="moe_router")(*args)


def _row_copy(src_ref, dst_ref, sem, src_row, dst_row):
    return pltpu.make_async_copy(src_ref.at[pl.ds(src_row, 1), :], dst_ref.at[pl.ds(dst_row, 1), :], sem)


def _gather_rows(idx_ref, idx0, stride, src_ref, dst_ref, sem, n):
    def start(r, c):
        _row_copy(src_ref, dst_ref, sem, idx_ref[idx0 + r * stride], r).start()
        return c
    lax.fori_loop(0, n, start, 0)

    def wait(r, c):
        _row_copy(src_ref, dst_ref, sem, idx_ref[idx0 + r * stride], r).wait()
        return c
    lax.fori_loop(0, n, wait, 0)


def _experts_kernel(blk_e_ref, row_tok_ref, nused_ref, h_ref, w1_ref, w3_ref, w2_ref, y_ref,
                    xs, xb, acc, sem):
    i = pl.program_id(0)
    j = pl.program_id(1)
    used = i < nused_ref[0]

    @pl.when(used & (j == 0))
    def _():
        _gather_rows(row_tok_ref, i * MOE_TILE, 1, h_ref, xs, sem, MOE_TILE)
        xb[...] = xs[...].astype(BF16)
        acc[...] = jnp.zeros(acc.shape, F32)

    @pl.when(used)
    def _():
        x = xb[...]
        z = _silu(_dot(x, w1_ref[...])) * _dot(x, w3_ref[...])
        acc[...] += _dot(z.astype(BF16), w2_ref[...])

    last = j == pl.num_programs(1) - 1

    @pl.when(used & last)
    def _():
        y_ref[...] = acc[...]

    @pl.when(jnp.logical_not(used) & last)
    def _():
        y_ref[...] = jnp.zeros(y_ref.shape, F32)


def _experts(blk_e, row_tok, nused, h_all, w1, w3, w2, n_blk):
    ff = w1.shape[2]
    nj = ff // MOE_FF_CHUNK

    def jj(i, j, nu):
        return jnp.where(i < nu[0], j, nj - 1)

    grid_spec = pltpu.PrefetchScalarGridSpec(
        num_scalar_prefetch=3, grid=(n_blk, nj),
        in_specs=[pl.BlockSpec(memory_space=pl.ANY),
                  pl.BlockSpec((None, D_MODEL, MOE_FF_CHUNK), lambda i, j, be, rt, nu: (be[i], 0, jj(i, j, nu))),
                  pl.BlockSpec((None, D_MODEL, MOE_FF_CHUNK), lambda i, j, be, rt, nu: (be[i], 0, jj(i, j, nu))),
                  pl.BlockSpec((None, MOE_FF_CHUNK, D_MODEL), lambda i, j, be, rt, nu: (be[i], jj(i, j, nu), 0))],
        out_specs=pl.BlockSpec((MOE_TILE, D_MODEL), lambda i, j, be, rt, nu: (i, 0)),
        scratch_shapes=[pltpu.VMEM((MOE_TILE, D_MODEL), F32), pltpu.VMEM((MOE_TILE, D_MODEL), BF16),
                        pltpu.VMEM((MOE_TILE, D_MODEL), F32), pltpu.SemaphoreType.DMA(())])
    return pl.pallas_call(
        _experts_kernel, grid_spec=grid_spec,
        out_shape=jax.ShapeDtypeStruct((n_blk * MOE_TILE, D_MODEL), F32),
        compiler_params=_cparams(2), name="moe_experts")(blk_e, row_tok, nused, h_all, w1, w3, w2)


def _combine_kernel(dest_ref, y_ref, gates_ref, x_ref, mod_ref, g_ref, b_ref, o_ref, y1, y2, sem,
                    *, tm, tok_off):
    i = pl.program_id(0)
    base = 2 * (tok_off + i * tm)
    _gather_rows(dest_ref, base, 2, y_ref, y1, sem, tm)
    _gather_rows(dest_ref, base + 1, 2, y_ref, y2, sem, tm)
    gates = gates_ref[...]
    out = gates[:, 0:1] * y1[...] + gates[:, 1:2] * y2[...]
    o_ref[...] = _post_norm(x_ref[...], mod_ref[0], out, g_ref[...], b_ref[...])


def _combine(dest, ys, gates_all, x2d, mod3, g, b, tm, tps, tok_off):
    m = x2d.shape[0]
    off = tok_off // tm
    if mod3.shape[1] == 1:
        mspec = pl.BlockSpec((1, 1, mod3.shape[2]), lambda i, d: (i // tps, 0, 0))
    else:
        mspec = pl.BlockSpec((1, tm, mod3.shape[2]), lambda i, d: (0, i, 0))
    vec = _const_spec((1, D_MODEL))
    grid_spec = pltpu.PrefetchScalarGridSpec(
        num_scalar_prefetch=1, grid=(m // tm,),
        in_specs=[pl.BlockSpec(memory_space=pl.ANY),
                  pl.BlockSpec((tm, SUBLANES), lambda i, d: (i + off, 0)),
                  pl.BlockSpec((tm, D_MODEL), lambda i, d: (i, 0)), mspec, vec, vec],
        out_specs=pl.BlockSpec((tm, D_MODEL), lambda i, d: (i, 0)),
        scratch_shapes=[pltpu.VMEM((tm, D_MODEL), F32), pltpu.VMEM((tm, D_MODEL), F32),
                        pltpu.SemaphoreType.DMA(())])
    return pl.pallas_call(
        functools.partial(_combine_kernel, tm=tm, tok_off=tok_off), grid_spec=grid_spec,
        out_shape=jax.ShapeDtypeStruct((m, D_MODEL), F32),
        compiler_params=_cparams(1), name="moe_combine")(dest, ys, gates_all, x2d, mod3, g, b)


def _route_plan(ids_all, n_tok):
    fe = ids_all[:, :2].reshape(-1)
    onehot = (fe[:, None] == jnp.arange(N_EXPERTS, dtype=I32)[None, :]).astype(I32)
    csum = jnp.cumsum(onehot, axis=0)
    rank = jnp.take_along_axis(csum, fe[:, None], axis=1)[:, 0] - 1
    counts = csum[-1]
    padded = (counts + MOE_TILE - 1) // MOE_TILE * MOE_TILE
    pend = jnp.cumsum(padded)
    pstart = pend - padded
    dest = (pstart[fe] + rank).astype(I32)
    n_blk = -(-(2 * n_tok) // MOE_TILE) + N_EXPERTS
    row_tok = jnp.zeros((n_blk * MOE_TILE,), I32).at[dest].set(jnp.arange(2 * n_tok, dtype=I32) // 2)
    nused = (pend[-1] // MOE_TILE).astype(I32)
    blk = jnp.arange(n_blk, dtype=I32)
    blk_e = jnp.searchsorted(pend, jnp.minimum(blk, nused - 1) * MOE_TILE, side='right').astype(I32)
    blk_e = jnp.minimum(blk_e, N_EXPERTS - 1)
    return dest, row_tok, blk_e, nused.reshape(1), n_blk


def _rope_tables(pos):
    half = HEAD_DIM // 2
    inv = 1.0 / (ROPE_THETA ** (jnp.arange(half, dtype=F32) * (2.0 / HEAD_DIM)))
    ang = pos.astype(F32)[:, None] * inv[None, :]
    c = jnp.cos(ang)
    s = jnp.sin(ang)
    return jnp.concatenate([c, c, c, c], axis=1), jnp.concatenate([-s, s, -s, s], axis=1)


def _cmp_rope_tables(nc):
    blk = jnp.concatenate([jnp.arange(0, nc, 2), jnp.arange(1, nc, 2)])
    return _rope_tables(blk * CMP_BLOCK + (CMP_BLOCK - 1))


def _w_in_layout(w_in):
    o1 = Q_COLS
    o2 = o1 + KV_COLS
    o3 = o2 + GATE_COLS
    pad = jnp.zeros((D_MODEL, LANES - GATE_COLS), w_in.dtype)
    return jnp.concatenate([w_in[:, :o2], w_in[:, o3:], w_in[:, o2:o3], pad], axis=1).astype(BF16)


def _block_diag(blocks):
    n, a, b = blocks.shape
    eye = jnp.eye(n, dtype=blocks.dtype)
    return (eye[:, None, :, None] * blocks[:, :, None, :]).reshape(n * a, n * b)


def _cmp_weights(w_ck, w_cv, pe_k, pe_v):
    wk = w_ck.reshape(CMP_BLOCK, HEAD_DIM, HEAD_DIM)
    wv = w_cv.reshape(CMP_BLOCK, HEAD_DIM, HEAD_DIM)
    w4 = jax.vmap(lambda a, b: _block_diag(jnp.stack([a, a, b, b])))(wk, wv).astype(BF16)
    pe4 = jnp.concatenate([pe_k, pe_k, pe_v, pe_v], axis=1)
    return w4, pe4


def kernel(x_prompt, x_sample, c_prompt, c_sample, page_table, cache_nsa_kv, state_nsa_win, state_rglru_h, state_rglru_conv, state_conformer_conv, w_mod, b_mod, ln_g, ln_b, w_in_even, w_out_even, w_cmp_k, w_cmp_v, pe_cmp_k, pe_cmp_v, rg_conv_w, rg_conv_b, rg_wa, rg_ba, rg_wx, rg_bx, rg_lam, cf_w1, cf_b1, cf_dw, cf_db, cf_ln_g, cf_ln_b, cf_w2, cf_b2, ff_w1, ff_w3, ff_w2, moe_router, moe_w1, moe_w3, moe_w2):
    bp, tp, d = x_prompt.shape
    bs, ts, _ = x_sample.shape
    npages = page_table.shape[1]
    past = npages * PAGE_SIZE
    n_p = bp * tp
    n_s = bs * ts
    n_tok = n_p + n_s
    assert d == D_MODEL and tp % ROW_TILE == 0 and tp % Q_TILE == 0 and tp >= WINDOW
    assert past % SEL_BLOCK == 0 and ts <= CMP_BLOCK and ts % SUBLANES == 0 and n_s % SUBLANES == 0
    assert n_p % MOE_TILE == 0 and n_p % n_s == 0
    tps_p = tp // ROW_TILE
    xp = x_prompt.reshape(n_p, d)
    xs = x_sample.reshape(n_s, d)
    vec = lambda v: v.reshape(1, -1)

    mod_all = _mod_all(jnp.concatenate([c_prompt, c_sample], axis=0), w_mod, b_mod)

    def mods(l, s):
        mrow = mod_all[2 * l + s]
        return mrow[:bp].reshape(bp, 1, 3 * d), jnp.repeat(mrow[bp:], ts, axis=0).reshape(1, n_s, 3 * d)

    cos_p, sin_p = _rope_tables(jnp.arange(tp))
    cos_s, sin_s = _rope_tables(jnp.tile(past + jnp.arange(ts), bs))
    ccos_p, csin_p = _cmp_rope_tables(tp // CMP_BLOCK)
    ccos_s, csin_s = _cmp_rope_tables(past // CMP_BLOCK)
    pt_flat = page_table.reshape(-1).astype(I32)
    n_even = cache_nsa_kv.shape[0]
    cache4 = cache_nsa_kv.reshape(n_even, cache_nsa_kv.shape[1], PAGE_SIZE, 4 * LANES)
    win4 = state_nsa_win.reshape(n_even, bs, state_nsa_win.shape[2], 2 * LANES)
    win_len = win4.shape[2]

    kv_p, kv_s, win_p, win_s, rh_p, rh_s, rc_p, rc_s, cc_p, cc_s = ([] for _ in range(10))
    for l in range(DEPTH):
        i = l // 2
        mp0, ms0 = mods(l, 0)
        mp1, ms1 = mods(l, 1)
        g0, b0, g1, b1 = vec(ln_g[l, 0]), vec(ln_b[l, 0]), vec(ln_g[l, 1]), vec(ln_b[l, 1])
        zero_bias = jnp.zeros((1, d), F32)
        if l % 2 == 0:
            w_in = _w_in_layout(w_in_even[i])
            w4, pe4 = _cmp_weights(w_cmp_k[i], w_cmp_v[i], pe_cmp_k[i], pe_cmp_v[i])
            wo_a = w_out_even[i][:NSA_WIDTH].astype(BF16)
            wo_r = w_out_even[i][NSA_WIDTH:].astype(BF16)
            wa = _block_diag(rg_wa[i]).astype(BF16)
            wx = _block_diag(rg_wx[i]).astype(BF16)
            rg_args = (rg_conv_w[i], vec(rg_conv_b[i]), wa, vec(rg_ba[i]), wx, vec(rg_bx[i]), vec(rg_lam[i]))

            q, kv4, wr, rx, ry, gl = _win_call(xp, mp0, w_in, cos_p, sin_p, ROW_TILE, tps_p)
            cmp = _compress_prompt(kv4, bp, tp, w4, pe4, ccos_p, csin_p)
            o = _pattn(q, gl, cmp, kv4, wr, bp, tp)
            st = jnp.zeros((bp, RG_STATE_ROWS, RG_WIDTH), F32)
            rg, h_last, new_buf = _rglru(rx, ry, st, *rg_args, bp, tp)
            xp = _proj_ln([o, rg], [wo_a, wo_r], xp, mp0, zero_bias, g0, b0, ROW_TILE, tps_p)
            kv_p.append(kv4.reshape(bp, tp, 4, N_KV_A, HEAD_DIM))
            win_p.append(wr.reshape(bp, tp, 2, N_KV_A, HEAD_DIM)[:, tp - min(WINDOW, tp):])
            rh_p.append(h_last.reshape(bp, RG_WIDTH))
            rc_p.append(new_buf)

            q, kv4, wr, rx, ry, gl = _win_call(xs, ms0, w_in, cos_s, sin_s, n_s, 1)
            cmp = _compress_sample(pt_flat, cache4, i, bs, npages, w4, pe4, ccos_s, csin_s)
            o = _sattn(pt_flat, cache4, i, q, gl, cmp, kv4, wr, win4, bs, ts, npages)
            st = jnp.concatenate([state_rglru_h[i][:, None, :],
                                  jnp.zeros((bs, RG_STATE_ROWS - RG_CONV, RG_WIDTH), F32),
                                  state_rglru_conv[i]], axis=1)
            rg, h_last, new_buf = _rglru(rx, ry, st, *rg_args, bs, ts)
            xs = _proj_ln([o, rg], [wo_a, wo_r], xs, ms0, zero_bias, g0, b0, n_s, 1)
            kv_s.append(kv4.reshape(bs, ts, 4, N_KV_A, HEAD_DIM))
            wfull = jnp.concatenate([win4[i], wr.reshape(bs, ts, 2 * LANES)], axis=1)
            win_s.append(wfull[:, wfull.shape[1] - min(WINDOW, wfull.shape[1]):].reshape(
                bs, -1, 2, N_KV_A, HEAD_DIM))
            rh_s.append(h_last.reshape(bs, RG_WIDTH))
            rc_s.append(new_buf)

            w1, w3, w2 = ff_w1[i].astype(BF16), ff_w3[i].astype(BF16), ff_w2[i].astype(BF16)
            xp = _ffn(xp, mp1, w1, w3, w2, g1, b1, ROW_TILE, tps_p)
            xs = _ffn(xs, ms1, w1, w3, w2, g1, b1, n_s, 1)
        else:
            cw1 = cf_w1[i].astype(BF16)
            cw2 = cf_w2[i].astype(BF16)
            dw = jnp.concatenate([cf_dw[i], jnp.zeros((CF_HALO - CF_KERNEL, d), F32)], axis=0)
            conv_args = (dw, vec(cf_db[i]), vec(cf_ln_g[i]), vec(cf_ln_b[i]))

            glu = _cf_in(xp, mp0, cw1, vec(cf_b1[i]), ROW_TILE, tps_p)
            per = CONV_TILE // CF_HALO
            z = _cf_conv(glu, glu, lambda bi, ti: (jnp.maximum((bi * (tp // CONV_TILE) + ti) * per - 1, 0), 0),
                         *conv_args, bp, tp, CONV_TILE, True)
            xp = _proj_ln([z], [cw2], xp, mp0, vec(cf_b2[i]), g0, b0, ROW_TILE, tps_p)
            cc_p.append(glu.reshape(bp, tp, d)[:, tp - (CF_KERNEL - 1):])

            glu = _cf_in(xs, ms0, cw1, vec(cf_b1[i]), n_s, 1)
            halo = jnp.concatenate([jnp.zeros((bs, CF_HALO - (CF_KERNEL - 1), d), F32),
                                    state_conformer_conv[i]], axis=1).reshape(bs * CF_HALO, d)
            z = _cf_conv(glu, halo, lambda bi, ti: (bi, 0), *conv_args, bs, ts, ts, False)
            xs = _proj_ln([z], [cw2], xs, ms0, vec(cf_b2[i]), g0, b0, n_s, 1)
            cc_s.append(jnp.concatenate([state_conformer_conv[i], glu.reshape(bs, ts, d)],
                                        axis=1)[:, -(CF_KERNEL - 1):])

            rw = jnp.concatenate([moe_router[i], jnp.zeros((d, LANES - N_EXPERTS), F32)], axis=1)
            part = _router(xp, mp1, rw, ROW_TILE, tps_p, n_tok, 0, None)
            h_all, ids_all, gates_all = _router(xs, ms1, rw, n_s, 1, n_tok, n_p, part)
            dest, row_tok, blk_e, nused, n_blk = _route_plan(ids_all, n_tok)
            ys = _experts(blk_e, row_tok, nused, h_all, moe_w1[i].astype(BF16), moe_w3[i].astype(BF16),
                          moe_w2[i].astype(BF16), n_blk)
            xp = _combine(dest, ys, gates_all, xp, mp1, g1, b1, ROW_TILE, tps_p, 0)
            xs = _combine(dest, ys, gates_all, xs, ms1, g1, b1, n_s, 1, n_p)
    return (xp.reshape(bp, tp, d), xs.reshape(bs, ts, d), jnp.stack(kv_p), jnp.stack(kv_s),
            jnp.stack(win_p), jnp.stack(win_s), jnp.stack(rh_p), jnp.stack(rh_s),
            jnp.stack(rc_p), jnp.stack(rc_s), jnp.stack(cc_p), jnp.stack(cc_s))
```

```python
import functools

import jax
import jax.numpy as jnp
from jax import lax
from jax.experimental import pallas as pl
from jax.experimental.pallas import tpu as pltpu

F32 = jnp.float32
BF16 = jnp.bfloat16
I32 = jnp.int32

D_MODEL = 1024
HEAD_DIM = 64
N_HEADS_A = 8
N_KV_A = 2
HEADS_PER_GROUP = N_HEADS_A // N_KV_A
GROUP_COLS = N_KV_A * HEAD_DIM
CMP_BLOCK = 32
SEL_BLOCK = 64
TOP_N = 16
WINDOW = 512
ROPE_THETA = 10000.0
PAGE_SIZE = 128
RG_WIDTH = 512
RG_BLOCKS = 8
RG_CONV = 4
RG_C = 8.0
NSA_WIDTH = 512
Q_COLS = NSA_WIDTH
KV_COLS = 6 * GROUP_COLS
GATE_COLS = 3 * N_HEADS_A
CF_KERNEL = 31
CF_HALO = 32
N_EXPERTS = 8
DEPTH = 4
ALPHA = (2.0 * DEPTH) ** 0.25
LN_EPS = 1e-5
NEG = -1e30
FORCE_SCORE = 1e4
ATTN_SCALE = HEAD_DIM ** -0.5

LANES = 128
SUBLANES = 8
VMEM_LIMIT = 56 * 1024 * 1024

ROW_TILE = 512
Q_TILE = 256
CONV_TILE = 256
CONV_SUB = 32
MOE_TILE = 512
FF_CHUNK = 256
MOE_FF_CHUNK = 512


def _cparams(n_axes, **kw):
    return pltpu.CompilerParams(dimension_semantics=("arbitrary",) * n_axes,
                                vmem_limit_bytes=VMEM_LIMIT, **kw)


def _const_spec(shape, single_buffer=False):
    n = len(shape)
    if single_buffer:
        return pl.BlockSpec(shape, lambda *a: (0,) * n, pipeline_mode=pl.Buffered(1))
    return pl.BlockSpec(shape, lambda *a: (0,) * n)


def _mod_spec(mod3, tm, tps):
    if mod3.shape[1] == 1:
        return pl.BlockSpec((1, 1, mod3.shape[2]), lambda i: (i // tps, 0, 0))
    return pl.BlockSpec((1, tm, mod3.shape[2]), lambda i: (0, i, 0))


def _dot(a, b):
    return jnp.dot(a, b, preferred_element_type=F32)


def _dot_nt(a, b):
    return lax.dot_general(a, b, (((1,), (1,)), ((), ())), preferred_element_type=F32)


def _layer_norm(y, g, b):
    mu = jnp.mean(y, axis=-1, keepdims=True)
    yc = y - mu
    var = jnp.mean(yc * yc, axis=-1, keepdims=True)
    return yc * lax.rsqrt(var + LN_EPS) * g + b


def _post_norm(x, mod, out, g, b):
    gate = mod[:, 2 * D_MODEL:]
    return _layer_norm(ALPHA * x + (1.0 + gate) * out, g, b)


def _modulate(x, mod):
    return x * (1.0 + mod[:, D_MODEL:2 * D_MODEL]) + mod[:, :D_MODEL]


def _silu(x):
    return x * jax.nn.sigmoid(x)


def _rope128(v, cos, sin_signed):
    lane = lax.broadcasted_iota(I32, v.shape, 1)
    from_hi = pltpu.roll(v, LANES - HEAD_DIM // 2, 1)
    from_lo = pltpu.roll(v, HEAD_DIM // 2, 1)
    swapped = jnp.where((lane & (HEAD_DIM - 1)) < HEAD_DIM // 2, from_hi, from_lo)
    return v * cos + swapped * sin_signed


def _rope(v, cos, sin_signed):
    k = v.shape[1] // LANES
    parts = [_rope128(v[:, i * LANES:(i + 1) * LANES], cos, sin_signed) for i in range(k)]
    return parts[0] if k == 1 else jnp.concatenate(parts, axis=1)


def _softmax_parts(parts):
    masked = [jnp.where(m, s, NEG) for s, m in parts]
    mx = functools.reduce(jnp.maximum, [jnp.max(s, axis=-1, keepdims=True) for s in masked])
    es = [jnp.where(m, jnp.exp(s - mx), 0.0) for s, (_, m) in zip(masked, parts)]
    den = functools.reduce(jnp.add, [jnp.sum(e, axis=-1, keepdims=True) for e in es])
    return es, 1.0 / jnp.maximum(den, 1e-30)


def _attend(qb, pieces):
    es, inv = _softmax_parts([(_dot(qb, k) if fm else _dot_nt(qb, k), m) for k, _, m, fm in pieces])
    o = functools.reduce(jnp.add, [_dot_nt(e.astype(BF16), v) if fm else _dot(e.astype(BF16), v)
                                   for e, (_, v, _, fm) in zip(es, pieces)])
    return o * inv


def _topk_mask(vals, n_valid, kk):
    lane = lax.broadcasted_iota(I32, vals.shape, 1)
    rank = jnp.zeros(vals.shape, I32)
    for m in range(n_valid):
        col = vals[:, m:m + 1]
        later = jnp.where(lane > m, 1, 0)
        rank = rank + jnp.where(col > vals, 1, jnp.where(col == vals, later, 0))
    return (rank < kk) & (lane < n_valid)


def _mod_kernel(c_ref, w_ref, b_ref, o_ref):
    s = _silu(c_ref[...]).astype(BF16)
    o_ref[0] = _dot(s, w_ref[0].astype(BF16)) + b_ref[0]


def _mod_all(c_all, w_mod, b_mod):
    n = c_all.shape[0]
    nl = w_mod.shape[0] * w_mod.shape[1]
    w = w_mod.reshape(nl, D_MODEL, 3 * D_MODEL)
    b = b_mod.reshape(nl, 1, 3 * D_MODEL)
    return pl.pallas_call(
        _mod_kernel, grid=(nl, 3),
        in_specs=[pl.BlockSpec((n, D_MODEL), lambda l, j: (0, 0)),
                  pl.BlockSpec((1, D_MODEL, D_MODEL), lambda l, j: (l, 0, j)),
                  pl.BlockSpec((1, 1, D_MODEL), lambda l, j: (l, 0, j))],
        out_specs=pl.BlockSpec((1, n, D_MODEL), lambda l, j: (l, 0, j)),
        out_shape=jax.ShapeDtypeStruct((nl, n, 3 * D_MODEL), F32),
        compiler_params=_cparams(2), name="mod_all")(c_all, w, b)


_O_KV = Q_COLS
_O_RX = _O_KV + KV_COLS
_O_RY = _O_RX + RG_WIDTH
_O_GL = _O_RY + RG_WIDTH
W_IN_COLS = _O_GL + LANES


def _win_kernel(x_ref, mod_ref, w_ref, cos_ref, sin_ref, q_ref, kv_ref, wr_ref, rx_ref, ry_ref, gl_ref):
    h = _modulate(x_ref[...], mod_ref[0]).astype(BF16)
    u = _dot(h, w_ref[...])
    cos = cos_ref[...]
    sin = sin_ref[...]
    q_ref[...] = _rope(u[:, :Q_COLS], cos, sin)
    c = _O_KV
    kv_ref[:, 0:2 * LANES] = u[:, c:c + 2 * LANES]
    kv_ref[:, 2 * LANES:3 * LANES] = _rope128(u[:, c + 2 * LANES:c + 3 * LANES], cos, sin)
    kv_ref[:, 3 * LANES:4 * LANES] = u[:, c + 3 * LANES:c + 4 * LANES]
    wr_ref[:, 0:LANES] = _rope128(u[:, c + 4 * LANES:c + 5 * LANES], cos, sin)
    wr_ref[:, LANES:2 * LANES] = u[:, c + 5 * LANES:c + 6 * LANES]
    rx_ref[...] = u[:, _O_RX:_O_RY]
    ry_ref[...] = u[:, _O_RY:_O_GL]
    gl_ref[...] = u[:, _O_GL:]


def _win_call(x2d, mod3, w, cos, sin, tm, tps):
    m = x2d.shape[0]
    row = lambda width: pl.BlockSpec((tm, width), lambda i: (i, 0))
    widths = (Q_COLS, 4 * LANES, 2 * LANES, RG_WIDTH, RG_WIDTH, LANES)
    return pl.pallas_call(
        _win_kernel, grid=(m // tm,),
        in_specs=[row(D_MODEL), _mod_spec(mod3, tm, tps), _const_spec(w.shape),
                  pl.BlockSpec((tm, LANES), lambda i: (i % tps, 0)),
                  pl.BlockSpec((tm, LANES), lambda i: (i % tps, 0))],
        out_specs=[row(wd) for wd in widths],
        out_shape=[jax.ShapeDtypeStruct((m, wd), F32) for wd in widths],
        compiler_params=_cparams(1), name="mixer_in")(x2d, mod3, w, cos, sin)


def _compress_core(k_ref, v_ref, w4_ref, pe4_ref, cos_ref, sin_ref, out_ref, nb):
    half = nb // 2
    acc = jnp.zeros((nb, 2 * LANES), F32)
    for l in range(CMP_BLOCK):
        even = pl.ds(l, half, stride=2 * CMP_BLOCK)
        odd = pl.ds(CMP_BLOCK + l, half, stride=2 * CMP_BLOCK)
        xk = jnp.concatenate([k_ref[even, :], k_ref[odd, :]], axis=0)
        xv = jnp.concatenate([v_ref[even, :], v_ref[odd, :]], axis=0)
        x = (jnp.concatenate([xk, xv], axis=1) + pe4_ref[pl.ds(l, 1), :]).astype(BF16)
        acc = acc + _dot(x, w4_ref[l])
    out_ref[:, 0:LANES] = _rope128(acc[:, 0:LANES], cos_ref[...], sin_ref[...])
    out_ref[:, LANES:2 * LANES] = acc[:, LANES:2 * LANES]


def _compress_prompt_kernel(k_ref, v_ref, w4_ref, pe4_ref, cos_ref, sin_ref, out_ref, *, nb):
    _compress_core(k_ref, v_ref, w4_ref, pe4_ref, cos_ref, sin_ref, out_ref, nb)


def _compress_prompt(kv4, bsz, t, w4, pe4, ccos, csin):
    nb = t // CMP_BLOCK
    return pl.pallas_call(
        functools.partial(_compress_prompt_kernel, nb=nb), grid=(bsz,),
        in_specs=[pl.BlockSpec((t, LANES), lambda b: (b, 0)), pl.BlockSpec((t, LANES), lambda b: (b, 1)),
                  _const_spec(w4.shape), _const_spec(pe4.shape),
                  _const_spec(ccos.shape), _const_spec(csin.shape)],
        out_specs=pl.BlockSpec((nb, 2 * LANES), lambda b: (b, 0)),
        out_shape=jax.ShapeDtypeStruct((bsz * nb, 2 * LANES), F32),
        compiler_params=_cparams(1), name="compress_prompt")(kv4, kv4, w4, pe4, ccos, csin)


def _prefetch_pages(pt_ref, cache_ref, sems, layer, feat0, npages, dst_fn):
    b = pl.program_id(0)

    def copy(seq, j):
        slot = seq % 2
        src = cache_ref.at[layer, pt_ref[seq * npages + j], pl.ds(feat0, 2 * LANES), :]
        return pltpu.make_async_copy(src, dst_fn(slot, j), sems.at[slot])

    def start_seq(seq):
        def body(j, c):
            copy(seq, j).start()
            return c
        lax.fori_loop(0, npages, body, 0)

    @pl.when(b == 0)
    def _():
        start_seq(b)

    @pl.when(b + 1 < pl.num_programs(0))
    def _():
        start_seq(b + 1)

    def wait(j, c):
        copy(b, j).wait()
        return c
    lax.fori_loop(0, npages, wait, 0)
    return b % 2


def _compress_sample_kernel(pt_ref, cache_ref, w4_ref, pe4_ref, cos_ref, sin_ref, out_ref, raw, kbuf, vbuf, sems,
                            *, layer, npages):
    def dst(slot, j):
        return raw.at[slot, pl.ds(pl.multiple_of(j * 2 * LANES, 2 * LANES), 2 * LANES), :]
    slot = _prefetch_pages(pt_ref, cache_ref, sems, layer, 0, npages, dst)

    def to_token_major(j, c):
        r0 = pl.multiple_of(j * 2 * LANES, 2 * LANES)
        t0 = pl.multiple_of(j * PAGE_SIZE, PAGE_SIZE)
        kbuf[pl.ds(t0, PAGE_SIZE), :] = raw[slot, pl.ds(r0, LANES), :].T
        vbuf[pl.ds(t0, PAGE_SIZE), :] = raw[slot, pl.ds(r0 + LANES, LANES), :].T
        return c
    lax.fori_loop(0, npages, to_token_major, 0, unroll=4)
    _compress_core(kbuf, vbuf, w4_ref, pe4_ref, cos_ref, sin_ref, out_ref, npages * PAGE_SIZE // CMP_BLOCK)


def _compress_sample(pt_flat, cache_fm, layer, bsz, npages, w4, pe4, ccos, csin):
    past = npages * PAGE_SIZE
    nb = past // CMP_BLOCK
    grid_spec = pltpu.PrefetchScalarGridSpec(
        num_scalar_prefetch=1, grid=(bsz,),
        in_specs=[pl.BlockSpec(memory_space=pl.ANY),
                  _const_spec(w4.shape), _const_spec(pe4.shape),
                  _const_spec(ccos.shape), _const_spec(csin.shape)],
        out_specs=pl.BlockSpec((nb, 2 * LANES), lambda b, pt: (b, 0)),
        scratch_shapes=[pltpu.VMEM((2, npages * 2 * LANES, PAGE_SIZE), F32),
                        pltpu.VMEM((past, LANES), F32), pltpu.VMEM((past, LANES), F32),
                        pltpu.SemaphoreType.DMA((2,))])
    return pl.pallas_call(
        functools.partial(_compress_sample_kernel, layer=layer, npages=npages),
        grid_spec=grid_spec,
        out_shape=jax.ShapeDtypeStruct((bsz * nb, 2 * LANES), F32),
        compiler_params=_cparams(1, disable_bounds_checks=True),
        name="compress_sample")(pt_flat, cache_fm, w4, pe4, ccos, csin)


def _cmp_positions(nc):
    n = lax.broadcasted_iota(I32, (1, nc), 1)
    half = nc // 2
    blk = jnp.where(n < half, 2 * n, 2 * (n - half) + 1)
    return blk * CMP_BLOCK + (CMP_BLOCK - 1)


def _pattn_kernel(q_ref, gl_ref, cmp_ref, ksv_ref, w0_ref, w1_ref, w2_ref, e_ref, o_ref, *, tq, t_len):
    ti = pl.program_id(1)
    nc = t_len // CMP_BLOCK
    nsel = t_len // SEL_BLOCK
    q0 = ti * tq
    qpos = q0 + lax.broadcasted_iota(I32, (tq, 1), 0)
    qs = (q_ref[...] * ATTN_SCALE).astype(BF16)
    gates = jax.nn.sigmoid(gl_ref[...])
    m_c = _cmp_positions(nc) <= qpos
    causal = lax.broadcasted_iota(I32, (1, t_len), 1) <= qpos
    wpos = q0 - 2 * tq + lax.broadcasted_iota(I32, (1, 3 * tq), 1)
    dpos = qpos - wpos
    m_w = (dpos >= 0) & (dpos < WINDOW) & (wpos >= 0)
    blk = lax.broadcasted_iota(I32, (1, nsel), 1)
    cur = qpos // SEL_BLOCK
    forced = (blk == cur) | (blk == 0)
    future = blk > cur
    for g in range(N_KV_A):
        ck = slice(g * HEAD_DIM, (g + 1) * HEAD_DIM)
        cv = slice(LANES + g * HEAD_DIM, LANES + (g + 1) * HEAD_DIM)
        kc = cmp_ref[:, ck].astype(BF16)
        vc = cmp_ref[:, cv].astype(BF16)
        o_cmp = []
        imp = jnp.zeros((tq, nc), F32)
        for r in range(HEADS_PER_GROUP):
            h = g * HEADS_PER_GROUP + r
            qh = qs[:, h * HEAD_DIM:(h + 1) * HEAD_DIM]
            (e,), inv = _softmax_parts([(_dot_nt(qh, kc), m_c)])
            p = e * inv
            o_cmp.append(_dot(p.astype(BF16), vc))
            imp = imp + p
        imp = imp[:, :nc // 2] + imp[:, nc // 2:]
        vals = jnp.where(forced, FORCE_SCORE, jnp.where(future, -1.0, imp))
        sel = _topk_mask(vals, nsel, min(TOP_N, nsel))
        sel_keys = _dot(jnp.where(sel, 1.0, 0.0).astype(BF16), e_ref[...])
        m_s = (sel_keys > 0.5) & causal
        ks = ksv_ref[:, ck].astype(BF16)
        vs = ksv_ref[:, cv].astype(BF16)
        kw = jnp.concatenate([w0_ref[:, ck], w1_ref[:, ck], w2_ref[:, ck]], axis=0).astype(BF16)
        vw = jnp.concatenate([w0_ref[:, cv], w1_ref[:, cv], w2_ref[:, cv]], axis=0).astype(BF16)
        for r in range(HEADS_PER_GROUP):
            h = g * HEADS_PER_GROUP + r
            qh = qs[:, h * HEAD_DIM:(h + 1) * HEAD_DIM]
            o_s = _attend(qh, [(ks, vs, m_s, False)])
            o_w = _attend(qh, [(kw, vw, m_w, False)])
            o_ref[:, h * HEAD_DIM:(h + 1) * HEAD_DIM] = (
                gates[:, 3 * h:3 * h + 1] * o_cmp[r] + gates[:, 3 * h + 1:3 * h + 2] * o_s
                + gates[:, 3 * h + 2:3 * h + 3] * o_w)


def _block_expand(nblk, nkeys):
    return (jnp.arange(nkeys)[None, :] // SEL_BLOCK == jnp.arange(nblk)[:, None]).astype(BF16)


def _pattn(q, gl, cmp, kv4, wr, bsz, t):
    tq = Q_TILE
    nt = t // tq
    nc = t // CMP_BLOCK
    expand = _block_expand(t // SEL_BLOCK, t)
    row = lambda width: pl.BlockSpec((tq, width), lambda b, i: (b * nt + i, 0))
    wspec = lambda back: pl.BlockSpec((tq, 2 * LANES), lambda b, i: (b * nt + jnp.maximum(i - back, 0), 0))
    return pl.pallas_call(
        functools.partial(_pattn_kernel, tq=tq, t_len=t), grid=(bsz, nt),
        in_specs=[row(Q_COLS), row(LANES),
                  pl.BlockSpec((nc, 2 * LANES), lambda b, i: (b, 0)),
                  pl.BlockSpec((t, 2 * LANES), lambda b, i: (b, 1)),
                  wspec(2), wspec(1), wspec(0), _const_spec(expand.shape)],
        out_specs=row(NSA_WIDTH),
        out_shape=jax.ShapeDtypeStruct((bsz * t, NSA_WIDTH), F32),
        compiler_params=_cparams(2), name="nsa_prompt")(q, gl, cmp, kv4, wr, wr, wr, expand)


def _sattn_kernel(pt_ref, cache_ref, q_ref, gl_ref, cmp_ref, kvn_ref, wrn_ref, win_ref, e_ref, o_ref,
                  buf, sems, *, layer, npages, ts):
    past = npages * PAGE_SIZE

    def dst(slot, j):
        return buf.at[slot, :, pl.ds(pl.multiple_of(j * PAGE_SIZE, PAGE_SIZE), PAGE_SIZE)]
    slot = _prefetch_pages(pt_ref, cache_ref, sems, layer, 2 * LANES, npages, dst)
    nc = past // CMP_BLOCK
    nblk_past = past // SEL_BLOCK
    nsel = nblk_past + 1
    lanes_sel = 2 * nblk_past
    rows = HEADS_PER_GROUP * ts
    tok1 = lax.broadcasted_iota(I32, (ts, 1), 0)
    tok = jnp.concatenate([tok1] * HEADS_PER_GROUP, axis=0)
    qpos1 = past + tok1
    qpos = past + tok
    qs = q_ref[...] * ATTN_SCALE
    gates = jax.nn.sigmoid(gl_ref[...])
    m_c = _cmp_positions(nc) <= qpos
    blk = lax.broadcasted_iota(I32, (1, lanes_sel), 1)
    cur = qpos1 // SEL_BLOCK
    forced = (blk == cur) | (blk == 0)
    future = blk > cur
    m_past = lax.broadcasted_iota(I32, (1, past), 1) <= qpos
    tkey = lax.broadcasted_iota(I32, (1, ts), 1)
    m_new = tkey <= tok
    win_len = win_ref.shape[1]
    wpos = past - win_len + lax.broadcasted_iota(I32, (1, win_len), 1)
    dpos = qpos - wpos
    m_wstate = (dpos >= 0) & (dpos < WINDOW) & (wpos >= 0)
    dnew = tok - tkey
    m_wnew = (dnew >= 0) & (dnew < WINDOW)
    for g in range(N_KV_A):
        ck = slice(g * HEAD_DIM, (g + 1) * HEAD_DIM)
        cv = slice(LANES + g * HEAD_DIM, LANES + (g + 1) * HEAD_DIM)
        heads = [g * HEADS_PER_GROUP + r for r in range(HEADS_PER_GROUP)]
        qg = jnp.concatenate([qs[:, h * HEAD_DIM:(h + 1) * HEAD_DIM] for h in heads], axis=0).astype(BF16)
        kc = cmp_ref[:, ck].astype(BF16)
        vc = cmp_ref[:, cv].astype(BF16)
        (e,), inv = _softmax_parts([(_dot_nt(qg, kc), m_c)])
        p = e * inv
        o_c = _dot(p.astype(BF16), vc)
        imp = p[0:ts]
        for r in range(1, HEADS_PER_GROUP):
            imp = imp + p[r * ts:(r + 1) * ts]
        imp = imp[:, :nc // 2] + imp[:, nc // 2:]
        imp = jnp.concatenate([imp, jnp.zeros((ts, lanes_sel - nc // 2), F32)], axis=1)
        vals = jnp.where(forced, FORCE_SCORE, jnp.where(future, -1.0, imp))
        sel = jnp.where(_topk_mask(vals, nsel, min(TOP_N, nsel)), 1.0, 0.0)
        sel = jnp.concatenate([sel] * HEADS_PER_GROUP, axis=0)
        sel_keys = _dot(sel[:, :nblk_past].astype(BF16), e_ref[...])
        m_s_past = (sel_keys > 0.5) & m_past
        m_s_new = (sel[:, nblk_past:nblk_past + 1] > 0.5) & m_new
        kn = kvn_ref[:, 2 * LANES:4 * LANES]
        o_s = _attend(qg, [(buf[slot, ck, :].astype(BF16), buf[slot, cv, :].astype(BF16), m_s_past, True),
                           (kn[:, ck].astype(BF16), kn[:, cv].astype(BF16), m_s_new, False)])
        o_w = _attend(qg, [(win_ref[ck, :].astype(BF16), win_ref[cv, :].astype(BF16), m_wstate, True),
                           (wrn_ref[:, ck].astype(BF16), wrn_ref[:, cv].astype(BF16), m_wnew, False)])
        for r, h in enumerate(heads):
            rs = slice(r * ts, (r + 1) * ts)
            o_ref[:, h * HEAD_DIM:(h + 1) * HEAD_DIM] = (
                gates[:, 3 * h:3 * h + 1] * o_c[rs] + gates[:, 3 * h + 1:3 * h + 2] * o_s[rs]
                + gates[:, 3 * h + 2:3 * h + 3] * o_w[rs])


def _sattn(pt_flat, cache_fm, layer, q, gl, cmp, kv4, wr, win_fm, bsz, ts, npages):
    past = npages * PAGE_SIZE
    nc = past // CMP_BLOCK
    win_len = win_fm.shape[3]
    expand = _block_expand(past // SEL_BLOCK, past)
    row = lambda width: pl.BlockSpec((ts, width), lambda b, pt: (b, 0))
    grid_spec = pltpu.PrefetchScalarGridSpec(
        num_scalar_prefetch=1, grid=(bsz,),
        in_specs=[pl.BlockSpec(memory_space=pl.ANY), row(Q_COLS), row(LANES),
                  pl.BlockSpec((nc, 2 * LANES), lambda b, pt: (b, 0)),
                  row(4 * LANES), row(2 * LANES),
                  pl.BlockSpec((None, None, 2 * LANES, win_len), lambda b, pt: (layer, b, 0, 0)),
                  _const_spec(expand.shape)],
        out_specs=row(NSA_WIDTH),
        scratch_shapes=[pltpu.VMEM((2, 2 * LANES, past), F32), pltpu.SemaphoreType.DMA((2,))])
    return pl.pallas_call(
        functools.partial(_sattn_kernel, layer=layer, npages=npages, ts=ts),
        grid_spec=grid_spec,
        out_shape=jax.ShapeDtypeStruct((bsz * ts, NSA_WIDTH), F32),
        compiler_params=_cparams(1, disable_bounds_checks=True),
        name="nsa_sample")(pt_flat, cache_fm, q, gl, cmp, kv4, wr, win_fm, expand)


RG_STATE_ROWS = SUBLANES
RG_ROWS = 256


def _rglru_kernel(rx_ref, ry_ref, st_ref, cw_ref, cb_ref, wa_ref, ba_ref, wx_ref, bx_ref, lam_ref,
                  out_ref, hl_ref, nb_ref, xs, a_s, b_s, *, t_len):
    ch = min(RG_ROWS, t_len)
    xs[0:RG_STATE_ROWS, :] = st_ref[0]
    xs[RG_STATE_ROWS:RG_STATE_ROWS + t_len, :] = rx_ref[...]
    lam = lam_ref[...]
    softplus_neg_lam = jnp.maximum(-lam, 0.0) + jnp.log1p(jnp.exp(-jnp.abs(lam)))
    sub = lax.broadcasted_iota(I32, (ch, RG_WIDTH), 0) & (SUBLANES - 1)
    for c in range(t_len // ch):
        r0 = c * ch
        xc = cb_ref[...]
        for j in range(RG_CONV):
            xc = xc + cw_ref[j:j + 1, :] * xs[r0 + RG_STATE_ROWS - (RG_CONV - 1) + j:
                                              r0 + RG_STATE_ROWS - (RG_CONV - 1) + j + ch, :]
        xb = xc.astype(BF16)
        r = jax.nn.sigmoid(_dot(xb, wa_ref[...]) + ba_ref[...])
        i = jax.nn.sigmoid(_dot(xb, wx_ref[...]) + bx_ref[...])
        log_a = -RG_C * r * softplus_neg_lam
        a = jnp.exp(log_a)
        one_minus_a2 = -jnp.tanh(log_a) * (jnp.exp(2.0 * log_a) + 1.0)
        bb = jnp.sqrt(one_minus_a2) * (i * xc)
        for s in (1, 2, 4):
            ok = sub >= s
            a_prev = pltpu.roll(a, s, 0)
            b_prev = pltpu.roll(bb, s, 0)
            bb = jnp.where(ok, a * b_prev + bb, bb)
            a = jnp.where(ok, a * a_prev, a)
        a_s[r0:r0 + ch, :] = a
        b_s[r0:r0 + ch, :] = bb

    def step(k, h):
        r0 = pl.multiple_of(k * SUBLANES, SUBLANES)
        hk = b_s[pl.ds(r0, SUBLANES), :] + a_s[pl.ds(r0, SUBLANES), :] * h
        b_s[pl.ds(r0, SUBLANES), :] = hk
        return jnp.broadcast_to(hk[SUBLANES - 1:SUBLANES, :], (SUBLANES, RG_WIDTH))

    h0 = jnp.broadcast_to(st_ref[0, 0:1, :], (SUBLANES, RG_WIDTH))
    h_fin = lax.fori_loop(0, t_len // SUBLANES, step, h0)
    hl_ref[0] = h_fin[0:1, :]
    nb_ref[0] = xs[RG_STATE_ROWS + t_len - (RG_CONV - 1):RG_STATE_ROWS + t_len, :]
    for c in range(t_len // ch):
        r0 = c * ch
        out_ref[r0:r0 + ch, :] = b_s[r0:r0 + ch, :] * jax.nn.gelu(ry_ref[r0:r0 + ch, :])


def _rglru(rx, ry, st, cw, cb, wa, ba, wx, bx, lam, bsz, t):
    row = pl.BlockSpec((t, RG_WIDTH), lambda b: (b, 0))
    vec = _const_spec((1, RG_WIDTH))
    return pl.pallas_call(
        functools.partial(_rglru_kernel, t_len=t), grid=(bsz,),
        in_specs=[row, row, pl.BlockSpec((1, RG_STATE_ROWS, RG_WIDTH), lambda b: (b, 0, 0)),
                  _const_spec(cw.shape), vec, _const_spec(wa.shape), vec, _const_spec(wx.shape), vec, vec],
        out_specs=[row, pl.BlockSpec((1, 1, RG_WIDTH), lambda b: (b, 0, 0)),
                   pl.BlockSpec((1, RG_CONV - 1, RG_WIDTH), lambda b: (b, 0, 0))],
        out_shape=[jax.ShapeDtypeStruct((bsz * t, RG_WIDTH), F32),
                   jax.ShapeDtypeStruct((bsz, 1, RG_WIDTH), F32),
                   jax.ShapeDtypeStruct((bsz, RG_CONV - 1, RG_WIDTH), F32)],
        scratch_shapes=[pltpu.VMEM((RG_STATE_ROWS + t, RG_WIDTH), F32),
                        pltpu.VMEM((t, RG_WIDTH), F32), pltpu.VMEM((t, RG_WIDTH), F32)],
        compiler_params=_cparams(1), name="rglru")(rx, ry, st, cw, cb, wa, ba, wx, bx, lam)


def _proj_ln_kernel(*refs, n_in):
    a_refs = refs[:n_in]
    w_refs = refs[n_in:2 * n_in]
    x_ref, mod_ref, bias_ref, g_ref, b_ref, o_ref = refs[2 * n_in:]
    out = bias_ref[...]
    for a_ref, w_ref in zip(a_refs, w_refs):
        out = out + _dot(a_ref[...].astype(BF16), w_ref[...])
    o_ref[...] = _post_norm(x_ref[...], mod_ref[0], out, g_ref[...], b_ref[...])


def _proj_ln(a_list, w_list, x2d, mod3, bias, g, b, tm, tps):
    m = x2d.shape[0]
    n_in = len(a_list)
    row = lambda width: pl.BlockSpec((tm, width), lambda i: (i, 0))
    vec = _const_spec((1, D_MODEL))
    return pl.pallas_call(
        functools.partial(_proj_ln_kernel, n_in=n_in), grid=(m // tm,),
        in_specs=[row(a.shape[1]) for a in a_list] + [_const_spec(w.shape) for w in w_list]
        + [row(D_MODEL), _mod_spec(mod3, tm, tps), vec, vec, vec],
        out_specs=row(D_MODEL),
        out_shape=jax.ShapeDtypeStruct((m, D_MODEL), F32),
        compiler_params=_cparams(1), name="proj_postnorm")(*a_list, *w_list, x2d, mod3, bias, g, b)


def _ffn_kernel(x_ref, mod_ref, w1_ref, w3_ref, w2_ref, g_ref, b_ref, o_ref, *, ff):
    x = x_ref[...]
    mod = mod_ref[0]
    h = _modulate(x, mod).astype(BF16)
    acc = jnp.zeros(x.shape, F32)
    for c in range(ff // FF_CHUNK):
        cs = slice(c * FF_CHUNK, (c + 1) * FF_CHUNK)
        z = _silu(_dot(h, w1_ref[:, cs])) * _dot(h, w3_ref[:, cs])
        acc = acc + _dot(z.astype(BF16), w2_ref[cs, :])
    o_ref[...] = _post_norm(x, mod, acc, g_ref[...], b_ref[...])


def _ffn(x2d, mod3, w1, w3, w2, g, b, tm, tps):
    m = x2d.shape[0]
    row = pl.BlockSpec((tm, D_MODEL), lambda i: (i, 0))
    vec = _const_spec((1, D_MODEL))
    return pl.pallas_call(
        functools.partial(_ffn_kernel, ff=w1.shape[1]), grid=(m // tm,),
        in_specs=[row, _mod_spec(mod3, tm, tps), _const_spec(w1.shape, True), _const_spec(w3.shape, True),
                  _const_spec(w2.shape, True), vec, vec],
        out_specs=row,
        out_shape=jax.ShapeDtypeStruct((m, D_MODEL), F32),
        compiler_params=_cparams(1), name="dense_ffn")(x2d, mod3, w1, w3, w2, g, b)


def _cf_in_kernel(x_ref, mod_ref, w_ref, b_ref, o_ref):
    h = _modulate(x_ref[...], mod_ref[0]).astype(BF16)
    u = _dot(h, w_ref[...]) + b_ref[...]
    o_ref[...] = u[:, :D_MODEL] * jax.nn.sigmoid(u[:, D_MODEL:])


def _cf_in(x2d, mod3, w, bias, tm, tps):
    m = x2d.shape[0]
    row = pl.BlockSpec((tm, D_MODEL), lambda i: (i, 0))
    return pl.pallas_call(
        _cf_in_kernel, grid=(m // tm,),
        in_specs=[row, _mod_spec(mod3, tm, tps), _const_spec(w.shape), _const_spec(bias.shape)],
        out_specs=row,
        out_shape=jax.ShapeDtypeStruct((m, D_MODEL), F32),
        compiler_params=_cparams(1), name="conformer_in")(x2d, mod3, w, bias)


def _cf_conv_kernel(x_ref, halo_ref, dw_ref, db_ref, g_ref, b_ref, z_ref, s_ref, *, tt, zero_first):
    halo = halo_ref[...]
    if zero_first:
        halo = jnp.where(pl.program_id(1) == 0, 0.0, halo)
    s_ref[0, 0:CF_HALO, :] = halo
    s_ref[0, CF_HALO:CF_HALO + tt, :] = x_ref[...]
    n_sh = CF_HALO + tt - SUBLANES
    for k in range(1, SUBLANES):
        s_ref[k, 0:n_sh, :] = s_ref[0, k:k + n_sh, :]
    sub = min(CONV_SUB, tt)
    first = CF_HALO - (CF_KERNEL - 1)
    for c in range(tt // sub):
        r0 = c * sub
        y = jnp.broadcast_to(db_ref[...], (sub, D_MODEL))
        for j in range(CF_KERNEL):
            k = (first + j) % SUBLANES
            a0 = r0 + first + j - k
            y = y + dw_ref[j:j + 1, :] * s_ref[k, a0:a0 + sub, :]
        z_ref[r0:r0 + sub, :] = _silu(_layer_norm(y, g_ref[...], b_ref[...]))


def _cf_conv(glu, halo_src, halo_map, dw, db, g, b, bsz, t, tt, zero_first):
    nt = t // tt
    vec = _const_spec((1, D_MODEL))
    return pl.pallas_call(
        functools.partial(_cf_conv_kernel, tt=tt, zero_first=zero_first), grid=(bsz, nt),
        in_specs=[pl.BlockSpec((tt, D_MODEL), lambda bi, ti: (bi * nt + ti, 0)),
                  pl.BlockSpec((CF_HALO, D_MODEL), halo_map),
                  _const_spec(dw.shape), vec, vec, vec],
        out_specs=pl.BlockSpec((tt, D_MODEL), lambda bi, ti: (bi * nt + ti, 0)),
        out_shape=jax.ShapeDtypeStruct((bsz * t, D_MODEL), F32),
        scratch_shapes=[pltpu.VMEM((SUBLANES, CF_HALO + tt, D_MODEL), F32)],
        compiler_params=_cparams(2), name="conformer_conv")(glu, halo_src, dw, db, g, b)


def _router_kernel(x_ref, mod_ref, rw_ref, h_ref, ids_ref, gates_ref, *, n_real):
    h = _modulate(x_ref[...], mod_ref[0])
    h_ref[...] = jnp.where(pl.program_id(0) < n_real, h, 0.0)
    logits = lax.dot_general(h, rw_ref[...], (((1,), (0,)), ((), ())), precision=lax.Precision.HIGHEST,
                             preferred_element_type=F32)
    lane = lax.broadcasted_iota(I32, logits.shape, 1)
    logits = jnp.where(lane < N_EXPERTS, logits, -jnp.inf)
    m1 = jnp.max(logits, axis=-1, keepdims=True)
    i1 = jnp.min(jnp.where(logits == m1, lane, LANES), axis=-1, keepdims=True)
    rest = jnp.where(lane == i1, -jnp.inf, logits)
    m2 = jnp.max(rest, axis=-1, keepdims=True)
    i2 = jnp.min(jnp.where(rest == m2, lane, LANES), axis=-1, keepdims=True)
    e2 = jnp.exp(m2 - m1)
    inv = 1.0 / (1.0 + e2)
    col = lax.broadcasted_iota(I32, ids_ref.shape, 1)
    ids_ref[...] = jnp.where(col == 0, i1, jnp.where(col == 1, i2, 0))
    gates_ref[...] = jnp.where(col == 0, inv, jnp.where(col == 1, e2 * inv, 0.0))


def _router(x2d, mod3, rw, tm, tps, n_total, row_off, h_prev):
    m = x2d.shape[0]
    n_real = m // tm
    off = row_off // tm
    if h_prev is None:
        steps = -(-n_total // tm)
        clamp = lambda i: jnp.minimum(i, n_real - 1)
    else:
        steps = n_real
        clamp = lambda i: i
    if mod3.shape[1] == 1:
        mspec = pl.BlockSpec((1, 1, mod3.shape[2]), lambda i: (clamp(i) // tps, 0, 0))
    else:
        mspec = pl.BlockSpec((1, tm, mod3.shape[2]), lambda i: (0, clamp(i), 0))
    small = lambda: pl.BlockSpec((tm, SUBLANES), lambda i: (clamp(i), 0))
    in_specs = [pl.BlockSpec((tm, D_MODEL), lambda i: (clamp(i), 0)), mspec, _const_spec(rw.shape)]
    args = [x2d, mod3, rw]
    kern = functools.partial(_router_kernel, n_real=n_real)
    aliases = {}
    if h_prev is not None:
        in_specs.append(pl.BlockSpec(memory_space=pl.ANY))
        args.append(h_prev)
        aliases = {3: 0}
        kern = lambda x, md, rw_, hp, h, ids, gt: _router_kernel(x, md, rw_, h, ids, gt, n_real=n_real)
    return pl.pallas_call(
        kern, grid=(steps,), in_specs=in_specs,
        out_specs=[pl.BlockSpec((tm, D_MODEL), lambda i: (i + off, 0)), small(), small()],
        out_shape=[jax.ShapeDtypeStruct((n_total, D_MODEL), F32), jax.ShapeDtypeStruct((m, SUBLANES), I32),
                   jax.ShapeDtypeStruct((m, SUBLANES), F32)],
        input_output_aliases=aliases,
        compiler_params=_cparams(1), name="moe_router")(*args)


ROW_SRC_BITS = 15


DMA_GROUP = 8


def _for_rows(n, fn):
    full = n // DMA_GROUP

    def group(gi, c):
        for u in range(DMA_GROUP):
            fn(gi * DMA_GROUP + u)
        return c
    lax.fori_loop(0, full, group, 0)
    if isinstance(n, int):
        for r in range(full * DMA_GROUP, n):
            fn(r)
    else:
        for u in range(DMA_GROUP - 1):
            r = full * DMA_GROUP + u

            @pl.when(r < n)
            def _():
                fn(r)


def _experts_kernel(blk_e_ref, plan_ref, cnt_ref, nused_ref, h_ref, w1_ref, w3_ref, w2_ref, y_ref,
                    xs, xb, acc, ybuf, gsem, ssem):
    i = pl.program_id(0)
    j = pl.program_id(1)
    nused = nused_ref[0]
    used = i < nused
    last = j == pl.num_programs(1) - 1
    src_mask = (1 << ROW_SRC_BITS) - 1

    def start_gather(blk):
        slot = blk % 2

        def one(r):
            tok = plan_ref[blk * MOE_TILE + r] & src_mask
            pltpu.make_async_copy(h_ref.at[pl.ds(tok, 1), :], xs.at[slot, pl.ds(r, 1), :], gsem.at[slot]).start()
        _for_rows(MOE_TILE, one)

    def wait_gather(blk):
        slot = blk % 2
        pltpu.make_async_copy(h_ref.at[pl.ds(0, MOE_TILE), :], xs.at[slot], gsem.at[slot]).wait()

    def scatter_copy(r, dst, rows):
        return pltpu.make_async_copy(ybuf.at[pl.ds(r, rows), :], y_ref.at[pl.ds(dst, rows), :], ssem)

    def start_scatter(blk):
        def one(r):
            scatter_copy(r, plan_ref[blk * MOE_TILE + r] >> ROW_SRC_BITS, 1).start()
        _for_rows(cnt_ref[blk], one)

    def wait_scatter(blk):
        n = cnt_ref[blk]
        full = n // DMA_GROUP

        def group(gi, c):
            scatter_copy(0, 0, DMA_GROUP).wait()
            return c
        lax.fori_loop(0, full, group, 0)

        def one(r, c):
            scatter_copy(0, 0, 1).wait()
            return c
        lax.fori_loop(full * DMA_GROUP, n, one, 0)

    @pl.when(used & (j == 0))
    def _():
        @pl.when(i == 0)
        def _():
            start_gather(i)

        @pl.when(i + 1 < nused)
        def _():
            start_gather(i + 1)
        wait_gather(i)
        xb[...] = xs[i % 2].astype(BF16)
        acc[...] = jnp.zeros(acc.shape, F32)

    @pl.when(used)
    def _():
        x = xb[...]
        z = _silu(_dot(x, w1_ref[...])) * _dot(x, w3_ref[...])
        acc[...] += _dot(z.astype(BF16), w2_ref[...])

    @pl.when(used & last)
    def _():
        @pl.when(i >= 1)
        def _():
            wait_scatter(i - 1)
        ybuf[...] = acc[...]
        start_scatter(i)

        @pl.when(i == nused - 1)
        def _():
            wait_scatter(i)


def _experts(blk_e, plan, cnt, nused, h_all, n_tok, w1, w3, w2, n_blk):
    ff = w1.shape[2]
    nj = ff // MOE_FF_CHUNK

    def jj(i, j, nu):
        return jnp.where(i < nu[0], j, nj - 1)

    wspec = lambda shape, imap: pl.BlockSpec(shape, imap)
    grid_spec = pltpu.PrefetchScalarGridSpec(
        num_scalar_prefetch=4, grid=(n_blk, nj),
        in_specs=[pl.BlockSpec(memory_space=pl.ANY),
                  wspec((None, D_MODEL, MOE_FF_CHUNK), lambda i, j, be, pn, ct, nu: (be[i], 0, jj(i, j, nu))),
                  wspec((None, D_MODEL, MOE_FF_CHUNK), lambda i, j, be, pn, ct, nu: (be[i], 0, jj(i, j, nu))),
                  wspec((None, MOE_FF_CHUNK, D_MODEL), lambda i, j, be, pn, ct, nu: (be[i], jj(i, j, nu), 0))],
        out_specs=pl.BlockSpec(memory_space=pl.ANY),
        scratch_shapes=[pltpu.VMEM((2, MOE_TILE, D_MODEL), F32), pltpu.VMEM((MOE_TILE, D_MODEL), BF16),
                        pltpu.VMEM((MOE_TILE, D_MODEL), F32), pltpu.VMEM((MOE_TILE, D_MODEL), F32),
                        pltpu.SemaphoreType.DMA((2,)), pltpu.SemaphoreType.DMA(())])
    return pl.pallas_call(
        _experts_kernel, grid_spec=grid_spec,
        out_shape=jax.ShapeDtypeStruct((2 * n_tok, D_MODEL), F32),
        compiler_params=_cparams(2, disable_bounds_checks=True),
        name="moe_experts")(blk_e, plan, cnt, nused, h_all, w1, w3, w2)


def _combine_kernel(y0_ref, y1_ref, gates_ref, x_ref, mod_ref, g_ref, b_ref, o_ref):
    gates = gates_ref[...]
    out = gates[:, 0:1] * y0_ref[...] + gates[:, 1:2] * y1_ref[...]
    o_ref[...] = _post_norm(x_ref[...], mod_ref[0], out, g_ref[...], b_ref[...])


def _combine(ys, gates, x2d, mod3, g, b, tm, tps, tok_off, n_tok):
    m = x2d.shape[0]
    off0 = tok_off // tm
    off1 = (n_tok + tok_off) // tm
    row = lambda width: pl.BlockSpec((tm, width), lambda i: (i, 0))
    vec = _const_spec((1, D_MODEL))
    return pl.pallas_call(
        _combine_kernel, grid=(m // tm,),
        in_specs=[pl.BlockSpec((tm, D_MODEL), lambda i: (i + off0, 0)),
                  pl.BlockSpec((tm, D_MODEL), lambda i: (i + off1, 0)),
                  row(SUBLANES), row(D_MODEL), _mod_spec(mod3, tm, tps), vec, vec],
        out_specs=row(D_MODEL),
        out_shape=jax.ShapeDtypeStruct((m, D_MODEL), F32),
        compiler_params=_cparams(1), name="moe_combine")(ys, ys, gates, x2d, mod3, g, b)


def _route_plan(ids_all, n_tok):
    fe = ids_all[:, :2].reshape(-1)
    onehot = (fe[:, None] == jnp.arange(N_EXPERTS, dtype=I32)[None, :]).astype(I32)
    csum = jnp.cumsum(onehot, axis=0)
    rank = jnp.take_along_axis(csum, fe[:, None], axis=1)[:, 0] - 1
    counts = csum[-1]
    padded = (counts + MOE_TILE - 1) // MOE_TILE * MOE_TILE
    pend = jnp.cumsum(padded)
    pstart = pend - padded
    dest = (pstart[fe] + rank).astype(I32)
    n_blk = -(-(2 * n_tok) // MOE_TILE) + N_EXPERTS
    a = jnp.arange(2 * n_tok, dtype=I32)
    word = (a // 2) | (((a % 2) * n_tok + a // 2) << ROW_SRC_BITS)
    plan = jnp.zeros((n_blk * MOE_TILE,), I32).at[dest].set(word)
    nused = (pend[-1] // MOE_TILE).astype(I32)
    blk = jnp.arange(n_blk, dtype=I32)
    blk_e = jnp.searchsorted(pend, jnp.minimum(blk, nused - 1) * MOE_TILE, side='right').astype(I32)
    blk_e = jnp.minimum(blk_e, N_EXPERTS - 1)
    cnt = jnp.clip(pstart[blk_e] + counts[blk_e] - blk * MOE_TILE, 0, MOE_TILE)
    cnt = jnp.where(blk < nused, cnt, 0).astype(I32)
    return plan, blk_e, cnt, nused.reshape(1), n_blk


def _rope_tables(pos):
    half = HEAD_DIM // 2
    inv = 1.0 / (ROPE_THETA ** (jnp.arange(half, dtype=F32) * (2.0 / HEAD_DIM)))
    ang = pos.astype(F32)[:, None] * inv[None, :]
    c = jnp.cos(ang)
    s = jnp.sin(ang)
    return jnp.concatenate([c, c, c, c], axis=1), jnp.concatenate([-s, s, -s, s], axis=1)


def _cmp_rope_tables(nc):
    blk = jnp.concatenate([jnp.arange(0, nc, 2), jnp.arange(1, nc, 2)])
    return _rope_tables(blk * CMP_BLOCK + (CMP_BLOCK - 1))


def _w_in_layout(w_in):
    o1 = Q_COLS
    o2 = o1 + KV_COLS
    o3 = o2 + GATE_COLS
    pad = jnp.zeros((D_MODEL, LANES - GATE_COLS), w_in.dtype)
    return jnp.concatenate([w_in[:, :o2], w_in[:, o3:], w_in[:, o2:o3], pad], axis=1).astype(BF16)


def _block_diag(blocks):
    n, a, b = blocks.shape
    eye = jnp.eye(n, dtype=blocks.dtype)
    return (eye[:, None, :, None] * blocks[:, :, None, :]).reshape(n * a, n * b)


def _cmp_weights(w_ck, w_cv, pe_k, pe_v):
    wk = w_ck.reshape(CMP_BLOCK, HEAD_DIM, HEAD_DIM)
    wv = w_cv.reshape(CMP_BLOCK, HEAD_DIM, HEAD_DIM)
    w4 = jax.vmap(lambda a, b: _block_diag(jnp.stack([a, a, b, b])))(wk, wv).astype(BF16)
    pe4 = jnp.concatenate([pe_k, pe_k, pe_v, pe_v], axis=1)
    return w4, pe4


def kernel(x_prompt, x_sample, c_prompt, c_sample, page_table, cache_nsa_kv, state_nsa_win, state_rglru_h, state_rglru_conv, state_conformer_conv, w_mod, b_mod, ln_g, ln_b, w_in_even, w_out_even, w_cmp_k, w_cmp_v, pe_cmp_k, pe_cmp_v, rg_conv_w, rg_conv_b, rg_wa, rg_ba, rg_wx, rg_bx, rg_lam, cf_w1, cf_b1, cf_dw, cf_db, cf_ln_g, cf_ln_b, cf_w2, cf_b2, ff_w1, ff_w3, ff_w2, moe_router, moe_w1, moe_w3, moe_w2):
    bp, tp, d = x_prompt.shape
    bs, ts, _ = x_sample.shape
    npages = page_table.shape[1]
    past = npages * PAGE_SIZE
    n_p = bp * tp
    n_s = bs * ts
    n_tok = n_p + n_s
    assert d == D_MODEL and tp % ROW_TILE == 0 and tp % Q_TILE == 0 and tp >= WINDOW
    assert past % SEL_BLOCK == 0 and ts <= CMP_BLOCK and ts % SUBLANES == 0 and n_s % SUBLANES == 0
    assert n_p % MOE_TILE == 0 and n_p % n_s == 0
    tps_p = tp // ROW_TILE
    xp = x_prompt.reshape(n_p, d)
    xs = x_sample.reshape(n_s, d)
    vec = lambda v: v.reshape(1, -1)

    mod_all = _mod_all(jnp.concatenate([c_prompt, c_sample], axis=0), w_mod, b_mod)

    def mods(l, s):
        mrow = mod_all[2 * l + s]
        return mrow[:bp].reshape(bp, 1, 3 * d), jnp.repeat(mrow[bp:], ts, axis=0).reshape(1, n_s, 3 * d)

    cos_p, sin_p = _rope_tables(jnp.arange(tp))
    cos_s, sin_s = _rope_tables(jnp.tile(past + jnp.arange(ts), bs))
    ccos_p, csin_p = _cmp_rope_tables(tp // CMP_BLOCK)
    ccos_s, csin_s = _cmp_rope_tables(past // CMP_BLOCK)
    pt_flat = page_table.reshape(-1).astype(I32)
    n_even = cache_nsa_kv.shape[0]
    win_len = state_nsa_win.shape[2]
    cache_fm = cache_nsa_kv.transpose(0, 1, 3, 4, 5, 2).reshape(n_even, cache_nsa_kv.shape[1], 4 * LANES, PAGE_SIZE)
    win_fm = state_nsa_win.transpose(0, 1, 3, 4, 5, 2).reshape(n_even, bs, 2 * LANES, win_len)
    tm_c = min(Q_TILE, n_s)
    assert n_p % tm_c == 0 and n_tok % tm_c == 0
    assert n_tok < (1 << ROW_SRC_BITS) and 2 * n_tok < (1 << (31 - ROW_SRC_BITS))

    kv_p, kv_s, win_p, win_s, rh_p, rh_s, rc_p, rc_s, cc_p, cc_s = ([] for _ in range(10))
    for l in range(DEPTH):
        i = l // 2
        mp0, ms0 = mods(l, 0)
        mp1, ms1 = mods(l, 1)
        g0, b0, g1, b1 = vec(ln_g[l, 0]), vec(ln_b[l, 0]), vec(ln_g[l, 1]), vec(ln_b[l, 1])
        zero_bias = jnp.zeros((1, d), F32)
        if l % 2 == 0:
            w_in = _w_in_layout(w_in_even[i])
            w4, pe4 = _cmp_weights(w_cmp_k[i], w_cmp_v[i], pe_cmp_k[i], pe_cmp_v[i])
            wo_a = w_out_even[i][:NSA_WIDTH].astype(BF16)
            wo_r = w_out_even[i][NSA_WIDTH:].astype(BF16)
            wa = _block_diag(rg_wa[i]).astype(BF16)
            wx = _block_diag(rg_wx[i]).astype(BF16)
            rg_args = (rg_conv_w[i], vec(rg_conv_b[i]), wa, vec(rg_ba[i]), wx, vec(rg_bx[i]), vec(rg_lam[i]))

            q, kv4, wr, rx, ry, gl = _win_call(xp, mp0, w_in, cos_p, sin_p, ROW_TILE, tps_p)
            cmp = _compress_prompt(kv4, bp, tp, w4, pe4, ccos_p, csin_p)
            o = _pattn(q, gl, cmp, kv4, wr, bp, tp)
            st = jnp.zeros((bp, RG_STATE_ROWS, RG_WIDTH), F32)
            rg, h_last, new_buf = _rglru(rx, ry, st, *rg_args, bp, tp)
            xp = _proj_ln([o, rg], [wo_a, wo_r], xp, mp0, zero_bias, g0, b0, ROW_TILE, tps_p)
            kv_p.append(kv4.reshape(bp, tp, 4, N_KV_A, HEAD_DIM))
            win_p.append(wr.reshape(bp, tp, 2, N_KV_A, HEAD_DIM)[:, tp - min(WINDOW, tp):])
            rh_p.append(h_last.reshape(bp, RG_WIDTH))
            rc_p.append(new_buf)

            q, kv4, wr, rx, ry, gl = _win_call(xs, ms0, w_in, cos_s, sin_s, n_s, 1)
            cmp = _compress_sample(pt_flat, cache_fm, i, bs, npages, w4, pe4, ccos_s, csin_s)
            o = _sattn(pt_flat, cache_fm, i, q, gl, cmp, kv4, wr, win_fm, bs, ts, npages)
            st = jnp.concatenate([state_rglru_h[i][:, None, :],
                                  jnp.zeros((bs, RG_STATE_ROWS - RG_CONV, RG_WIDTH), F32),
                                  state_rglru_conv[i]], axis=1)
            rg, h_last, new_buf = _rglru(rx, ry, st, *rg_args, bs, ts)
            xs = _proj_ln([o, rg], [wo_a, wo_r], xs, ms0, zero_bias, g0, b0, n_s, 1)
            kv_s.append(kv4.reshape(bs, ts, 4, N_KV_A, HEAD_DIM))
            wfull = jnp.concatenate([state_nsa_win[i], wr.reshape(bs, ts, 2, N_KV_A, HEAD_DIM)], axis=1)
            win_s.append(wfull[:, wfull.shape[1] - min(WINDOW, wfull.shape[1]):])
            rh_s.append(h_last.reshape(bs, RG_WIDTH))
            rc_s.append(new_buf)

            w1, w3, w2 = ff_w1[i].astype(BF16), ff_w3[i].astype(BF16), ff_w2[i].astype(BF16)
            xp = _ffn(xp, mp1, w1, w3, w2, g1, b1, ROW_TILE, tps_p)
            xs = _ffn(xs, ms1, w1, w3, w2, g1, b1, n_s, 1)
        else:
            cw1 = cf_w1[i].astype(BF16)
            cw2 = cf_w2[i].astype(BF16)
            dw = jnp.concatenate([cf_dw[i], jnp.zeros((CF_HALO - CF_KERNEL, d), F32)], axis=0)
            conv_args = (dw, vec(cf_db[i]), vec(cf_ln_g[i]), vec(cf_ln_b[i]))

            glu = _cf_in(xp, mp0, cw1, vec(cf_b1[i]), ROW_TILE, tps_p)
            per = CONV_TILE // CF_HALO
            z = _cf_conv(glu, glu, lambda bi, ti: (jnp.maximum((bi * (tp // CONV_TILE) + ti) * per - 1, 0), 0),
                         *conv_args, bp, tp, CONV_TILE, True)
            xp = _proj_ln([z], [cw2], xp, mp0, vec(cf_b2[i]), g0, b0, ROW_TILE, tps_p)
            cc_p.append(glu.reshape(bp, tp, d)[:, tp - (CF_KERNEL - 1):])

            glu = _cf_in(xs, ms0, cw1, vec(cf_b1[i]), n_s, 1)
            halo = jnp.concatenate([jnp.zeros((bs, CF_HALO - (CF_KERNEL - 1), d), F32),
                                    state_conformer_conv[i]], axis=1).reshape(bs * CF_HALO, d)
            z = _cf_conv(glu, halo, lambda bi, ti: (bi, 0), *conv_args, bs, ts, ts, False)
            xs = _proj_ln([z], [cw2], xs, ms0, vec(cf_b2[i]), g0, b0, n_s, 1)
            cc_s.append(jnp.concatenate([state_conformer_conv[i], glu.reshape(bs, ts, d)],
                                        axis=1)[:, -(CF_KERNEL - 1):])

            rw = jnp.concatenate([moe_router[i], jnp.zeros((d, LANES - N_EXPERTS), F32)], axis=1)
            h_all, ids_p, gates_p = _router(xp, mp1, rw, ROW_TILE, tps_p, n_tok, 0, None)
            h_all, ids_s, gates_s = _router(xs, ms1, rw, n_s, 1, n_tok, n_p, h_all)
            plan, blk_e, cnt, nused, n_blk = _route_plan(jnp.concatenate([ids_p, ids_s], axis=0), n_tok)
            ys = _experts(blk_e, plan, cnt, nused, h_all, n_tok, moe_w1[i].astype(BF16),
                          moe_w3[i].astype(BF16), moe_w2[i].astype(BF16), n_blk)
            xp = _combine(ys, gates_p, xp, mp1, g1, b1, tm_c, tp // tm_c, 0, n_tok)
            xs = _combine(ys, gates_s, xs, ms1, g1, b1, tm_c, 1, n_p, n_tok)
    return (xp.reshape(bp, tp, d), xs.reshape(bs, ts, d), jnp.stack(kv_p), jnp.stack(kv_s),
            jnp.stack(win_p), jnp.stack(win_s), jnp.stack(rh_p), jnp.stack(rh_s),
            jnp.stack(rc_p), jnp.stack(rc_s), jnp.stack(cc_p), jnp.stack(cc_s))
```

```python
import functools

import jax
import jax.numpy as jnp
from jax import lax
from jax.experimental import pallas as pl
from jax.experimental.pallas import tpu as pltpu

F32 = jnp.float32
BF16 = jnp.bfloat16
I32 = jnp.int32

D_MODEL = 1024
HEAD_DIM = 64
N_HEADS_A = 8
N_KV_A = 2
HEADS_PER_GROUP = N_HEADS_A // N_KV_A
GROUP_COLS = N_KV_A * HEAD_DIM
CMP_BLOCK = 32
SEL_BLOCK = 64
TOP_N = 16
WINDOW = 512
ROPE_THETA = 10000.0
PAGE_SIZE = 128
RG_WIDTH = 512
RG_BLOCKS = 8
RG_CONV = 4
RG_C = 8.0
NSA_WIDTH = 512
Q_COLS = NSA_WIDTH
KV_COLS = 6 * GROUP_COLS
GATE_COLS = 3 * N_HEADS_A
CF_KERNEL = 31
CF_HALO = 32
N_EXPERTS = 8
DEPTH = 4
ALPHA = (2.0 * DEPTH) ** 0.25
LN_EPS = 1e-5
NEG = -1e30
FORCE_SCORE = 1e4
ATTN_SCALE = HEAD_DIM ** -0.5

LANES = 128
SUBLANES = 8
VMEM_LIMIT = 56 * 1024 * 1024

ROW_TILE = 512
Q_TILE = 256
CONV_TILE = 256
CONV_SUB = 32
MOE_TILE = 512
FF_CHUNK = 256
MOE_FF_CHUNK = 512


def _cparams(n_axes, **kw):
    return pltpu.CompilerParams(dimension_semantics=("arbitrary",) * n_axes,
                                vmem_limit_bytes=VMEM_LIMIT, **kw)


def _const_spec(shape, single_buffer=False):
    n = len(shape)
    if single_buffer:
        return pl.BlockSpec(shape, lambda *a: (0,) * n, pipeline_mode=pl.Buffered(1))
    return pl.BlockSpec(shape, lambda *a: (0,) * n)


def _mod_spec(mod3, tm, tps):
    if mod3.shape[1] == 1:
        return pl.BlockSpec((1, 1, mod3.shape[2]), lambda i: (i // tps, 0, 0))
    return pl.BlockSpec((1, tm, mod3.shape[2]), lambda i: (0, i, 0))


def _dot(a, b):
    return jnp.dot(a, b, preferred_element_type=F32)


def _dot_nt(a, b):
    return lax.dot_general(a, b, (((1,), (1,)), ((), ())), preferred_element_type=F32)


def _layer_norm(y, g, b):
    mu = jnp.mean(y, axis=-1, keepdims=True)
    yc = y - mu
    var = jnp.mean(yc * yc, axis=-1, keepdims=True)
    return yc * lax.rsqrt(var + LN_EPS) * g + b


def _post_norm(x, mod, out, g, b):
    gate = mod[:, 2 * D_MODEL:]
    return _layer_norm(ALPHA * x + (1.0 + gate) * out, g, b)


def _modulate(x, mod):
    return x * (1.0 + mod[:, D_MODEL:2 * D_MODEL]) + mod[:, :D_MODEL]


def _silu(x):
    return x * jax.nn.sigmoid(x)


def _rope128(v, cos, sin_signed):
    lane = lax.broadcasted_iota(I32, v.shape, 1)
    from_hi = pltpu.roll(v, LANES - HEAD_DIM // 2, 1)
    from_lo = pltpu.roll(v, HEAD_DIM // 2, 1)
    swapped = jnp.where((lane & (HEAD_DIM - 1)) < HEAD_DIM // 2, from_hi, from_lo)
    return v * cos + swapped * sin_signed


def _rope(v, cos, sin_signed):
    k = v.shape[1] // LANES
    parts = [_rope128(v[:, i * LANES:(i + 1) * LANES], cos, sin_signed) for i in range(k)]
    return parts[0] if k == 1 else jnp.concatenate(parts, axis=1)


def _softmax_parts(parts):
    masked = [jnp.where(m, s, NEG) for s, m in parts]
    mx = functools.reduce(jnp.maximum, [jnp.max(s, axis=-1, keepdims=True) for s in masked])
    es = [jnp.where(m, jnp.exp(s - mx), 0.0) for s, (_, m) in zip(masked, parts)]
    den = functools.reduce(jnp.add, [jnp.sum(e, axis=-1, keepdims=True) for e in es])
    return es, 1.0 / jnp.maximum(den, 1e-30)


def _attend(qb, pieces):
    es, inv = _softmax_parts([(_dot(qb, k) if fm else _dot_nt(qb, k), m) for k, _, m, fm in pieces])
    o = functools.reduce(jnp.add, [_dot_nt(e.astype(BF16), v) if fm else _dot(e.astype(BF16), v)
                                   for e, (_, v, _, fm) in zip(es, pieces)])
    return o * inv


def _topk_mask(vals, n_valid, kk):
    lane = lax.broadcasted_iota(I32, vals.shape, 1)
    rank = jnp.zeros(vals.shape, I32)
    for m in range(n_valid):
        col = vals[:, m:m + 1]
        later = jnp.where(lane > m, 1, 0)
        rank = rank + jnp.where(col > vals, 1, jnp.where(col == vals, later, 0))
    return (rank < kk) & (lane < n_valid)


def _mod_kernel(c_ref, w_ref, b_ref, o_ref):
    s = _silu(c_ref[...]).astype(BF16)
    o_ref[0] = _dot(s, w_ref[0].astype(BF16)) + b_ref[0]


def _mod_all(c_all, w_mod, b_mod):
    n = c_all.shape[0]
    nl = w_mod.shape[0] * w_mod.shape[1]
    w = w_mod.reshape(nl, D_MODEL, 3 * D_MODEL)
    b = b_mod.reshape(nl, 1, 3 * D_MODEL)
    return pl.pallas_call(
        _mod_kernel, grid=(nl, 3),
        in_specs=[pl.BlockSpec((n, D_MODEL), lambda l, j: (0, 0)),
                  pl.BlockSpec((1, D_MODEL, D_MODEL), lambda l, j: (l, 0, j)),
                  pl.BlockSpec((1, 1, D_MODEL), lambda l, j: (l, 0, j))],
        out_specs=pl.BlockSpec((1, n, D_MODEL), lambda l, j: (l, 0, j)),
        out_shape=jax.ShapeDtypeStruct((nl, n, 3 * D_MODEL), F32),
        compiler_params=_cparams(2), name="mod_all")(c_all, w, b)


_O_KV = Q_COLS
_O_RX = _O_KV + KV_COLS
_O_RY = _O_RX + RG_WIDTH
_O_GL = _O_RY + RG_WIDTH
W_IN_COLS = _O_GL + LANES


def _win_kernel(x_ref, mod_ref, w_ref, cos_ref, sin_ref, q_ref, kv_ref, wr_ref, rx_ref, ry_ref, gl_ref,
                *fm_refs):
    h = _modulate(x_ref[...], mod_ref[0]).astype(BF16)
    u = _dot(h, w_ref[...])
    cos = cos_ref[...]
    sin = sin_ref[...]
    q_ref[...] = _rope(u[:, :Q_COLS], cos, sin)
    c = _O_KV
    kv = jnp.concatenate([u[:, c:c + 2 * LANES],
                          _rope128(u[:, c + 2 * LANES:c + 3 * LANES], cos, sin),
                          u[:, c + 3 * LANES:c + 4 * LANES]], axis=1)
    wr = jnp.concatenate([_rope128(u[:, c + 4 * LANES:c + 5 * LANES], cos, sin),
                          u[:, c + 5 * LANES:c + 6 * LANES]], axis=1)
    kv_ref[...] = kv
    wr_ref[...] = wr
    rx_ref[...] = u[:, _O_RX:_O_RY]
    ry_ref[...] = u[:, _O_RY:_O_GL]
    gl_ref[...] = u[:, _O_GL:]
    if fm_refs:
        kvt_ref, wrt_ref = fm_refs
        kvt_ref[...] = kv.T
        wrt_ref[...] = wr.T


def _win_call(x2d, mod3, w, cos, sin, tm, tps, t_len=None):
    m = x2d.shape[0]
    row = lambda width: pl.BlockSpec((tm, width), lambda i: (i, 0))
    widths = (Q_COLS, 4 * LANES, 2 * LANES, RG_WIDTH, RG_WIDTH, LANES)
    out_specs = [row(wd) for wd in widths]
    out_shape = [jax.ShapeDtypeStruct((m, wd), F32) for wd in widths]
    if t_len is not None:
        for feat in (4 * LANES, 2 * LANES):
            out_specs.append(pl.BlockSpec((feat, tm), lambda i: (i // tps, i % tps)))
            out_shape.append(jax.ShapeDtypeStruct((m // t_len * feat, t_len), F32))
    return pl.pallas_call(
        _win_kernel, grid=(m // tm,),
        in_specs=[row(D_MODEL), _mod_spec(mod3, tm, tps), _const_spec(w.shape),
                  pl.BlockSpec((tm, LANES), lambda i: (i % tps, 0)),
                  pl.BlockSpec((tm, LANES), lambda i: (i % tps, 0))],
        out_specs=out_specs, out_shape=out_shape,
        compiler_params=_cparams(1), name="mixer_in")(x2d, mod3, w, cos, sin)


def _compress_core(k_ref, v_ref, w4_ref, pe4_ref, cos_ref, sin_ref, out_ref, nb):
    half = nb // 2
    acc = jnp.zeros((nb, 2 * LANES), F32)
    for l in range(CMP_BLOCK):
        even = pl.ds(l, half, stride=2 * CMP_BLOCK)
        odd = pl.ds(CMP_BLOCK + l, half, stride=2 * CMP_BLOCK)
        xk = jnp.concatenate([k_ref[even, :], k_ref[odd, :]], axis=0)
        xv = jnp.concatenate([v_ref[even, :], v_ref[odd, :]], axis=0)
        x = (jnp.concatenate([xk, xv], axis=1) + pe4_ref[pl.ds(l, 1), :]).astype(BF16)
        acc = acc + _dot(x, w4_ref[l])
    out_ref[:, 0:LANES] = _rope128(acc[:, 0:LANES], cos_ref[...], sin_ref[...])
    out_ref[:, LANES:2 * LANES] = acc[:, LANES:2 * LANES]


def _compress_prompt_kernel(k_ref, v_ref, w4_ref, pe4_ref, cos_ref, sin_ref, out_ref, *, nb):
    _compress_core(k_ref, v_ref, w4_ref, pe4_ref, cos_ref, sin_ref, out_ref, nb)


def _compress_prompt(kv4, bsz, t, w4, pe4, ccos, csin):
    nb = t // CMP_BLOCK
    return pl.pallas_call(
        functools.partial(_compress_prompt_kernel, nb=nb), grid=(bsz,),
        in_specs=[pl.BlockSpec((t, LANES), lambda b: (b, 0)), pl.BlockSpec((t, LANES), lambda b: (b, 1)),
                  _const_spec(w4.shape), _const_spec(pe4.shape),
                  _const_spec(ccos.shape), _const_spec(csin.shape)],
        out_specs=pl.BlockSpec((nb, 2 * LANES), lambda b: (b, 0)),
        out_shape=jax.ShapeDtypeStruct((bsz * nb, 2 * LANES), F32),
        compiler_params=_cparams(1), name="compress_prompt")(kv4, kv4, w4, pe4, ccos, csin)


def _prefetch_pages(pt_ref, cache_ref, sems, layer, feat0, npages, dst_fn):
    b = pl.program_id(0)

    def copy(seq, j):
        slot = seq % 2
        src = cache_ref.at[layer, pt_ref[seq * npages + j], pl.ds(feat0, 2 * LANES), :]
        return pltpu.make_async_copy(src, dst_fn(slot, j), sems.at[slot])

    def start_seq(seq):
        def body(j, c):
            copy(seq, j).start()
            return c
        lax.fori_loop(0, npages, body, 0)

    @pl.when(b == 0)
    def _():
        start_seq(b)

    @pl.when(b + 1 < pl.num_programs(0))
    def _():
        start_seq(b + 1)

    def wait(j, c):
        copy(b, j).wait()
        return c
    lax.fori_loop(0, npages, wait, 0)
    return b % 2


def _compress_sample_kernel(pt_ref, cache_ref, w4_ref, pe4_ref, cos_ref, sin_ref, out_ref, raw, kbuf, vbuf, sems,
                            *, layer, npages):
    def dst(slot, j):
        return raw.at[slot, pl.ds(pl.multiple_of(j * 2 * LANES, 2 * LANES), 2 * LANES), :]
    slot = _prefetch_pages(pt_ref, cache_ref, sems, layer, 0, npages, dst)

    def to_token_major(j, c):
        r0 = pl.multiple_of(j * 2 * LANES, 2 * LANES)
        t0 = pl.multiple_of(j * PAGE_SIZE, PAGE_SIZE)
        kbuf[pl.ds(t0, PAGE_SIZE), :] = raw[slot, pl.ds(r0, LANES), :].T
        vbuf[pl.ds(t0, PAGE_SIZE), :] = raw[slot, pl.ds(r0 + LANES, LANES), :].T
        return c
    lax.fori_loop(0, npages, to_token_major, 0, unroll=4)
    _compress_core(kbuf, vbuf, w4_ref, pe4_ref, cos_ref, sin_ref, out_ref, npages * PAGE_SIZE // CMP_BLOCK)


def _compress_sample(pt_flat, cache_fm, layer, bsz, npages, w4, pe4, ccos, csin):
    past = npages * PAGE_SIZE
    nb = past // CMP_BLOCK
    grid_spec = pltpu.PrefetchScalarGridSpec(
        num_scalar_prefetch=1, grid=(bsz,),
        in_specs=[pl.BlockSpec(memory_space=pl.ANY),
                  _const_spec(w4.shape), _const_spec(pe4.shape),
                  _const_spec(ccos.shape), _const_spec(csin.shape)],
        out_specs=pl.BlockSpec((nb, 2 * LANES), lambda b, pt: (b, 0)),
        scratch_shapes=[pltpu.VMEM((2, npages * 2 * LANES, PAGE_SIZE), F32),
                        pltpu.VMEM((past, LANES), F32), pltpu.VMEM((past, LANES), F32),
                        pltpu.SemaphoreType.DMA((2,))])
    return pl.pallas_call(
        functools.partial(_compress_sample_kernel, layer=layer, npages=npages),
        grid_spec=grid_spec,
        out_shape=jax.ShapeDtypeStruct((bsz * nb, 2 * LANES), F32),
        compiler_params=_cparams(1, disable_bounds_checks=True),
        name="compress_sample")(pt_flat, cache_fm, w4, pe4, ccos, csin)


def _cmp_positions(nc):
    n = lax.broadcasted_iota(I32, (1, nc), 1)
    half = nc // 2
    blk = jnp.where(n < half, 2 * n, 2 * (n - half) + 1)
    return blk * CMP_BLOCK + (CMP_BLOCK - 1)


def _softmax_cols(s, mask):
    s = jnp.where(mask, s, NEG)
    e = jnp.where(mask, jnp.exp(s - jnp.max(s, axis=0, keepdims=True)), 0.0)
    return e, 1.0 / jnp.maximum(jnp.sum(e, axis=0, keepdims=True), 1e-30)


def _pattn_kernel(q_ref, gl_ref, cmp_ref, ks_ref, vst_ref, kw0_ref, kw1_ref, kw2_ref, vwt0_ref, vwt1_ref,
                  vwt2_ref, o_ref, ks_bf, vst_bf, bias_ref, acc_ref, ot_ref, *, tq, t_len):
    ti = pl.program_id(1)
    nc = t_len // CMP_BLOCK
    nsel = t_len // SEL_BLOCK
    half = nc // 2
    blk_per_chunk = tq // SEL_BLOCK
    q0 = ti * tq

    @pl.when(ti == 0)
    def _():
        for g in range(N_KV_A):
            ks_bf[g] = ks_ref[:, g * HEAD_DIM:(g + 1) * HEAD_DIM].astype(BF16)
        vst_bf[...] = vst_ref[...].astype(BF16)

    qj = lax.broadcasted_iota(I32, (1, tq), 1)
    ki = lax.broadcasted_iota(I32, (tq, 1), 0)
    qpos = q0 + qj
    qs = (q_ref[...] * ATTN_SCALE).astype(BF16)
    gates_t = jax.nn.sigmoid(gl_ref[...]).T
    n = lax.broadcasted_iota(I32, (nc, 1), 0)
    cpos = jnp.where(n < half, 2 * n, 2 * (n - half) + 1) * CMP_BLOCK + (CMP_BLOCK - 1)
    m_c = cpos <= qpos
    blk = lax.broadcasted_iota(I32, (nsel, 1), 0)
    cur = qpos // SEL_BLOCK
    forced = (blk == cur) | (blk == 0)
    future = blk > cur
    later_blk = [jnp.where(blk > m, 1, 0) for m in range(nsel)]
    diag_bias = jnp.where(ki <= qj, 0.0, NEG)
    w_masks = []
    for k in range(3):
        dpos = qj - ki + (2 - k) * tq
        w_masks.append((dpos >= 0) & (dpos < WINDOW) & (ti + k - 2 >= 0))
    w_bias = jnp.where(jnp.concatenate(w_masks, axis=0), 0.0, NEG)
    kw_refs = (kw0_ref, kw1_ref, kw2_ref)
    vwt_refs = (vwt0_ref, vwt1_ref, vwt2_ref)

    for g in range(N_KV_A):
        ck = slice(g * HEAD_DIM, (g + 1) * HEAD_DIM)
        kc = cmp_ref[:, ck].astype(BF16)
        vc = cmp_ref[:, LANES + g * HEAD_DIM:LANES + (g + 1) * HEAD_DIM].astype(BF16)
        heads = [g * HEADS_PER_GROUP + r for r in range(HEADS_PER_GROUP)]
        q_heads = [qs[:, h * HEAD_DIM:(h + 1) * HEAD_DIM] for h in heads]
        o_cmp = []
        imp = jnp.zeros((nc, tq), F32)
        for s in [_dot_nt(kc, qh) for qh in q_heads]:
            e, inv = _softmax_cols(s, m_c)
            p = e * inv
            o_cmp.append(lax.dot_general(vc, p.astype(BF16), (((0,), (0,)), ((), ())),
                                         preferred_element_type=F32))
            imp = imp + p
        imp = imp[:half] + imp[half:]
        vals = jnp.where(forced, FORCE_SCORE, jnp.where(future, -1.0, imp))
        rank = jnp.zeros((nsel, tq), I32)
        for m in range(nsel):
            row = vals[m:m + 1, :]
            rank = rank + jnp.where(row > vals, 1, jnp.where(row == vals, later_blk[m], 0))
        bias_ref[g] = jnp.where(rank < min(TOP_N, nsel), 0.0, NEG)

        kw = jnp.concatenate([r[:, ck] for r in kw_refs], axis=0).astype(BF16)
        vwt = [r[ck, :].astype(BF16) for r in vwt_refs]

        def block_bias(c):
            rows = [jnp.broadcast_to(bias_ref[g, pl.ds(c * blk_per_chunk + j, 1), :], (SEL_BLOCK, tq))
                    for j in range(blk_per_chunk)]
            return jnp.concatenate(rows, axis=0)

        def chunk(c, carry, extra_bias=None):
            r0 = pl.multiple_of(c * tq, tq)
            k_chunk = ks_bf[g, pl.ds(r0, tq), :]
            v_chunk = vst_bf[ck, pl.ds(r0, tq)]
            bias = block_bias(c)
            if extra_bias is not None:
                bias = bias + extra_bias
            scores = [_dot_nt(k_chunk, qh) for qh in q_heads]
            out, probs, alphas = [], [], []
            for r in range(HEADS_PER_GROUP):
                m_run, l_run = carry[r]
                s = scores[r] + bias
                m_new = jnp.maximum(m_run, jnp.max(s, axis=0, keepdims=True))
                alpha = jnp.exp(m_run - m_new)
                p = jnp.exp(s - m_new)
                out.append((m_new, alpha * l_run + jnp.sum(p, axis=0, keepdims=True)))
                probs.append(p.astype(BF16))
                alphas.append(alpha)
            for r in range(HEADS_PER_GROUP):
                acc_ref[r] = alphas[r] * acc_ref[r] + _dot(v_chunk, probs[r])
            return tuple(out)

        acc_ref[...] = jnp.zeros(acc_ref.shape, F32)
        init = tuple((jnp.full((1, tq), -jnp.inf, F32), jnp.zeros((1, tq), F32)) for _ in heads)
        stats = chunk(ti, lax.fori_loop(0, ti, chunk, init), diag_bias)

        w_scores = [_dot_nt(kw, qh) + w_bias for qh in q_heads]
        w_probs = [jnp.exp(s - jnp.max(s, axis=0, keepdims=True)) for s in w_scores]
        for r, h in enumerate(heads):
            o_s = acc_ref[r] * (1.0 / jnp.maximum(stats[r][1], 1e-30))
            e = w_probs[r]
            inv = 1.0 / jnp.maximum(jnp.sum(e, axis=0, keepdims=True), 1e-30)
            e = e.astype(BF16)
            o_w = functools.reduce(jnp.add, [_dot(vwt[k], e[k * tq:(k + 1) * tq]) for k in range(3)]) * inv
            ot_ref[h * HEAD_DIM:(h + 1) * HEAD_DIM, :] = (
                gates_t[3 * h:3 * h + 1, :] * o_cmp[r] + gates_t[3 * h + 1:3 * h + 2, :] * o_s
                + gates_t[3 * h + 2:3 * h + 3, :] * o_w)
    o_ref[...] = ot_ref[...].T


def _block_expand(nblk, nkeys):
    return (jnp.arange(nkeys)[None, :] // SEL_BLOCK == jnp.arange(nblk)[:, None]).astype(BF16)


def _pattn(q, gl, cmp, kv4, kvt, wr, wrt, bsz, t):
    tq = Q_TILE
    nt = t // tq
    nc = t // CMP_BLOCK
    nsel = t // SEL_BLOCK
    row = lambda width: pl.BlockSpec((tq, width), lambda b, i: (b * nt + i, 0))
    kw = lambda back: pl.BlockSpec((tq, LANES), lambda b, i: (b * nt + jnp.maximum(i - back, 0), 0))
    vwt = lambda back: pl.BlockSpec((LANES, tq), lambda b, i: (b * 2 + 1, jnp.maximum(i - back, 0)))
    return pl.pallas_call(
        functools.partial(_pattn_kernel, tq=tq, t_len=t), grid=(bsz, nt),
        in_specs=[row(Q_COLS), row(LANES),
                  pl.BlockSpec((nc, 2 * LANES), lambda b, i: (b, 0)),
                  pl.BlockSpec((t, LANES), lambda b, i: (b, 2)),
                  pl.BlockSpec((LANES, t), lambda b, i: (b * 4 + 3, 0)),
                  kw(2), kw(1), kw(0), vwt(2), vwt(1), vwt(0)],
        out_specs=row(NSA_WIDTH),
        out_shape=jax.ShapeDtypeStruct((bsz * t, NSA_WIDTH), F32),
        scratch_shapes=[pltpu.VMEM((N_KV_A, t, HEAD_DIM), BF16), pltpu.VMEM((LANES, t), BF16),
                        pltpu.VMEM((N_KV_A, nsel, tq), F32), pltpu.VMEM((HEADS_PER_GROUP, HEAD_DIM, tq), F32),
                        pltpu.VMEM((NSA_WIDTH, tq), F32)],
        compiler_params=_cparams(2), name="nsa_prompt")(q, gl, cmp, kv4, kvt, wr, wr, wr, wrt, wrt, wrt)


def _sattn_kernel(pt_ref, cache_ref, q_ref, gl_ref, cmp_ref, kvn_ref, wrn_ref, win_ref, e_ref, o_ref,
                  buf, sems, *, layer, npages, ts):
    past = npages * PAGE_SIZE

    def dst(slot, j):
        return buf.at[slot, :, pl.ds(pl.multiple_of(j * PAGE_SIZE, PAGE_SIZE), PAGE_SIZE)]
    slot = _prefetch_pages(pt_ref, cache_ref, sems, layer, 2 * LANES, npages, dst)
    nc = past // CMP_BLOCK
    nblk_past = past // SEL_BLOCK
    nsel = nblk_past + 1
    lanes_sel = 2 * nblk_past
    rows = HEADS_PER_GROUP * ts
    tok1 = lax.broadcasted_iota(I32, (ts, 1), 0)
    tok = jnp.concatenate([tok1] * HEADS_PER_GROUP, axis=0)
    qpos1 = past + tok1
    qpos = past + tok
    qs = q_ref[...] * ATTN_SCALE
    gates = jax.nn.sigmoid(gl_ref[...])
    m_c = _cmp_positions(nc) <= qpos
    blk = lax.broadcasted_iota(I32, (1, lanes_sel), 1)
    cur = qpos1 // SEL_BLOCK
    forced = (blk == cur) | (blk == 0)
    future = blk > cur
    m_past = lax.broadcasted_iota(I32, (1, past), 1) <= qpos
    tkey = lax.broadcasted_iota(I32, (1, ts), 1)
    m_new = tkey <= tok
    win_len = win_ref.shape[1]
    wpos = past - win_len + lax.broadcasted_iota(I32, (1, win_len), 1)
    dpos = qpos - wpos
    m_wstate = (dpos >= 0) & (dpos < WINDOW) & (wpos >= 0)
    dnew = tok - tkey
    m_wnew = (dnew >= 0) & (dnew < WINDOW)
    for g in range(N_KV_A):
        ck = slice(g * HEAD_DIM, (g + 1) * HEAD_DIM)
        cv = slice(LANES + g * HEAD_DIM, LANES + (g + 1) * HEAD_DIM)
        heads = [g * HEADS_PER_GROUP + r for r in range(HEADS_PER_GROUP)]
        qg = jnp.concatenate([qs[:, h * HEAD_DIM:(h + 1) * HEAD_DIM] for h in heads], axis=0).astype(BF16)
        kc = cmp_ref[:, ck].astype(BF16)
        vc = cmp_ref[:, cv].astype(BF16)
        (e,), inv = _softmax_parts([(_dot_nt(qg, kc), m_c)])
        p = e * inv
        o_c = _dot(p.astype(BF16), vc)
        imp = p[0:ts]
        for r in range(1, HEADS_PER_GROUP):
            imp = imp + p[r * ts:(r + 1) * ts]
        imp = imp[:, :nc // 2] + imp[:, nc // 2:]
        imp = jnp.concatenate([imp, jnp.zeros((ts, lanes_sel - nc // 2), F32)], axis=1)
        vals = jnp.where(forced, FORCE_SCORE, jnp.where(future, -1.0, imp))
        sel = jnp.where(_topk_mask(vals, nsel, min(TOP_N, nsel)), 1.0, 0.0)
        sel = jnp.concatenate([sel] * HEADS_PER_GROUP, axis=0)
        sel_keys = _dot(sel[:, :nblk_past].astype(BF16), e_ref[...])
        m_s_past = (sel_keys > 0.5) & m_past
        m_s_new = (sel[:, nblk_past:nblk_past + 1] > 0.5) & m_new
        kn = kvn_ref[:, 2 * LANES:4 * LANES]
        o_s = _attend(qg, [(buf[slot, ck, :].astype(BF16), buf[slot, cv, :].astype(BF16), m_s_past, True),
                           (kn[:, ck].astype(BF16), kn[:, cv].astype(BF16), m_s_new, False)])
        o_w = _attend(qg, [(win_ref[ck, :].astype(BF16), win_ref[cv, :].astype(BF16), m_wstate, True),
                           (wrn_ref[:, ck].astype(BF16), wrn_ref[:, cv].astype(BF16), m_wnew, False)])
        for r, h in enumerate(heads):
            rs = slice(r * ts, (r + 1) * ts)
            o_ref[:, h * HEAD_DIM:(h + 1) * HEAD_DIM] = (
                gates[:, 3 * h:3 * h + 1] * o_c[rs] + gates[:, 3 * h + 1:3 * h + 2] * o_s[rs]
                + gates[:, 3 * h + 2:3 * h + 3] * o_w[rs])


def _sattn(pt_flat, cache_fm, layer, q, gl, cmp, kv4, wr, win_fm, bsz, ts, npages):
    past = npages * PAGE_SIZE
    nc = past // CMP_BLOCK
    win_len = win_fm.shape[3]
    expand = _block_expand(past // SEL_BLOCK, past)
    row = lambda width: pl.BlockSpec((ts, width), lambda b, pt: (b, 0))
    grid_spec = pltpu.PrefetchScalarGridSpec(
        num_scalar_prefetch=1, grid=(bsz,),
        in_specs=[pl.BlockSpec(memory_space=pl.ANY), row(Q_COLS), row(LANES),
                  pl.BlockSpec((nc, 2 * LANES), lambda b, pt: (b, 0)),
                  row(4 * LANES), row(2 * LANES),
                  pl.BlockSpec((None, None, 2 * LANES, win_len), lambda b, pt: (layer, b, 0, 0)),
                  _const_spec(expand.shape)],
        out_specs=row(NSA_WIDTH),
        scratch_shapes=[pltpu.VMEM((2, 2 * LANES, past), F32), pltpu.SemaphoreType.DMA((2,))])
    return pl.pallas_call(
        functools.partial(_sattn_kernel, layer=layer, npages=npages, ts=ts),
        grid_spec=grid_spec,
        out_shape=jax.ShapeDtypeStruct((bsz * ts, NSA_WIDTH), F32),
        compiler_params=_cparams(1, disable_bounds_checks=True),
        name="nsa_sample")(pt_flat, cache_fm, q, gl, cmp, kv4, wr, win_fm, expand)


RG_STATE_ROWS = SUBLANES
RG_ROWS = 256


def _rglru_kernel(rx_ref, ry_ref, st_ref, cw_ref, cb_ref, wa_ref, ba_ref, wx_ref, bx_ref, lam_ref,
                  out_ref, hl_ref, nb_ref, xs, a_s, b_s, *, t_len):
    ch = min(RG_ROWS, t_len)
    xs[0:RG_STATE_ROWS, :] = st_ref[0]
    xs[RG_STATE_ROWS:RG_STATE_ROWS + t_len, :] = rx_ref[...]
    lam = lam_ref[...]
    softplus_neg_lam = jnp.maximum(-lam, 0.0) + jnp.log1p(jnp.exp(-jnp.abs(lam)))
    sub = lax.broadcasted_iota(I32, (ch, RG_WIDTH), 0) & (SUBLANES - 1)
    for c in range(t_len // ch):
        r0 = c * ch
        xc = cb_ref[...]
        for j in range(RG_CONV):
            xc = xc + cw_ref[j:j + 1, :] * xs[r0 + RG_STATE_ROWS - (RG_CONV - 1) + j:
                                              r0 + RG_STATE_ROWS - (RG_CONV - 1) + j + ch, :]
        xb = xc.astype(BF16)
        r = jax.nn.sigmoid(_dot(xb, wa_ref[...]) + ba_ref[...])
        i = jax.nn.sigmoid(_dot(xb, wx_ref[...]) + bx_ref[...])
        log_a = -RG_C * r * softplus_neg_lam
        a = jnp.exp(log_a)
        one_minus_a2 = -jnp.tanh(log_a) * (jnp.exp(2.0 * log_a) + 1.0)
        bb = jnp.sqrt(one_minus_a2) * (i * xc)
        for s in (1, 2, 4):
            ok = sub >= s
            a_prev = pltpu.roll(a, s, 0)
            b_prev = pltpu.roll(bb, s, 0)
            bb = jnp.where(ok, a * b_prev + bb, bb)
            a = jnp.where(ok, a * a_prev, a)
        a_s[r0:r0 + ch, :] = a
        b_s[r0:r0 + ch, :] = bb

    def step(k, h):
        r0 = pl.multiple_of(k * SUBLANES, SUBLANES)
        hk = b_s[pl.ds(r0, SUBLANES), :] + a_s[pl.ds(r0, SUBLANES), :] * h
        b_s[pl.ds(r0, SUBLANES), :] = hk
        return jnp.broadcast_to(hk[SUBLANES - 1:SUBLANES, :], (SUBLANES, RG_WIDTH))

    h0 = jnp.broadcast_to(st_ref[0, 0:1, :], (SUBLANES, RG_WIDTH))
    h_fin = lax.fori_loop(0, t_len // SUBLANES, step, h0)
    hl_ref[0] = h_fin[0:1, :]
    nb_ref[0] = xs[RG_STATE_ROWS + t_len - (RG_CONV - 1):RG_STATE_ROWS + t_len, :]
    for c in range(t_len // ch):
        r0 = c * ch
        out_ref[r0:r0 + ch, :] = b_s[r0:r0 + ch, :] * jax.nn.gelu(ry_ref[r0:r0 + ch, :])


def _rglru(rx, ry, st, cw, cb, wa, ba, wx, bx, lam, bsz, t):
    row = pl.BlockSpec((t, RG_WIDTH), lambda b: (b, 0))
    vec = _const_spec((1, RG_WIDTH))
    return pl.pallas_call(
        functools.partial(_rglru_kernel, t_len=t), grid=(bsz,),
        in_specs=[row, row, pl.BlockSpec((1, RG_STATE_ROWS, RG_WIDTH), lambda b: (b, 0, 0)),
                  _const_spec(cw.shape), vec, _const_spec(wa.shape), vec, _const_spec(wx.shape), vec, vec],
        out_specs=[row, pl.BlockSpec((1, 1, RG_WIDTH), lambda b: (b, 0, 0)),
                   pl.BlockSpec((1, RG_CONV - 1, RG_WIDTH), lambda b: (b, 0, 0))],
        out_shape=[jax.ShapeDtypeStruct((bsz * t, RG_WIDTH), F32),
                   jax.ShapeDtypeStruct((bsz, 1, RG_WIDTH), F32),
                   jax.ShapeDtypeStruct((bsz, RG_CONV - 1, RG_WIDTH), F32)],
        scratch_shapes=[pltpu.VMEM((RG_STATE_ROWS + t, RG_WIDTH), F32),
                        pltpu.VMEM((t, RG_WIDTH), F32), pltpu.VMEM((t, RG_WIDTH), F32)],
        compiler_params=_cparams(1), name="rglru")(rx, ry, st, cw, cb, wa, ba, wx, bx, lam)


def _proj_ln_kernel(*refs, n_in):
    a_refs = refs[:n_in]
    w_refs = refs[n_in:2 * n_in]
    x_ref, mod_ref, bias_ref, g_ref, b_ref, o_ref = refs[2 * n_in:]
    out = bias_ref[...]
    for a_ref, w_ref in zip(a_refs, w_refs):
        out = out + _dot(a_ref[...].astype(BF16), w_ref[...])
    o_ref[...] = _post_norm(x_ref[...], mod_ref[0], out, g_ref[...], b_ref[...])


def _proj_ln(a_list, w_list, x2d, mod3, bias, g, b, tm, tps):
    m = x2d.shape[0]
    n_in = len(a_list)
    row = lambda width: pl.BlockSpec((tm, width), lambda i: (i, 0))
    vec = _const_spec((1, D_MODEL))
    return pl.pallas_call(
        functools.partial(_proj_ln_kernel, n_in=n_in), grid=(m // tm,),
        in_specs=[row(a.shape[1]) for a in a_list] + [_const_spec(w.shape) for w in w_list]
        + [row(D_MODEL), _mod_spec(mod3, tm, tps), vec, vec, vec],
        out_specs=row(D_MODEL),
        out_shape=jax.ShapeDtypeStruct((m, D_MODEL), F32),
        compiler_params=_cparams(1), name="proj_postnorm")(*a_list, *w_list, x2d, mod3, bias, g, b)


def _ffn_kernel(x_ref, mod_ref, w1_ref, w3_ref, w2_ref, g_ref, b_ref, o_ref, *, ff):
    x = x_ref[...]
    mod = mod_ref[0]
    h = _modulate(x, mod).astype(BF16)
    acc = jnp.zeros(x.shape, F32)
    for c in range(ff // FF_CHUNK):
        cs = slice(c * FF_CHUNK, (c + 1) * FF_CHUNK)
        z = _silu(_dot(h, w1_ref[:, cs])) * _dot(h, w3_ref[:, cs])
        acc = acc + _dot(z.astype(BF16), w2_ref[cs, :])
    o_ref[...] = _post_norm(x, mod, acc, g_ref[...], b_ref[...])


def _ffn(x2d, mod3, w1, w3, w2, g, b, tm, tps):
    m = x2d.shape[0]
    row = pl.BlockSpec((tm, D_MODEL), lambda i: (i, 0))
    vec = _const_spec((1, D_MODEL))
    return pl.pallas_call(
        functools.partial(_ffn_kernel, ff=w1.shape[1]), grid=(m // tm,),
        in_specs=[row, _mod_spec(mod3, tm, tps), _const_spec(w1.shape, True), _const_spec(w3.shape, True),
                  _const_spec(w2.shape, True), vec, vec],
        out_specs=row,
        out_shape=jax.ShapeDtypeStruct((m, D_MODEL), F32),
        compiler_params=_cparams(1), name="dense_ffn")(x2d, mod3, w1, w3, w2, g, b)


def _cf_in_kernel(x_ref, mod_ref, w_ref, b_ref, o_ref):
    h = _modulate(x_ref[...], mod_ref[0]).astype(BF16)
    u = _dot(h, w_ref[...]) + b_ref[...]
    o_ref[...] = u[:, :D_MODEL] * jax.nn.sigmoid(u[:, D_MODEL:])


def _cf_in(x2d, mod3, w, bias, tm, tps):
    m = x2d.shape[0]
    row = pl.BlockSpec((tm, D_MODEL), lambda i: (i, 0))
    return pl.pallas_call(
        _cf_in_kernel, grid=(m // tm,),
        in_specs=[row, _mod_spec(mod3, tm, tps), _const_spec(w.shape), _const_spec(bias.shape)],
        out_specs=row,
        out_shape=jax.ShapeDtypeStruct((m, D_MODEL), F32),
        compiler_params=_cparams(1), name="conformer_in")(x2d, mod3, w, bias)


def _cf_conv_kernel(x_ref, halo_ref, dw_ref, db_ref, g_ref, b_ref, z_ref, s_ref, *, tt, zero_first):
    halo = halo_ref[...]
    if zero_first:
        halo = jnp.where(pl.program_id(1) == 0, 0.0, halo)
    s_ref[0, 0:CF_HALO, :] = halo
    s_ref[0, CF_HALO:CF_HALO + tt, :] = x_ref[...]
    n_sh = CF_HALO + tt - SUBLANES
    for k in range(1, SUBLANES):
        s_ref[k, 0:n_sh, :] = s_ref[0, k:k + n_sh, :]
    sub = min(CONV_SUB, tt)
    first = CF_HALO - (CF_KERNEL - 1)
    for c in range(tt // sub):
        r0 = c * sub
        y = jnp.broadcast_to(db_ref[...], (sub, D_MODEL))
        for j in range(CF_KERNEL):
            k = (first + j) % SUBLANES
            a0 = r0 + first + j - k
            y = y + dw_ref[j:j + 1, :] * s_ref[k, a0:a0 + sub, :]
        z_ref[r0:r0 + sub, :] = _silu(_layer_norm(y, g_ref[...], b_ref[...]))


def _cf_conv(glu, halo_src, halo_map, dw, db, g, b, bsz, t, tt, zero_first):
    nt = t // tt
    vec = _const_spec((1, D_MODEL))
    return pl.pallas_call(
        functools.partial(_cf_conv_kernel, tt=tt, zero_first=zero_first), grid=(bsz, nt),
        in_specs=[pl.BlockSpec((tt, D_MODEL), lambda bi, ti: (bi * nt + ti, 0)),
                  pl.BlockSpec((CF_HALO, D_MODEL), halo_map),
                  _const_spec(dw.shape), vec, vec, vec],
        out_specs=pl.BlockSpec((tt, D_MODEL), lambda bi, ti: (bi * nt + ti, 0)),
        out_shape=jax.ShapeDtypeStruct((bsz * t, D_MODEL), F32),
        scratch_shapes=[pltpu.VMEM((SUBLANES, CF_HALO + tt, D_MODEL), F32)],
        compiler_params=_cparams(2), name="conformer_conv")(glu, halo_src, dw, db, g, b)


def _router_kernel(x_ref, mod_ref, rw_ref, h_ref, ids_ref, gates_ref, *, n_real):
    h = _modulate(x_ref[...], mod_ref[0])
    h_ref[...] = jnp.where(pl.program_id(0) < n_real, h, 0.0)
    logits = lax.dot_general(h, rw_ref[...], (((1,), (0,)), ((), ())), precision=lax.Precision.HIGHEST,
                             preferred_element_type=F32)
    lane = lax.broadcasted_iota(I32, logits.shape, 1)
    logits = jnp.where(lane < N_EXPERTS, logits, -jnp.inf)
    m1 = jnp.max(logits, axis=-1, keepdims=True)
    i1 = jnp.min(jnp.where(logits == m1, lane, LANES), axis=-1, keepdims=True)
    rest = jnp.where(lane == i1, -jnp.inf, logits)
    m2 = jnp.max(rest, axis=-1, keepdims=True)
    i2 = jnp.min(jnp.where(rest == m2, lane, LANES), axis=-1, keepdims=True)
    e2 = jnp.exp(m2 - m1)
    inv = 1.0 / (1.0 + e2)
    col = lax.broadcasted_iota(I32, ids_ref.shape, 1)
    ids_ref[...] = jnp.where(col == 0, i1, jnp.where(col == 1, i2, 0))
    gates_ref[...] = jnp.where(col == 0, inv, jnp.where(col == 1, e2 * inv, 0.0))


def _router(x2d, mod3, rw, tm, tps, n_total, row_off, h_prev):
    m = x2d.shape[0]
    n_real = m // tm
    off = row_off // tm
    if h_prev is None:
        steps = -(-n_total // tm)
        clamp = lambda i: jnp.minimum(i, n_real - 1)
    else:
        steps = n_real
        clamp = lambda i: i
    if mod3.shape[1] == 1:
        mspec = pl.BlockSpec((1, 1, mod3.shape[2]), lambda i: (clamp(i) // tps, 0, 0))
    else:
        mspec = pl.BlockSpec((1, tm, mod3.shape[2]), lambda i: (0, clamp(i), 0))
    small = lambda: pl.BlockSpec((tm, SUBLANES), lambda i: (clamp(i), 0))
    in_specs = [pl.BlockSpec((tm, D_MODEL), lambda i: (clamp(i), 0)), mspec, _const_spec(rw.shape)]
    args = [x2d, mod3, rw]
    kern = functools.partial(_router_kernel, n_real=n_real)
    aliases = {}
    if h_prev is not None:
        in_specs.append(pl.BlockSpec(memory_space=pl.ANY))
        args.append(h_prev)
        aliases = {3: 0}
        kern = lambda x, md, rw_, hp, h, ids, gt: _router_kernel(x, md, rw_, h, ids, gt, n_real=n_real)
    return pl.pallas_call(
        kern, grid=(steps,), in_specs=in_specs,
        out_specs=[pl.BlockSpec((tm, D_MODEL), lambda i: (i + off, 0)), small(), small()],
        out_shape=[jax.ShapeDtypeStruct((n_total, D_MODEL), F32), jax.ShapeDtypeStruct((m, SUBLANES), I32),
                   jax.ShapeDtypeStruct((m, SUBLANES), F32)],
        input_output_aliases=aliases,
        compiler_params=_cparams(1), name="moe_router")(*args)


ROW_SRC_BITS = 15


DMA_GROUP = 8


def _for_rows(n, fn):
    full = n // DMA_GROUP

    def group(gi, c):
        for u in range(DMA_GROUP):
            fn(gi * DMA_GROUP + u)
        return c
    lax.fori_loop(0, full, group, 0)
    if isinstance(n, int):
        for r in range(full * DMA_GROUP, n):
            fn(r)
    else:
        for u in range(DMA_GROUP - 1):
            r = full * DMA_GROUP + u

            @pl.when(r < n)
            def _():
                fn(r)


def _experts_kernel(blk_e_ref, plan_ref, cnt_ref, nused_ref, h_ref, w1_ref, w3_ref, w2_ref, y_ref,
                    xs, xb, acc, ybuf, gsem, ssem):
    i = pl.program_id(0)
    j = pl.program_id(1)
    nused = nused_ref[0]
    used = i < nused
    last = j == pl.num_programs(1) - 1
    src_mask = (1 << ROW_SRC_BITS) - 1

    def start_gather(blk):
        slot = blk % 2

        def one(r):
            tok = plan_ref[blk * MOE_TILE + r] & src_mask
            pltpu.make_async_copy(h_ref.at[pl.ds(tok, 1), :], xs.at[slot, pl.ds(r, 1), :], gsem.at[slot]).start()
        _for_rows(MOE_TILE, one)

    def wait_gather(blk):
        slot = blk % 2
        pltpu.make_async_copy(h_ref.at[pl.ds(0, MOE_TILE), :], xs.at[slot], gsem.at[slot]).wait()

    def scatter_copy(r, dst, rows):
        return pltpu.make_async_copy(ybuf.at[pl.ds(r, rows), :], y_ref.at[pl.ds(dst, rows), :], ssem)

    def start_scatter(blk):
        def one(r):
            scatter_copy(r, plan_ref[blk * MOE_TILE + r] >> ROW_SRC_BITS, 1).start()
        _for_rows(cnt_ref[blk], one)

    def wait_scatter(blk):
        n = cnt_ref[blk]
        full = n // DMA_GROUP

        def group(gi, c):
            scatter_copy(0, 0, DMA_GROUP).wait()
            return c
        lax.fori_loop(0, full, group, 0)

        def one(r, c):
            scatter_copy(0, 0, 1).wait()
            return c
        lax.fori_loop(full * DMA_GROUP, n, one, 0)

    @pl.when(used & (j == 0))
    def _():
        @pl.when(i == 0)
        def _():
            start_gather(i)

        @pl.when(i + 1 < nused)
        def _():
            start_gather(i + 1)
        wait_gather(i)
        xb[...] = xs[i % 2].astype(BF16)
        acc[...] = jnp.zeros(acc.shape, F32)

    @pl.when(used)
    def _():
        x = xb[...]
        z = _silu(_dot(x, w1_ref[...])) * _dot(x, w3_ref[...])
        acc[...] += _dot(z.astype(BF16), w2_ref[...])

    @pl.when(used & last)
    def _():
        @pl.when(i >= 1)
        def _():
            wait_scatter(i - 1)
        ybuf[...] = acc[...]
        start_scatter(i)

        @pl.when(i == nused - 1)
        def _():
            wait_scatter(i)


def _experts(blk_e, plan, cnt, nused, h_all, n_tok, w1, w3, w2, layer, n_blk):
    ff = w1.shape[3]
    nj = ff // MOE_FF_CHUNK

    def jj(i, j, nu):
        return jnp.where(i < nu[0], j, nj - 1)

    wspec = lambda shape, imap: pl.BlockSpec(shape, imap)
    up = (None, None, D_MODEL, MOE_FF_CHUNK)
    grid_spec = pltpu.PrefetchScalarGridSpec(
        num_scalar_prefetch=4, grid=(n_blk, nj),
        in_specs=[pl.BlockSpec(memory_space=pl.ANY),
                  wspec(up, lambda i, j, be, pn, ct, nu: (layer, be[i], 0, jj(i, j, nu))),
                  wspec(up, lambda i, j, be, pn, ct, nu: (layer, be[i], 0, jj(i, j, nu))),
                  wspec((None, None, MOE_FF_CHUNK, D_MODEL),
                        lambda i, j, be, pn, ct, nu: (layer, be[i], jj(i, j, nu), 0))],
        out_specs=pl.BlockSpec(memory_space=pl.ANY),
        scratch_shapes=[pltpu.VMEM((2, MOE_TILE, D_MODEL), F32), pltpu.VMEM((MOE_TILE, D_MODEL), BF16),
                        pltpu.VMEM((MOE_TILE, D_MODEL), F32), pltpu.VMEM((MOE_TILE, D_MODEL), F32),
                        pltpu.SemaphoreType.DMA((2,)), pltpu.SemaphoreType.DMA(())])
    return pl.pallas_call(
        _experts_kernel, grid_spec=grid_spec,
        out_shape=jax.ShapeDtypeStruct((2 * n_tok, D_MODEL), F32),
        compiler_params=_cparams(2, disable_bounds_checks=True),
        name="moe_experts")(blk_e, plan, cnt, nused, h_all, w1, w3, w2)


def _combine_kernel(y0_ref, y1_ref, gates_ref, x_ref, mod_ref, g_ref, b_ref, o_ref):
    gates = gates_ref[...]
    out = gates[:, 0:1] * y0_ref[...] + gates[:, 1:2] * y1_ref[...]
    o_ref[...] = _post_norm(x_ref[...], mod_ref[0], out, g_ref[...], b_ref[...])


def _combine(ys, gates, x2d, mod3, g, b, tm, tps, tok_off, n_tok):
    m = x2d.shape[0]
    off0 = tok_off // tm
    off1 = (n_tok + tok_off) // tm
    row = lambda width: pl.BlockSpec((tm, width), lambda i: (i, 0))
    vec = _const_spec((1, D_MODEL))
    return pl.pallas_call(
        _combine_kernel, grid=(m // tm,),
        in_specs=[pl.BlockSpec((tm, D_MODEL), lambda i: (i + off0, 0)),
                  pl.BlockSpec((tm, D_MODEL), lambda i: (i + off1, 0)),
                  row(SUBLANES), row(D_MODEL), _mod_spec(mod3, tm, tps), vec, vec],
        out_specs=row(D_MODEL),
        out_shape=jax.ShapeDtypeStruct((m, D_MODEL), F32),
        compiler_params=_cparams(1), name="moe_combine")(ys, ys, gates, x2d, mod3, g, b)


def _route_plan(ids_all, n_tok):
    fe = ids_all[:, :2].reshape(-1)
    onehot = (fe[:, None] == jnp.arange(N_EXPERTS, dtype=I32)[None, :]).astype(I32)
    csum = jnp.cumsum(onehot, axis=0)
    rank = jnp.take_along_axis(csum, fe[:, None], axis=1)[:, 0] - 1
    counts = csum[-1]
    padded = (counts + MOE_TILE - 1) // MOE_TILE * MOE_TILE
    pend = jnp.cumsum(padded)
    pstart = pend - padded
    dest = (pstart[fe] + rank).astype(I32)
    n_blk = -(-(2 * n_tok) // MOE_TILE) + N_EXPERTS
    a = jnp.arange(2 * n_tok, dtype=I32)
    word = (a // 2) | (((a % 2) * n_tok + a // 2) << ROW_SRC_BITS)
    plan = jnp.zeros((n_blk * MOE_TILE,), I32).at[dest].set(word)
    nused = (pend[-1] // MOE_TILE).astype(I32)
    blk = jnp.arange(n_blk, dtype=I32)
    blk_e = jnp.searchsorted(pend, jnp.minimum(blk, nused - 1) * MOE_TILE, side='right').astype(I32)
    blk_e = jnp.minimum(blk_e, N_EXPERTS - 1)
    cnt = jnp.clip(pstart[blk_e] + counts[blk_e] - blk * MOE_TILE, 0, MOE_TILE)
    cnt = jnp.where(blk < nused, cnt, 0).astype(I32)
    return plan, blk_e, cnt, nused.reshape(1), n_blk


def _rope_tables(pos):
    half = HEAD_DIM // 2
    inv = 1.0 / (ROPE_THETA ** (jnp.arange(half, dtype=F32) * (2.0 / HEAD_DIM)))
    ang = pos.astype(F32)[:, None] * inv[None, :]
    c = jnp.cos(ang)
    s = jnp.sin(ang)
    return jnp.concatenate([c, c, c, c], axis=1), jnp.concatenate([-s, s, -s, s], axis=1)


def _cmp_rope_tables(nc):
    blk = jnp.concatenate([jnp.arange(0, nc, 2), jnp.arange(1, nc, 2)])
    return _rope_tables(blk * CMP_BLOCK + (CMP_BLOCK - 1))


def _w_in_layout(w_in):
    o1 = Q_COLS
    o2 = o1 + KV_COLS
    o3 = o2 + GATE_COLS
    pad = jnp.zeros((D_MODEL, LANES - GATE_COLS), w_in.dtype)
    return jnp.concatenate([w_in[:, :o2], w_in[:, o3:], w_in[:, o2:o3], pad], axis=1).astype(BF16)


def _block_diag(blocks):
    n, a, b = blocks.shape
    eye = jnp.eye(n, dtype=blocks.dtype)
    return (eye[:, None, :, None] * blocks[:, :, None, :]).reshape(n * a, n * b)


def _cmp_weights(w_ck, w_cv, pe_k, pe_v):
    wk = w_ck.reshape(CMP_BLOCK, HEAD_DIM, HEAD_DIM)
    wv = w_cv.reshape(CMP_BLOCK, HEAD_DIM, HEAD_DIM)
    w4 = jax.vmap(lambda a, b: _block_diag(jnp.stack([a, a, b, b])))(wk, wv).astype(BF16)
    pe4 = jnp.concatenate([pe_k, pe_k, pe_v, pe_v], axis=1)
    return w4, pe4


def kernel(x_prompt, x_sample, c_prompt, c_sample, page_table, cache_nsa_kv, state_nsa_win, state_rglru_h, state_rglru_conv, state_conformer_conv, w_mod, b_mod, ln_g, ln_b, w_in_even, w_out_even, w_cmp_k, w_cmp_v, pe_cmp_k, pe_cmp_v, rg_conv_w, rg_conv_b, rg_wa, rg_ba, rg_wx, rg_bx, rg_lam, cf_w1, cf_b1, cf_dw, cf_db, cf_ln_g, cf_ln_b, cf_w2, cf_b2, ff_w1, ff_w3, ff_w2, moe_router, moe_w1, moe_w3, moe_w2):
    bp, tp, d = x_prompt.shape
    bs, ts, _ = x_sample.shape
    npages = page_table.shape[1]
    past = npages * PAGE_SIZE
    n_p = bp * tp
    n_s = bs * ts
    n_tok = n_p + n_s
    assert d == D_MODEL and tp % ROW_TILE == 0 and tp % Q_TILE == 0 and tp >= WINDOW
    assert past % SEL_BLOCK == 0 and ts <= CMP_BLOCK and ts % SUBLANES == 0 and n_s % SUBLANES == 0
    assert n_p % MOE_TILE == 0 and n_p % n_s == 0
    tps_p = tp // ROW_TILE
    xp = x_prompt.reshape(n_p, d)
    xs = x_sample.reshape(n_s, d)
    vec = lambda v: v.reshape(1, -1)

    mod_all = _mod_all(jnp.concatenate([c_prompt, c_sample], axis=0), w_mod, b_mod)

    def mods(l, s):
        mrow = mod_all[2 * l + s]
        return mrow[:bp].reshape(bp, 1, 3 * d), jnp.repeat(mrow[bp:], ts, axis=0).reshape(1, n_s, 3 * d)

    cos_p, sin_p = _rope_tables(jnp.arange(tp))
    cos_s, sin_s = _rope_tables(jnp.tile(past + jnp.arange(ts), bs))
    ccos_p, csin_p = _cmp_rope_tables(tp // CMP_BLOCK)
    ccos_s, csin_s = _cmp_rope_tables(past // CMP_BLOCK)
    pt_flat = page_table.reshape(-1).astype(I32)
    n_even = cache_nsa_kv.shape[0]
    win_len = state_nsa_win.shape[2]
    cache_fm = cache_nsa_kv.transpose(0, 1, 3, 4, 5, 2).reshape(n_even, cache_nsa_kv.shape[1], 4 * LANES, PAGE_SIZE)
    win_fm = state_nsa_win.transpose(0, 1, 3, 4, 5, 2).reshape(n_even, bs, 2 * LANES, win_len)
    tm_c = min(Q_TILE, n_s)
    assert n_p % tm_c == 0 and n_tok % tm_c == 0
    assert n_tok < (1 << ROW_SRC_BITS) and 2 * n_tok < (1 << (31 - ROW_SRC_BITS))

    moe_w1_bf, moe_w3_bf, moe_w2_bf = moe_w1.astype(BF16), moe_w3.astype(BF16), moe_w2.astype(BF16)

    kv_p, kv_s, win_p, win_s, rh_p, rh_s, rc_p, rc_s, cc_p, cc_s = ([] for _ in range(10))
    for l in range(DEPTH):
        i = l // 2
        mp0, ms0 = mods(l, 0)
        mp1, ms1 = mods(l, 1)
        g0, b0, g1, b1 = vec(ln_g[l, 0]), vec(ln_b[l, 0]), vec(ln_g[l, 1]), vec(ln_b[l, 1])
        zero_bias = jnp.zeros((1, d), F32)
        if l % 2 == 0:
            w_in = _w_in_layout(w_in_even[i])
            w4, pe4 = _cmp_weights(w_cmp_k[i], w_cmp_v[i], pe_cmp_k[i], pe_cmp_v[i])
            wo_a = w_out_even[i][:NSA_WIDTH].astype(BF16)
            wo_r = w_out_even[i][NSA_WIDTH:].astype(BF16)
            wa = _block_diag(rg_wa[i]).astype(BF16)
            wx = _block_diag(rg_wx[i]).astype(BF16)
            rg_args = (rg_conv_w[i], vec(rg_conv_b[i]), wa, vec(rg_ba[i]), wx, vec(rg_bx[i]), vec(rg_lam[i]))

            q, kv4, wr, rx, ry, gl, kvt, wrt = _win_call(xp, mp0, w_in, cos_p, sin_p, ROW_TILE, tps_p, tp)
            cmp = _compress_prompt(kv4, bp, tp, w4, pe4, ccos_p, csin_p)
            o = _pattn(q, gl, cmp, kv4, kvt, wr, wrt, bp, tp)
            st = jnp.zeros((bp, RG_STATE_ROWS, RG_WIDTH), F32)
            rg, h_last, new_buf = _rglru(rx, ry, st, *rg_args, bp, tp)
            xp = _proj_ln([o, rg], [wo_a, wo_r], xp, mp0, zero_bias, g0, b0, ROW_TILE, tps_p)
            kv_p.append(kvt.reshape(bp, 4, N_KV_A, HEAD_DIM, tp).transpose(0, 4, 1, 2, 3))
            win_p.append(wrt.reshape(bp, 2, N_KV_A, HEAD_DIM, tp)[..., tp - min(WINDOW, tp):]
                         .transpose(0, 4, 1, 2, 3))
            rh_p.append(h_last.reshape(bp, RG_WIDTH))
            rc_p.append(new_buf)

            q, kv4, wr, rx, ry, gl = _win_call(xs, ms0, w_in, cos_s, sin_s, n_s, 1)
            cmp = _compress_sample(pt_flat, cache_fm, i, bs, npages, w4, pe4, ccos_s, csin_s)
            o = _sattn(pt_flat, cache_fm, i, q, gl, cmp, kv4, wr, win_fm, bs, ts, npages)
            st = jnp.concatenate([state_rglru_h[i][:, None, :],
                                  jnp.zeros((bs, RG_STATE_ROWS - RG_CONV, RG_WIDTH), F32),
                                  state_rglru_conv[i]], axis=1)
            rg, h_last, new_buf = _rglru(rx, ry, st, *rg_args, bs, ts)
            xs = _proj_ln([o, rg], [wo_a, wo_r], xs, ms0, zero_bias, g0, b0, n_s, 1)
            kv_s.append(kv4.reshape(bs, ts, 4, N_KV_A, HEAD_DIM))
            wfull = jnp.concatenate([state_nsa_win[i], wr.reshape(bs, ts, 2, N_KV_A, HEAD_DIM)], axis=1)
            win_s.append(wfull[:, wfull.shape[1] - min(WINDOW, wfull.shape[1]):])
            rh_s.append(h_last.reshape(bs, RG_WIDTH))
            rc_s.append(new_buf)

            w1, w3, w2 = ff_w1[i].astype(BF16), ff_w3[i].astype(BF16), ff_w2[i].astype(BF16)
            xp = _ffn(xp, mp1, w1, w3, w2, g1, b1, ROW_TILE, tps_p)
            xs = _ffn(xs, ms1, w1, w3, w2, g1, b1, n_s, 1)
        else:
            cw1 = cf_w1[i].astype(BF16)
            cw2 = cf_w2[i].astype(BF16)
            dw = jnp.concatenate([cf_dw[i], jnp.zeros((CF_HALO - CF_KERNEL, d), F32)], axis=0)
            conv_args = (dw, vec(cf_db[i]), vec(cf_ln_g[i]), vec(cf_ln_b[i]))

            glu = _cf_in(xp, mp0, cw1, vec(cf_b1[i]), ROW_TILE, tps_p)
            per = CONV_TILE // CF_HALO
            z = _cf_conv(glu, glu, lambda bi, ti: (jnp.maximum((bi * (tp // CONV_TILE) + ti) * per - 1, 0), 0),
                         *conv_args, bp, tp, CONV_TILE, True)
            xp = _proj_ln([z], [cw2], xp, mp0, vec(cf_b2[i]), g0, b0, ROW_TILE, tps_p)
            cc_p.append(glu.reshape(bp, tp, d)[:, tp - (CF_KERNEL - 1):])

            glu = _cf_in(xs, ms0, cw1, vec(cf_b1[i]), n_s, 1)
            halo = jnp.concatenate([jnp.zeros((bs, CF_HALO - (CF_KERNEL - 1), d), F32),
                                    state_conformer_conv[i]], axis=1).reshape(bs * CF_HALO, d)
            z = _cf_conv(glu, halo, lambda bi, ti: (bi, 0), *conv_args, bs, ts, ts, False)
            xs = _proj_ln([z], [cw2], xs, ms0, vec(cf_b2[i]), g0, b0, n_s, 1)
            cc_s.append(jnp.concatenate([state_conformer_conv[i], glu.reshape(bs, ts, d)],
                                        axis=1)[:, -(CF_KERNEL - 1):])

            rw = jnp.concatenate([moe_router[i], jnp.zeros((d, LANES - N_EXPERTS), F32)], axis=1)
            h_all, ids_p, gates_p = _router(xp, mp1, rw, ROW_TILE, tps_p, n_tok, 0, None)
            h_all, ids_s, gates_s = _router(xs, ms1, rw, n_s, 1, n_tok, n_p, h_all)
            plan, blk_e, cnt, nused, n_blk = _route_plan(jnp.concatenate([ids_p, ids_s], axis=0), n_tok)
            ys = _experts(blk_e, plan, cnt, nused, h_all, n_tok, moe_w1_bf, moe_w3_bf, moe_w2_bf, i, n_blk)
            xp = _combine(ys, gates_p, xp, mp1, g1, b1, tm_c, tp // tm_c, 0, n_tok)
            xs = _combine(ys, gates_s, xs, ms1, g1, b1, tm_c, 1, n_p, n_tok)
    return (xp.reshape(bp, tp, d), xs.reshape(bs, ts, d), jnp.stack(kv_p), jnp.stack(kv_s),
            jnp.stack(win_p), jnp.stack(win_s), jnp.stack(rh_p), jnp.stack(rh_s),
            jnp.stack(rc_p), jnp.stack(rc_s), jnp.stack(cc_p), jnp.stack(cc_s))
```

```python
import functools

import jax
import jax.numpy as jnp
from jax import lax
from jax.experimental import pallas as pl
from jax.experimental.pallas import tpu as pltpu

F32 = jnp.float32
BF16 = jnp.bfloat16
I32 = jnp.int32

D_MODEL = 1024
HEAD_DIM = 64
N_HEADS_A = 8
N_KV_A = 2
HEADS_PER_GROUP = N_HEADS_A // N_KV_A
GROUP_COLS = N_KV_A * HEAD_DIM
CMP_BLOCK = 32
SEL_BLOCK = 64
TOP_N = 16
WINDOW = 512
ROPE_THETA = 10000.0
PAGE_SIZE = 128
RG_WIDTH = 512
RG_BLOCKS = 8
RG_CONV = 4
RG_C = 8.0
NSA_WIDTH = 512
Q_COLS = NSA_WIDTH
KV_COLS = 6 * GROUP_COLS
GATE_COLS = 3 * N_HEADS_A
CF_KERNEL = 31
CF_HALO = 32
N_EXPERTS = 8
DEPTH = 4
ALPHA = (2.0 * DEPTH) ** 0.25
LN_EPS = 1e-5
NEG = -1e30
FORCE_SCORE = 1e4
ATTN_SCALE = HEAD_DIM ** -0.5

LANES = 128
SUBLANES = 8
VMEM_LIMIT = 56 * 1024 * 1024

ROW_TILE = 512
Q_TILE = 256
CONV_TILE = 256
CONV_SUB = 32
MOE_TILE = 512
FF_CHUNK = 256
MOE_FF_CHUNK = 896
MOE_STEPS = 4


def _cparams(n_axes, **kw):
    return pltpu.CompilerParams(dimension_semantics=("arbitrary",) * n_axes,
                                vmem_limit_bytes=VMEM_LIMIT, **kw)


def _const_spec(shape, single_buffer=False):
    n = len(shape)
    if single_buffer:
        return pl.BlockSpec(shape, lambda *a: (0,) * n, pipeline_mode=pl.Buffered(1))
    return pl.BlockSpec(shape, lambda *a: (0,) * n)


def _mod_spec(mod3, tm, tps):
    if mod3.shape[1] == 1:
        return pl.BlockSpec((1, 1, mod3.shape[2]), lambda i: (i // tps, 0, 0))
    return pl.BlockSpec((1, tm, mod3.shape[2]), lambda i: (0, i, 0))


def _dot(a, b):
    return jnp.dot(a, b, preferred_element_type=F32)


def _dot_nt(a, b):
    return lax.dot_general(a, b, (((1,), (1,)), ((), ())), preferred_element_type=F32)


def _layer_norm(y, g, b):
    mu = jnp.mean(y, axis=-1, keepdims=True)
    yc = y - mu
    var = jnp.mean(yc * yc, axis=-1, keepdims=True)
    return yc * lax.rsqrt(var + LN_EPS) * g + b


def _post_norm(x, mod, out, g, b):
    gate = mod[:, 2 * D_MODEL:]
    return _layer_norm(ALPHA * x + (1.0 + gate) * out, g, b)


def _modulate(x, mod):
    return x * (1.0 + mod[:, D_MODEL:2 * D_MODEL]) + mod[:, :D_MODEL]


def _silu(x):
    return x * jax.nn.sigmoid(x)


def _rope128(v, cos, sin_signed):
    lane = lax.broadcasted_iota(I32, v.shape, 1)
    from_hi = pltpu.roll(v, LANES - HEAD_DIM // 2, 1)
    from_lo = pltpu.roll(v, HEAD_DIM // 2, 1)
    swapped = jnp.where((lane & (HEAD_DIM - 1)) < HEAD_DIM // 2, from_hi, from_lo)
    return v * cos + swapped * sin_signed


def _rope(v, cos, sin_signed):
    k = v.shape[1] // LANES
    parts = [_rope128(v[:, i * LANES:(i + 1) * LANES], cos, sin_signed) for i in range(k)]
    return parts[0] if k == 1 else jnp.concatenate(parts, axis=1)


def _softmax_parts(parts):
    masked = [jnp.where(m, s, NEG) for s, m in parts]
    mx = functools.reduce(jnp.maximum, [jnp.max(s, axis=-1, keepdims=True) for s in masked])
    es = [jnp.where(m, jnp.exp(s - mx), 0.0) for s, (_, m) in zip(masked, parts)]
    den = functools.reduce(jnp.add, [jnp.sum(e, axis=-1, keepdims=True) for e in es])
    return es, 1.0 / jnp.maximum(den, 1e-30)


def _attend(qb, pieces):
    es, inv = _softmax_parts([(_dot(qb, k) if fm else _dot_nt(qb, k), m) for k, _, m, fm in pieces])
    o = functools.reduce(jnp.add, [_dot_nt(e.astype(BF16), v) if fm else _dot(e.astype(BF16), v)
                                   for e, (_, v, _, fm) in zip(es, pieces)])
    return o * inv


def _topk_mask(vals, n_valid, kk):
    lane = lax.broadcasted_iota(I32, vals.shape, 1)
    rank = jnp.zeros(vals.shape, I32)
    for m in range(n_valid):
        col = vals[:, m:m + 1]
        later = jnp.where(lane > m, 1, 0)
        rank = rank + jnp.where(col > vals, 1, jnp.where(col == vals, later, 0))
    return (rank < kk) & (lane < n_valid)


def _mod_kernel(c_ref, w_ref, b_ref, o_ref):
    s = _silu(c_ref[...]).astype(BF16)
    o_ref[0] = _dot(s, w_ref[0].astype(BF16)) + b_ref[0]


def _mod_all(c_all, w_mod, b_mod):
    n = c_all.shape[0]
    nl = w_mod.shape[0] * w_mod.shape[1]
    w = w_mod.reshape(nl, D_MODEL, 3 * D_MODEL)
    b = b_mod.reshape(nl, 1, 3 * D_MODEL)
    return pl.pallas_call(
        _mod_kernel, grid=(nl, 3),
        in_specs=[pl.BlockSpec((n, D_MODEL), lambda l, j: (0, 0)),
                  pl.BlockSpec((1, D_MODEL, D_MODEL), lambda l, j: (l, 0, j)),
                  pl.BlockSpec((1, 1, D_MODEL), lambda l, j: (l, 0, j))],
        out_specs=pl.BlockSpec((1, n, D_MODEL), lambda l, j: (l, 0, j)),
        out_shape=jax.ShapeDtypeStruct((nl, n, 3 * D_MODEL), F32),
        compiler_params=_cparams(2), name="mod_all")(c_all, w, b)


_O_KV = Q_COLS
_O_RX = _O_KV + KV_COLS
_O_RY = _O_RX + RG_WIDTH
_O_GL = _O_RY + RG_WIDTH
W_IN_COLS = _O_GL + LANES


def _win_kernel(x_ref, mod_ref, w_ref, cos_ref, sin_ref, q_ref, kv_ref, wr_ref, rx_ref, ry_ref, gl_ref,
                *fm_refs):
    h = _modulate(x_ref[...], mod_ref[0]).astype(BF16)
    u = _dot(h, w_ref[...])
    cos = cos_ref[...]
    sin = sin_ref[...]
    q_ref[...] = _rope(u[:, :Q_COLS], cos, sin)
    c = _O_KV
    kv = jnp.concatenate([u[:, c:c + 2 * LANES],
                          _rope128(u[:, c + 2 * LANES:c + 3 * LANES], cos, sin),
                          u[:, c + 3 * LANES:c + 4 * LANES]], axis=1)
    wr = jnp.concatenate([_rope128(u[:, c + 4 * LANES:c + 5 * LANES], cos, sin),
                          u[:, c + 5 * LANES:c + 6 * LANES]], axis=1)
    kv_ref[...] = kv
    wr_ref[...] = wr
    rx_ref[...] = u[:, _O_RX:_O_RY]
    ry_ref[...] = u[:, _O_RY:_O_GL]
    gl_ref[...] = u[:, _O_GL:]
    if fm_refs:
        kvt_ref, wrt_ref = fm_refs
        kvt_ref[...] = kv.T
        wrt_ref[...] = wr.T


def _win_call(x2d, mod3, w, cos, sin, tm, tps, t_len=None):
    m = x2d.shape[0]
    row = lambda width: pl.BlockSpec((tm, width), lambda i: (i, 0))
    widths = (Q_COLS, 4 * LANES, 2 * LANES, RG_WIDTH, RG_WIDTH, LANES)
    out_specs = [row(wd) for wd in widths]
    out_shape = [jax.ShapeDtypeStruct((m, wd), F32) for wd in widths]
    if t_len is not None:
        for feat in (4 * LANES, 2 * LANES):
            out_specs.append(pl.BlockSpec((feat, tm), lambda i: (i // tps, i % tps)))
            out_shape.append(jax.ShapeDtypeStruct((m // t_len * feat, t_len), F32))
    return pl.pallas_call(
        _win_kernel, grid=(m // tm,),
        in_specs=[row(D_MODEL), _mod_spec(mod3, tm, tps), _const_spec(w.shape),
                  pl.BlockSpec((tm, LANES), lambda i: (i % tps, 0)),
                  pl.BlockSpec((tm, LANES), lambda i: (i % tps, 0))],
        out_specs=out_specs, out_shape=out_shape,
        compiler_params=_cparams(1), name="mixer_in")(x2d, mod3, w, cos, sin)


def _compress_core(k_ref, v_ref, w4_ref, pe4_ref, cos_ref, sin_ref, out_ref, nb):
    half = nb // 2
    acc = jnp.zeros((nb, 2 * LANES), F32)
    for l in range(CMP_BLOCK):
        even = pl.ds(l, half, stride=2 * CMP_BLOCK)
        odd = pl.ds(CMP_BLOCK + l, half, stride=2 * CMP_BLOCK)
        xk = jnp.concatenate([k_ref[even, :], k_ref[odd, :]], axis=0)
        xv = jnp.concatenate([v_ref[even, :], v_ref[odd, :]], axis=0)
        x = (jnp.concatenate([xk, xv], axis=1) + pe4_ref[pl.ds(l, 1), :]).astype(BF16)
        acc = acc + _dot(x, w4_ref[l])
    out_ref[:, 0:LANES] = _rope128(acc[:, 0:LANES], cos_ref[...], sin_ref[...])
    out_ref[:, LANES:2 * LANES] = acc[:, LANES:2 * LANES]


def _compress_prompt_kernel(k_ref, v_ref, w4_ref, pe4_ref, cos_ref, sin_ref, out_ref, *, nb):
    _compress_core(k_ref, v_ref, w4_ref, pe4_ref, cos_ref, sin_ref, out_ref, nb)


def _compress_prompt(kv4, bsz, t, w4, pe4, ccos, csin):
    nb = t // CMP_BLOCK
    return pl.pallas_call(
        functools.partial(_compress_prompt_kernel, nb=nb), grid=(bsz,),
        in_specs=[pl.BlockSpec((t, LANES), lambda b: (b, 0)), pl.BlockSpec((t, LANES), lambda b: (b, 1)),
                  _const_spec(w4.shape), _const_spec(pe4.shape),
                  _const_spec(ccos.shape), _const_spec(csin.shape)],
        out_specs=pl.BlockSpec((nb, 2 * LANES), lambda b: (b, 0)),
        out_shape=jax.ShapeDtypeStruct((bsz * nb, 2 * LANES), F32),
        compiler_params=_cparams(1), name="compress_prompt")(kv4, kv4, w4, pe4, ccos, csin)


def _prefetch_pages(pt_ref, cache_ref, sems, layer, feat0, npages, dst_fn):
    b = pl.program_id(0)

    def copy(seq, j):
        slot = seq % 2
        src = cache_ref.at[layer, pt_ref[seq * npages + j], pl.ds(feat0, 2 * LANES), :]
        return pltpu.make_async_copy(src, dst_fn(slot, j), sems.at[slot])

    def start_seq(seq):
        def body(j, c):
            copy(seq, j).start()
            return c
        lax.fori_loop(0, npages, body, 0)

    @pl.when(b == 0)
    def _():
        start_seq(b)

    @pl.when(b + 1 < pl.num_programs(0))
    def _():
        start_seq(b + 1)

    def wait(j, c):
        copy(b, j).wait()
        return c
    lax.fori_loop(0, npages, wait, 0)
    return b % 2


def _compress_sample_kernel(pt_ref, cache_ref, w4_ref, pe4_ref, cos_ref, sin_ref, out_ref, raw, kbuf, vbuf, sems,
                            *, layer, npages):
    def dst(slot, j):
        return raw.at[slot, pl.ds(pl.multiple_of(j * 2 * LANES, 2 * LANES), 2 * LANES), :]
    slot = _prefetch_pages(pt_ref, cache_ref, sems, layer, 0, npages, dst)

    def to_token_major(j, c):
        r0 = pl.multiple_of(j * 2 * LANES, 2 * LANES)
        t0 = pl.multiple_of(j * PAGE_SIZE, PAGE_SIZE)
        kbuf[pl.ds(t0, PAGE_SIZE), :] = raw[slot, pl.ds(r0, LANES), :].T
        vbuf[pl.ds(t0, PAGE_SIZE), :] = raw[slot, pl.ds(r0 + LANES, LANES), :].T
        return c
    lax.fori_loop(0, npages, to_token_major, 0, unroll=4)
    _compress_core(kbuf, vbuf, w4_ref, pe4_ref, cos_ref, sin_ref, out_ref, npages * PAGE_SIZE // CMP_BLOCK)


def _compress_sample(pt_flat, cache_fm, layer, bsz, npages, w4, pe4, ccos, csin):
    past = npages * PAGE_SIZE
    nb = past // CMP_BLOCK
    grid_spec = pltpu.PrefetchScalarGridSpec(
        num_scalar_prefetch=1, grid=(bsz,),
        in_specs=[pl.BlockSpec(memory_space=pl.ANY),
                  _const_spec(w4.shape), _const_spec(pe4.shape),
                  _const_spec(ccos.shape), _const_spec(csin.shape)],
        out_specs=pl.BlockSpec((nb, 2 * LANES), lambda b, pt: (b, 0)),
        scratch_shapes=[pltpu.VMEM((2, npages * 2 * LANES, PAGE_SIZE), F32),
                        pltpu.VMEM((past, LANES), F32), pltpu.VMEM((past, LANES), F32),
                        pltpu.SemaphoreType.DMA((2,))])
    return pl.pallas_call(
        functools.partial(_compress_sample_kernel, layer=layer, npages=npages),
        grid_spec=grid_spec,
        out_shape=jax.ShapeDtypeStruct((bsz * nb, 2 * LANES), F32),
        compiler_params=_cparams(1, disable_bounds_checks=True),
        name="compress_sample")(pt_flat, cache_fm, w4, pe4, ccos, csin)


def _cmp_positions(nc):
    n = lax.broadcasted_iota(I32, (1, nc), 1)
    half = nc // 2
    blk = jnp.where(n < half, 2 * n, 2 * (n - half) + 1)
    return blk * CMP_BLOCK + (CMP_BLOCK - 1)


def _softmax_cols(s, mask):
    s = jnp.where(mask, s, NEG)
    e = jnp.where(mask, jnp.exp(s - jnp.max(s, axis=0, keepdims=True)), 0.0)
    return e, 1.0 / jnp.maximum(jnp.sum(e, axis=0, keepdims=True), 1e-30)


def _pattn_kernel(q_ref, gl_ref, cmp_ref, ks_ref, vst_ref, kw0_ref, kw1_ref, kw2_ref, vwt0_ref, vwt1_ref,
                  vwt2_ref, o_ref, ks_bf, vst_bf, bias_ref, acc_ref, ot_ref, *, tq, t_len):
    ti = pl.program_id(1)
    nc = t_len // CMP_BLOCK
    nsel = t_len // SEL_BLOCK
    half = nc // 2
    blk_per_chunk = tq // SEL_BLOCK
    q0 = ti * tq

    @pl.when(ti == 0)
    def _():
        for g in range(N_KV_A):
            ks_bf[g] = ks_ref[:, g * HEAD_DIM:(g + 1) * HEAD_DIM].astype(BF16)
        vst_bf[...] = vst_ref[...].astype(BF16)

    qj = lax.broadcasted_iota(I32, (1, tq), 1)
    ki = lax.broadcasted_iota(I32, (tq, 1), 0)
    qpos = q0 + qj
    qs = (q_ref[...] * ATTN_SCALE).astype(BF16)
    gates_t = jax.nn.sigmoid(gl_ref[...]).T
    n = lax.broadcasted_iota(I32, (nc, 1), 0)
    cpos = jnp.where(n < half, 2 * n, 2 * (n - half) + 1) * CMP_BLOCK + (CMP_BLOCK - 1)
    m_c = cpos <= qpos
    blk = lax.broadcasted_iota(I32, (nsel, 1), 0)
    cur = qpos // SEL_BLOCK
    forced = (blk == cur) | (blk == 0)
    future = blk > cur
    later_blk = [jnp.where(blk > m, 1, 0) for m in range(nsel)]
    diag_bias = jnp.where(ki <= qj, 0.0, NEG)
    w_masks = []
    for k in range(3):
        dpos = qj - ki + (2 - k) * tq
        w_masks.append((dpos >= 0) & (dpos < WINDOW) & (ti + k - 2 >= 0))
    w_bias = jnp.where(jnp.concatenate(w_masks, axis=0), 0.0, NEG)
    kw_refs = (kw0_ref, kw1_ref, kw2_ref)
    vwt_refs = (vwt0_ref, vwt1_ref, vwt2_ref)

    for g in range(N_KV_A):
        ck = slice(g * HEAD_DIM, (g + 1) * HEAD_DIM)
        kc = cmp_ref[:, ck].astype(BF16)
        vc = cmp_ref[:, LANES + g * HEAD_DIM:LANES + (g + 1) * HEAD_DIM].astype(BF16)
        heads = [g * HEADS_PER_GROUP + r for r in range(HEADS_PER_GROUP)]
        q_heads = [qs[:, h * HEAD_DIM:(h + 1) * HEAD_DIM] for h in heads]
        o_cmp = []
        imp = jnp.zeros((nc, tq), F32)
        for s in [_dot_nt(kc, qh) for qh in q_heads]:
            e, inv = _softmax_cols(s, m_c)
            p = e * inv
            o_cmp.append(lax.dot_general(vc, p.astype(BF16), (((0,), (0,)), ((), ())),
                                         preferred_element_type=F32))
            imp = imp + p
        imp = imp[:half] + imp[half:]
        vals = jnp.where(forced, FORCE_SCORE, jnp.where(future, -1.0, imp))
        rank = jnp.zeros((nsel, tq), I32)
        for m in range(nsel):
            row = vals[m:m + 1, :]
            rank = rank + jnp.where(row > vals, 1, jnp.where(row == vals, later_blk[m], 0))
        bias_ref[g] = jnp.where(rank < min(TOP_N, nsel), 0.0, NEG)

        kw = jnp.concatenate([r[:, ck] for r in kw_refs], axis=0).astype(BF16)
        vwt = [r[ck, :].astype(BF16) for r in vwt_refs]

        def block_bias(c):
            rows = [jnp.broadcast_to(bias_ref[g, pl.ds(c * blk_per_chunk + j, 1), :], (SEL_BLOCK, tq))
                    for j in range(blk_per_chunk)]
            return jnp.concatenate(rows, axis=0)

        def chunk(c, carry, extra_bias=None):
            r0 = pl.multiple_of(c * tq, tq)
            k_chunk = ks_bf[g, pl.ds(r0, tq), :]
            v_chunk = vst_bf[ck, pl.ds(r0, tq)]
            bias = block_bias(c)
            if extra_bias is not None:
                bias = bias + extra_bias
            scores = [_dot_nt(k_chunk, qh) for qh in q_heads]
            out, probs, alphas = [], [], []
            for r in range(HEADS_PER_GROUP):
                m_run, l_run = carry[r]
                s = scores[r] + bias
                m_new = jnp.maximum(m_run, jnp.max(s, axis=0, keepdims=True))
                alpha = jnp.exp(m_run - m_new)
                p = jnp.exp(s - m_new)
                out.append((m_new, alpha * l_run + jnp.sum(p, axis=0, keepdims=True)))
                probs.append(p.astype(BF16))
                alphas.append(alpha)
            for r in range(HEADS_PER_GROUP):
                acc_ref[r] = alphas[r] * acc_ref[r] + _dot(v_chunk, probs[r])
            return tuple(out)

        acc_ref[...] = jnp.zeros(acc_ref.shape, F32)
        init = tuple((jnp.full((1, tq), -jnp.inf, F32), jnp.zeros((1, tq), F32)) for _ in heads)
        stats = chunk(ti, lax.fori_loop(0, ti, chunk, init), diag_bias)

        w_scores = [_dot_nt(kw, qh) + w_bias for qh in q_heads]
        w_probs = [jnp.exp(s - jnp.max(s, axis=0, keepdims=True)) for s in w_scores]
        for r, h in enumerate(heads):
            o_s = acc_ref[r] * (1.0 / jnp.maximum(stats[r][1], 1e-30))
            e = w_probs[r]
            inv = 1.0 / jnp.maximum(jnp.sum(e, axis=0, keepdims=True), 1e-30)
            e = e.astype(BF16)
            o_w = functools.reduce(jnp.add, [_dot(vwt[k], e[k * tq:(k + 1) * tq]) for k in range(3)]) * inv
            ot_ref[h * HEAD_DIM:(h + 1) * HEAD_DIM, :] = (
                gates_t[3 * h:3 * h + 1, :] * o_cmp[r] + gates_t[3 * h + 1:3 * h + 2, :] * o_s
                + gates_t[3 * h + 2:3 * h + 3, :] * o_w)
    o_ref[...] = ot_ref[...].T


def _block_expand(nblk, nkeys):
    return (jnp.arange(nkeys)[None, :] // SEL_BLOCK == jnp.arange(nblk)[:, None]).astype(BF16)


def _pattn(q, gl, cmp, kv4, kvt, wr, wrt, bsz, t):
    tq = Q_TILE
    nt = t // tq
    nc = t // CMP_BLOCK
    nsel = t // SEL_BLOCK
    row = lambda width: pl.BlockSpec((tq, width), lambda b, i: (b * nt + i, 0))
    kw = lambda back: pl.BlockSpec((tq, LANES), lambda b, i: (b * nt + jnp.maximum(i - back, 0), 0))
    vwt = lambda back: pl.BlockSpec((LANES, tq), lambda b, i: (b * 2 + 1, jnp.maximum(i - back, 0)))
    return pl.pallas_call(
        functools.partial(_pattn_kernel, tq=tq, t_len=t), grid=(bsz, nt),
        in_specs=[row(Q_COLS), row(LANES),
                  pl.BlockSpec((nc, 2 * LANES), lambda b, i: (b, 0)),
                  pl.BlockSpec((t, LANES), lambda b, i: (b, 2)),
                  pl.BlockSpec((LANES, t), lambda b, i: (b * 4 + 3, 0)),
                  kw(2), kw(1), kw(0), vwt(2), vwt(1), vwt(0)],
        out_specs=row(NSA_WIDTH),
        out_shape=jax.ShapeDtypeStruct((bsz * t, NSA_WIDTH), F32),
        scratch_shapes=[pltpu.VMEM((N_KV_A, t, HEAD_DIM), BF16), pltpu.VMEM((LANES, t), BF16),
                        pltpu.VMEM((N_KV_A, nsel, tq), F32), pltpu.VMEM((HEADS_PER_GROUP, HEAD_DIM, tq), F32),
                        pltpu.VMEM((NSA_WIDTH, tq), F32)],
        compiler_params=_cparams(2), name="nsa_prompt")(q, gl, cmp, kv4, kvt, wr, wr, wr, wrt, wrt, wrt)


def _sattn_kernel(pt_ref, cache_ref, q_ref, gl_ref, cmp_ref, kvn_ref, wrn_ref, win_ref, e_ref, o_ref,
                  buf, sems, *, layer, npages, ts):
    past = npages * PAGE_SIZE

    def dst(slot, j):
        return buf.at[slot, :, pl.ds(pl.multiple_of(j * PAGE_SIZE, PAGE_SIZE), PAGE_SIZE)]
    slot = _prefetch_pages(pt_ref, cache_ref, sems, layer, 2 * LANES, npages, dst)
    nc = past // CMP_BLOCK
    nblk_past = past // SEL_BLOCK
    nsel = nblk_past + 1
    lanes_sel = 2 * nblk_past
    rows = HEADS_PER_GROUP * ts
    tok1 = lax.broadcasted_iota(I32, (ts, 1), 0)
    tok = jnp.concatenate([tok1] * HEADS_PER_GROUP, axis=0)
    qpos1 = past + tok1
    qpos = past + tok
    qs = q_ref[...] * ATTN_SCALE
    gates = jax.nn.sigmoid(gl_ref[...])
    m_c = _cmp_positions(nc) <= qpos
    blk = lax.broadcasted_iota(I32, (1, lanes_sel), 1)
    cur = qpos1 // SEL_BLOCK
    forced = (blk == cur) | (blk == 0)
    future = blk > cur
    m_past = lax.broadcasted_iota(I32, (1, past), 1) <= qpos
    tkey = lax.broadcasted_iota(I32, (1, ts), 1)
    m_new = tkey <= tok
    win_len = win_ref.shape[1]
    wpos = past - win_len + lax.broadcasted_iota(I32, (1, win_len), 1)
    dpos = qpos - wpos
    m_wstate = (dpos >= 0) & (dpos < WINDOW) & (wpos >= 0)
    dnew = tok - tkey
    m_wnew = (dnew >= 0) & (dnew < WINDOW)
    for g in range(N_KV_A):
        ck = slice(g * HEAD_DIM, (g + 1) * HEAD_DIM)
        cv = slice(LANES + g * HEAD_DIM, LANES + (g + 1) * HEAD_DIM)
        heads = [g * HEADS_PER_GROUP + r for r in range(HEADS_PER_GROUP)]
        qg = jnp.concatenate([qs[:, h * HEAD_DIM:(h + 1) * HEAD_DIM] for h in heads], axis=0).astype(BF16)
        kc = cmp_ref[:, ck].astype(BF16)
        vc = cmp_ref[:, cv].astype(BF16)
        (e,), inv = _softmax_parts([(_dot_nt(qg, kc), m_c)])
        p = e * inv
        o_c = _dot(p.astype(BF16), vc)
        imp = p[0:ts]
        for r in range(1, HEADS_PER_GROUP):
            imp = imp + p[r * ts:(r + 1) * ts]
        imp = imp[:, :nc // 2] + imp[:, nc // 2:]
        imp = jnp.concatenate([imp, jnp.zeros((ts, lanes_sel - nc // 2), F32)], axis=1)
        vals = jnp.where(forced, FORCE_SCORE, jnp.where(future, -1.0, imp))
        sel = jnp.where(_topk_mask(vals, nsel, min(TOP_N, nsel)), 1.0, 0.0)
        sel = jnp.concatenate([sel] * HEADS_PER_GROUP, axis=0)
        sel_keys = _dot(sel[:, :nblk_past].astype(BF16), e_ref[...])
        m_s_past = (sel_keys > 0.5) & m_past
        m_s_new = (sel[:, nblk_past:nblk_past + 1] > 0.5) & m_new
        kn = kvn_ref[:, 2 * LANES:4 * LANES]
        o_s = _attend(qg, [(buf[slot, ck, :].astype(BF16), buf[slot, cv, :].astype(BF16), m_s_past, True),
                           (kn[:, ck].astype(BF16), kn[:, cv].astype(BF16), m_s_new, False)])
        o_w = _attend(qg, [(win_ref[ck, :].astype(BF16), win_ref[cv, :].astype(BF16), m_wstate, True),
                           (wrn_ref[:, ck].astype(BF16), wrn_ref[:, cv].astype(BF16), m_wnew, False)])
        for r, h in enumerate(heads):
            rs = slice(r * ts, (r + 1) * ts)
            o_ref[:, h * HEAD_DIM:(h + 1) * HEAD_DIM] = (
                gates[:, 3 * h:3 * h + 1] * o_c[rs] + gates[:, 3 * h + 1:3 * h + 2] * o_s[rs]
                + gates[:, 3 * h + 2:3 * h + 3] * o_w[rs])


def _sattn(pt_flat, cache_fm, layer, q, gl, cmp, kv4, wr, win_fm, bsz, ts, npages):
    past = npages * PAGE_SIZE
    nc = past // CMP_BLOCK
    win_len = win_fm.shape[3]
    expand = _block_expand(past // SEL_BLOCK, past)
    row = lambda width: pl.BlockSpec((ts, width), lambda b, pt: (b, 0))
    grid_spec = pltpu.PrefetchScalarGridSpec(
        num_scalar_prefetch=1, grid=(bsz,),
        in_specs=[pl.BlockSpec(memory_space=pl.ANY), row(Q_COLS), row(LANES),
                  pl.BlockSpec((nc, 2 * LANES), lambda b, pt: (b, 0)),
                  row(4 * LANES), row(2 * LANES),
                  pl.BlockSpec((None, None, 2 * LANES, win_len), lambda b, pt: (layer, b, 0, 0)),
                  _const_spec(expand.shape)],
        out_specs=row(NSA_WIDTH),
        scratch_shapes=[pltpu.VMEM((2, 2 * LANES, past), F32), pltpu.SemaphoreType.DMA((2,))])
    return pl.pallas_call(
        functools.partial(_sattn_kernel, layer=layer, npages=npages, ts=ts),
        grid_spec=grid_spec,
        out_shape=jax.ShapeDtypeStruct((bsz * ts, NSA_WIDTH), F32),
        compiler_params=_cparams(1, disable_bounds_checks=True),
        name="nsa_sample")(pt_flat, cache_fm, q, gl, cmp, kv4, wr, win_fm, expand)


RG_STATE_ROWS = SUBLANES
RG_ROWS = 256


def _rglru_kernel(rx_ref, ry_ref, st_ref, cw_ref, cb_ref, wa_ref, ba_ref, wx_ref, bx_ref, lam_ref,
                  out_ref, hl_ref, nb_ref, xs, a_s, b_s, *, t_len):
    ch = min(RG_ROWS, t_len)
    xs[0:RG_STATE_ROWS, :] = st_ref[0]
    xs[RG_STATE_ROWS:RG_STATE_ROWS + t_len, :] = rx_ref[...]
    lam = lam_ref[...]
    softplus_neg_lam = jnp.maximum(-lam, 0.0) + jnp.log1p(jnp.exp(-jnp.abs(lam)))
    sub = lax.broadcasted_iota(I32, (ch, RG_WIDTH), 0) & (SUBLANES - 1)
    for c in range(t_len // ch):
        r0 = c * ch
        xc = cb_ref[...]
        for j in range(RG_CONV):
            xc = xc + cw_ref[j:j + 1, :] * xs[r0 + RG_STATE_ROWS - (RG_CONV - 1) + j:
                                              r0 + RG_STATE_ROWS - (RG_CONV - 1) + j + ch, :]
        xb = xc.astype(BF16)
        r = jax.nn.sigmoid(_dot(xb, wa_ref[...]) + ba_ref[...])
        i = jax.nn.sigmoid(_dot(xb, wx_ref[...]) + bx_ref[...])
        log_a = -RG_C * r * softplus_neg_lam
        a = jnp.exp(log_a)
        one_minus_a2 = -jnp.tanh(log_a) * (jnp.exp(2.0 * log_a) + 1.0)
        bb = jnp.sqrt(one_minus_a2) * (i * xc)
        for s in (1, 2, 4):
            ok = sub >= s
            a_prev = pltpu.roll(a, s, 0)
            b_prev = pltpu.roll(bb, s, 0)
            bb = jnp.where(ok, a * b_prev + bb, bb)
            a = jnp.where(ok, a * a_prev, a)
        a_s[r0:r0 + ch, :] = a
        b_s[r0:r0 + ch, :] = bb

    def step(k, h):
        r0 = pl.multiple_of(k * SUBLANES, SUBLANES)
        hk = b_s[pl.ds(r0, SUBLANES), :] + a_s[pl.ds(r0, SUBLANES), :] * h
        b_s[pl.ds(r0, SUBLANES), :] = hk
        return jnp.broadcast_to(hk[SUBLANES - 1:SUBLANES, :], (SUBLANES, RG_WIDTH))

    h0 = jnp.broadcast_to(st_ref[0, 0:1, :], (SUBLANES, RG_WIDTH))
    h_fin = lax.fori_loop(0, t_len // SUBLANES, step, h0)
    hl_ref[0] = h_fin[0:1, :]
    nb_ref[0] = xs[RG_STATE_ROWS + t_len - (RG_CONV - 1):RG_STATE_ROWS + t_len, :]
    for c in range(t_len // ch):
        r0 = c * ch
        out_ref[r0:r0 + ch, :] = b_s[r0:r0 + ch, :] * jax.nn.gelu(ry_ref[r0:r0 + ch, :])


def _rglru(rx, ry, st, cw, cb, wa, ba, wx, bx, lam, bsz, t):
    row = pl.BlockSpec((t, RG_WIDTH), lambda b: (b, 0))
    vec = _const_spec((1, RG_WIDTH))
    return pl.pallas_call(
        functools.partial(_rglru_kernel, t_len=t), grid=(bsz,),
        in_specs=[row, row, pl.BlockSpec((1, RG_STATE_ROWS, RG_WIDTH), lambda b: (b, 0, 0)),
                  _const_spec(cw.shape), vec, _const_spec(wa.shape), vec, _const_spec(wx.shape), vec, vec],
        out_specs=[row, pl.BlockSpec((1, 1, RG_WIDTH), lambda b: (b, 0, 0)),
                   pl.BlockSpec((1, RG_CONV - 1, RG_WIDTH), lambda b: (b, 0, 0))],
        out_shape=[jax.ShapeDtypeStruct((bsz * t, RG_WIDTH), F32),
                   jax.ShapeDtypeStruct((bsz, 1, RG_WIDTH), F32),
                   jax.ShapeDtypeStruct((bsz, RG_CONV - 1, RG_WIDTH), F32)],
        scratch_shapes=[pltpu.VMEM((RG_STATE_ROWS + t, RG_WIDTH), F32),
                        pltpu.VMEM((t, RG_WIDTH), F32), pltpu.VMEM((t, RG_WIDTH), F32)],
        compiler_params=_cparams(1), name="rglru")(rx, ry, st, cw, cb, wa, ba, wx, bx, lam)


def _proj_ln_kernel(*refs, n_in):
    a_refs = refs[:n_in]
    w_refs = refs[n_in:2 * n_in]
    x_ref, mod_ref, bias_ref, g_ref, b_ref, o_ref = refs[2 * n_in:]
    out = bias_ref[...]
    for a_ref, w_ref in zip(a_refs, w_refs):
        out = out + _dot(a_ref[...].astype(BF16), w_ref[...])
    o_ref[...] = _post_norm(x_ref[...], mod_ref[0], out, g_ref[...], b_ref[...])


def _proj_ln(a_list, w_list, x2d, mod3, bias, g, b, tm, tps):
    m = x2d.shape[0]
    n_in = len(a_list)
    row = lambda width: pl.BlockSpec((tm, width), lambda i: (i, 0))
    vec = _const_spec((1, D_MODEL))
    return pl.pallas_call(
        functools.partial(_proj_ln_kernel, n_in=n_in), grid=(m // tm,),
        in_specs=[row(a.shape[1]) for a in a_list] + [_const_spec(w.shape) for w in w_list]
        + [row(D_MODEL), _mod_spec(mod3, tm, tps), vec, vec, vec],
        out_specs=row(D_MODEL),
        out_shape=jax.ShapeDtypeStruct((m, D_MODEL), F32),
        compiler_params=_cparams(1), name="proj_postnorm")(*a_list, *w_list, x2d, mod3, bias, g, b)


def _ffn_kernel(x_ref, mod_ref, w1_ref, w3_ref, w2_ref, g_ref, b_ref, o_ref, *, ff):
    x = x_ref[...]
    mod = mod_ref[0]
    h = _modulate(x, mod).astype(BF16)
    acc = jnp.zeros(x.shape, F32)
    for c in range(ff // FF_CHUNK):
        cs = slice(c * FF_CHUNK, (c + 1) * FF_CHUNK)
        z = _silu(_dot(h, w1_ref[:, cs])) * _dot(h, w3_ref[:, cs])
        acc = acc + _dot(z.astype(BF16), w2_ref[cs, :])
    o_ref[...] = _post_norm(x, mod, acc, g_ref[...], b_ref[...])


def _ffn(x2d, mod3, w1, w3, w2, g, b, tm, tps):
    m = x2d.shape[0]
    row = pl.BlockSpec((tm, D_MODEL), lambda i: (i, 0))
    vec = _const_spec((1, D_MODEL))
    return pl.pallas_call(
        functools.partial(_ffn_kernel, ff=w1.shape[1]), grid=(m // tm,),
        in_specs=[row, _mod_spec(mod3, tm, tps), _const_spec(w1.shape, True), _const_spec(w3.shape, True),
                  _const_spec(w2.shape, True), vec, vec],
        out_specs=row,
        out_shape=jax.ShapeDtypeStruct((m, D_MODEL), F32),
        compiler_params=_cparams(1), name="dense_ffn")(x2d, mod3, w1, w3, w2, g, b)


def _cf_in_kernel(x_ref, mod_ref, w_ref, b_ref, o_ref):
    h = _modulate(x_ref[...], mod_ref[0]).astype(BF16)
    u = _dot(h, w_ref[...]) + b_ref[...]
    o_ref[...] = u[:, :D_MODEL] * jax.nn.sigmoid(u[:, D_MODEL:])


def _cf_in(x2d, mod3, w, bias, tm, tps):
    m = x2d.shape[0]
    row = pl.BlockSpec((tm, D_MODEL), lambda i: (i, 0))
    return pl.pallas_call(
        _cf_in_kernel, grid=(m // tm,),
        in_specs=[row, _mod_spec(mod3, tm, tps), _const_spec(w.shape), _const_spec(bias.shape)],
        out_specs=row,
        out_shape=jax.ShapeDtypeStruct((m, D_MODEL), F32),
        compiler_params=_cparams(1), name="conformer_in")(x2d, mod3, w, bias)


def _cf_conv_kernel(x_ref, halo_ref, dw_ref, db_ref, g_ref, b_ref, z_ref, s_ref, *, tt, zero_first):
    halo = halo_ref[...]
    if zero_first:
        halo = jnp.where(pl.program_id(1) == 0, 0.0, halo)
    s_ref[0, 0:CF_HALO, :] = halo
    s_ref[0, CF_HALO:CF_HALO + tt, :] = x_ref[...]
    n_sh = CF_HALO + tt - SUBLANES
    for k in range(1, SUBLANES):
        s_ref[k, 0:n_sh, :] = s_ref[0, k:k + n_sh, :]
    sub = min(CONV_SUB, tt)
    first = CF_HALO - (CF_KERNEL - 1)
    for c in range(tt // sub):
        r0 = c * sub
        y = jnp.broadcast_to(db_ref[...], (sub // SUBLANES, SUBLANES, D_MODEL))
        for j in range(CF_KERNEL):
            k = (first + j) % SUBLANES
            a0 = r0 + first + j - k
            y = y + dw_ref[j] * s_ref[k, a0:a0 + sub, :].reshape(sub // SUBLANES, SUBLANES, D_MODEL)
        z_ref[r0:r0 + sub, :] = _silu(_layer_norm(y.reshape(sub, D_MODEL), g_ref[...], b_ref[...]))


def _cf_conv(glu, halo_src, halo_map, dw, db, g, b, bsz, t, tt, zero_first):
    nt = t // tt
    vec = _const_spec((1, D_MODEL))
    return pl.pallas_call(
        functools.partial(_cf_conv_kernel, tt=tt, zero_first=zero_first), grid=(bsz, nt),
        in_specs=[pl.BlockSpec((tt, D_MODEL), lambda bi, ti: (bi * nt + ti, 0)),
                  pl.BlockSpec((CF_HALO, D_MODEL), halo_map),
                  _const_spec(dw.shape), vec, vec, vec],
        out_specs=pl.BlockSpec((tt, D_MODEL), lambda bi, ti: (bi * nt + ti, 0)),
        out_shape=jax.ShapeDtypeStruct((bsz * t, D_MODEL), F32),
        scratch_shapes=[pltpu.VMEM((SUBLANES, CF_HALO + tt, D_MODEL), F32)],
        compiler_params=_cparams(2), name="conformer_conv")(glu, halo_src, dw, db, g, b)


def _router_kernel(x_ref, mod_ref, rw_ref, h_ref, ids_ref, gates_ref, *, n_real):
    h = _modulate(x_ref[...], mod_ref[0])
    h_ref[...] = jnp.where(pl.program_id(0) < n_real, h, 0.0)
    logits = lax.dot_general(h, rw_ref[...], (((1,), (0,)), ((), ())), precision=lax.Precision.HIGHEST,
                             preferred_element_type=F32)
    lane = lax.broadcasted_iota(I32, logits.shape, 1)
    logits = jnp.where(lane < N_EXPERTS, logits, -jnp.inf)
    m1 = jnp.max(logits, axis=-1, keepdims=True)
    i1 = jnp.min(jnp.where(logits == m1, lane, LANES), axis=-1, keepdims=True)
    rest = jnp.where(lane == i1, -jnp.inf, logits)
    m2 = jnp.max(rest, axis=-1, keepdims=True)
    i2 = jnp.min(jnp.where(rest == m2, lane, LANES), axis=-1, keepdims=True)
    e2 = jnp.exp(m2 - m1)
    inv = 1.0 / (1.0 + e2)
    col = lax.broadcasted_iota(I32, ids_ref.shape, 1)
    ids_ref[...] = jnp.where(col == 0, i1, jnp.where(col == 1, i2, 0))
    gates_ref[...] = jnp.where(col == 0, inv, jnp.where(col == 1, e2 * inv, 0.0))


def _router(x2d, mod3, rw, tm, tps, n_total, row_off, h_prev):
    m = x2d.shape[0]
    n_real = m // tm
    off = row_off // tm
    if h_prev is None:
        steps = -(-n_total // tm)
        clamp = lambda i: jnp.minimum(i, n_real - 1)
    else:
        steps = n_real
        clamp = lambda i: i
    if mod3.shape[1] == 1:
        mspec = pl.BlockSpec((1, 1, mod3.shape[2]), lambda i: (clamp(i) // tps, 0, 0))
    else:
        mspec = pl.BlockSpec((1, tm, mod3.shape[2]), lambda i: (0, clamp(i), 0))
    small = lambda: pl.BlockSpec((tm, SUBLANES), lambda i: (clamp(i), 0))
    in_specs = [pl.BlockSpec((tm, D_MODEL), lambda i: (clamp(i), 0)), mspec, _const_spec(rw.shape)]
    args = [x2d, mod3, rw]
    kern = functools.partial(_router_kernel, n_real=n_real)
    aliases = {}
    if h_prev is not None:
        in_specs.append(pl.BlockSpec(memory_space=pl.ANY))
        args.append(h_prev)
        aliases = {3: 0}
        kern = lambda x, md, rw_, hp, h, ids, gt: _router_kernel(x, md, rw_, h, ids, gt, n_real=n_real)
    return pl.pallas_call(
        kern, grid=(steps,), in_specs=in_specs,
        out_specs=[pl.BlockSpec((tm, D_MODEL), lambda i: (i + off, 0)), small(), small()],
        out_shape=[jax.ShapeDtypeStruct((n_total, D_MODEL), F32), jax.ShapeDtypeStruct((m, SUBLANES), I32),
                   jax.ShapeDtypeStruct((m, SUBLANES), F32)],
        input_output_aliases=aliases,
        compiler_params=_cparams(1), name="moe_router")(*args)


ROW_SRC_BITS = 15


DMA_GROUP = 8


def _for_rows(n, fn):
    full = n // DMA_GROUP

    def group(gi, c):
        for u in range(DMA_GROUP):
            fn(gi * DMA_GROUP + u)
        return c
    lax.fori_loop(0, full, group, 0)
    if isinstance(n, int):
        for r in range(full * DMA_GROUP, n):
            fn(r)
    else:
        for u in range(DMA_GROUP - 1):
            r = full * DMA_GROUP + u

            @pl.when(r < n)
            def _():
                fn(r)


def _experts_kernel(blk_e_ref, plan_ref, cnt_ref, nused_ref, h_ref, w1_ref, w3_ref, w2_ref, y_ref,
                    xs, xb, acc, ybuf, gsem, ssem, *, n_dst):
    i = pl.program_id(0)
    j = pl.program_id(1)
    nused = nused_ref[0]
    used = i < nused
    last = j == pl.num_programs(1) - 1
    src_mask = (1 << ROW_SRC_BITS) - 1
    rows_per_step = MOE_TILE // MOE_STEPS

    def gather_row(blk, r):
        tok = plan_ref[blk * MOE_TILE + r] & src_mask
        pltpu.make_async_copy(h_ref.at[pl.ds(tok, 1), :], xs.at[blk % 2, pl.ds(r, 1), :], gsem.at[blk % 2]).start()

    def wait_gather(slot):
        pltpu.make_async_copy(h_ref.at[pl.ds(0, MOE_TILE), :], xs.at[slot], gsem.at[slot]).wait()

    def scatter_row(blk, n_valid, r):
        dst = plan_ref[blk * MOE_TILE + r] >> ROW_SRC_BITS
        dst = jnp.where(r < n_valid, dst, n_dst + r)
        pltpu.make_async_copy(ybuf.at[pl.ds(r, 1), :], y_ref.at[pl.ds(dst, 1), :], ssem).start()

    def wait_scatter():
        pltpu.make_async_copy(ybuf, y_ref.at[pl.ds(0, MOE_TILE), :], ssem).wait()

    @pl.when(used & (j == 0))
    def _():
        @pl.when(i == 0)
        def _():
            _for_rows(MOE_TILE, lambda r: gather_row(i, r))
            ybuf[...] = jnp.zeros(ybuf.shape, F32)
        wait_gather(i % 2)
        xb[...] = xs[i % 2].astype(BF16)
        acc[...] = jnp.zeros(acc.shape, F32)

    @pl.when(used)
    def _():
        nxt = jnp.minimum(i + 1, nused - 1)
        prev = jnp.maximum(i - 1, 0)
        n_prev = jnp.where(i >= 1, cnt_ref[prev], 0)
        r0 = j * rows_per_step
        for u in range(rows_per_step):
            gather_row(nxt, r0 + u)
            scatter_row(prev, n_prev, r0 + u)
        x = xb[...]
        z = _silu(_dot(x, w1_ref[...])) * _dot(x, w3_ref[...])
        acc[...] += _dot(z.astype(BF16), w2_ref[...])

    @pl.when(used & last)
    def _():
        wait_scatter()
        ybuf[...] = acc[...]

        @pl.when(i == nused - 1)
        def _():
            _for_rows(MOE_TILE, lambda r: scatter_row(i, cnt_ref[i], r))
            wait_scatter()
            wait_gather(i % 2)


def _experts(blk_e, plan, cnt, nused, h_all, n_tok, w1, w3, w2, layer, n_blk):
    ff = w1.shape[3]
    nj = ff // MOE_FF_CHUNK

    def jj(i, j, nu):
        return jnp.where(i < nu[0], j, nj - 1)

    wspec = lambda shape, imap: pl.BlockSpec(shape, imap)
    up = (None, None, D_MODEL, MOE_FF_CHUNK)
    grid_spec = pltpu.PrefetchScalarGridSpec(
        num_scalar_prefetch=4, grid=(n_blk, nj),
        in_specs=[pl.BlockSpec(memory_space=pl.ANY),
                  wspec(up, lambda i, j, be, pn, ct, nu: (layer, be[i], 0, jj(i, j, nu))),
                  wspec(up, lambda i, j, be, pn, ct, nu: (layer, be[i], 0, jj(i, j, nu))),
                  wspec((None, None, MOE_FF_CHUNK, D_MODEL),
                        lambda i, j, be, pn, ct, nu: (layer, be[i], jj(i, j, nu), 0))],
        out_specs=pl.BlockSpec(memory_space=pl.ANY),
        scratch_shapes=[pltpu.VMEM((2, MOE_TILE, D_MODEL), F32), pltpu.VMEM((MOE_TILE, D_MODEL), BF16),
                        pltpu.VMEM((MOE_TILE, D_MODEL), F32), pltpu.VMEM((MOE_TILE, D_MODEL), F32),
                        pltpu.SemaphoreType.DMA((2,)), pltpu.SemaphoreType.DMA(())])
    assert nj == MOE_STEPS and ff == nj * MOE_FF_CHUNK
    return pl.pallas_call(
        functools.partial(_experts_kernel, n_dst=2 * n_tok), grid_spec=grid_spec,
        out_shape=jax.ShapeDtypeStruct((2 * n_tok + MOE_TILE, D_MODEL), F32),
        compiler_params=_cparams(2, disable_bounds_checks=True),
        name="moe_experts")(blk_e, plan, cnt, nused, h_all, w1, w3, w2)


def _combine_kernel(y0_ref, y1_ref, gates_ref, x_ref, mod_ref, g_ref, b_ref, o_ref):
    gates = gates_ref[...]
    out = gates[:, 0:1] * y0_ref[...] + gates[:, 1:2] * y1_ref[...]
    o_ref[...] = _post_norm(x_ref[...], mod_ref[0], out, g_ref[...], b_ref[...])


def _combine(ys, gates, x2d, mod3, g, b, tm, tps, tok_off, n_tok):
    m = x2d.shape[0]
    off0 = tok_off // tm
    off1 = (n_tok + tok_off) // tm
    row = lambda width: pl.BlockSpec((tm, width), lambda i: (i, 0))
    vec = _const_spec((1, D_MODEL))
    return pl.pallas_call(
        _combine_kernel, grid=(m // tm,),
        in_specs=[pl.BlockSpec((tm, D_MODEL), lambda i: (i + off0, 0)),
                  pl.BlockSpec((tm, D_MODEL), lambda i: (i + off1, 0)),
                  row(SUBLANES), row(D_MODEL), _mod_spec(mod3, tm, tps), vec, vec],
        out_specs=row(D_MODEL),
        out_shape=jax.ShapeDtypeStruct((m, D_MODEL), F32),
        compiler_params=_cparams(1), name="moe_combine")(ys, ys, gates, x2d, mod3, g, b)


def _route_plan(ids_all, n_tok):
    fe = ids_all[:, :2].reshape(-1)
    onehot = (fe[:, None] == jnp.arange(N_EXPERTS, dtype=I32)[None, :]).astype(I32)
    csum = jnp.cumsum(onehot, axis=0)
    rank = jnp.take_along_axis(csum, fe[:, None], axis=1)[:, 0] - 1
    counts = csum[-1]
    padded = (counts + MOE_TILE - 1) // MOE_TILE * MOE_TILE
    pend = jnp.cumsum(padded)
    pstart = pend - padded
    dest = (pstart[fe] + rank).astype(I32)
    n_blk = -(-(2 * n_tok) // MOE_TILE) + N_EXPERTS
    a = jnp.arange(2 * n_tok, dtype=I32)
    word = (a // 2) | (((a % 2) * n_tok + a // 2) << ROW_SRC_BITS)
    plan = jnp.zeros((n_blk * MOE_TILE,), I32).at[dest].set(word)
    nused = (pend[-1] // MOE_TILE).astype(I32)
    blk = jnp.arange(n_blk, dtype=I32)
    blk_e = jnp.searchsorted(pend, jnp.minimum(blk, nused - 1) * MOE_TILE, side='right').astype(I32)
    blk_e = jnp.minimum(blk_e, N_EXPERTS - 1)
    cnt = jnp.clip(pstart[blk_e] + counts[blk_e] - blk * MOE_TILE, 0, MOE_TILE)
    cnt = jnp.where(blk < nused, cnt, 0).astype(I32)
    return plan, blk_e, cnt, nused.reshape(1), n_blk


def _rope_tables(pos):
    half = HEAD_DIM // 2
    inv = 1.0 / (ROPE_THETA ** (jnp.arange(half, dtype=F32) * (2.0 / HEAD_DIM)))
    ang = pos.astype(F32)[:, None] * inv[None, :]
    c = jnp.cos(ang)
    s = jnp.sin(ang)
    return jnp.concatenate([c, c, c, c], axis=1), jnp.concatenate([-s, s, -s, s], axis=1)


def _cmp_rope_tables(nc):
    blk = jnp.concatenate([jnp.arange(0, nc, 2), jnp.arange(1, nc, 2)])
    return _rope_tables(blk * CMP_BLOCK + (CMP_BLOCK - 1))


def _w_in_layout(w_in):
    o1 = Q_COLS
    o2 = o1 + KV_COLS
    o3 = o2 + GATE_COLS
    pad = jnp.zeros((D_MODEL, LANES - GATE_COLS), w_in.dtype)
    return jnp.concatenate([w_in[:, :o2], w_in[:, o3:], w_in[:, o2:o3], pad], axis=1).astype(BF16)


def _block_diag(blocks):
    n, a, b = blocks.shape
    eye = jnp.eye(n, dtype=blocks.dtype)
    return (eye[:, None, :, None] * blocks[:, :, None, :]).reshape(n * a, n * b)


def _cmp_weights(w_ck, w_cv, pe_k, pe_v):
    wk = w_ck.reshape(CMP_BLOCK, HEAD_DIM, HEAD_DIM)
    wv = w_cv.reshape(CMP_BLOCK, HEAD_DIM, HEAD_DIM)
    w4 = jax.vmap(lambda a, b: _block_diag(jnp.stack([a, a, b, b])))(wk, wv).astype(BF16)
    pe4 = jnp.concatenate([pe_k, pe_k, pe_v, pe_v], axis=1)
    return w4, pe4


def kernel(x_prompt, x_sample, c_prompt, c_sample, page_table, cache_nsa_kv, state_nsa_win, state_rglru_h, state_rglru_conv, state_conformer_conv, w_mod, b_mod, ln_g, ln_b, w_in_even, w_out_even, w_cmp_k, w_cmp_v, pe_cmp_k, pe_cmp_v, rg_conv_w, rg_conv_b, rg_wa, rg_ba, rg_wx, rg_bx, rg_lam, cf_w1, cf_b1, cf_dw, cf_db, cf_ln_g, cf_ln_b, cf_w2, cf_b2, ff_w1, ff_w3, ff_w2, moe_router, moe_w1, moe_w3, moe_w2):
    bp, tp, d = x_prompt.shape
    bs, ts, _ = x_sample.shape
    npages = page_table.shape[1]
    past = npages * PAGE_SIZE
    n_p = bp * tp
    n_s = bs * ts
    n_tok = n_p + n_s
    assert d == D_MODEL and tp % ROW_TILE == 0 and tp % Q_TILE == 0 and tp >= WINDOW
    assert past % SEL_BLOCK == 0 and ts <= CMP_BLOCK and ts % SUBLANES == 0 and n_s % SUBLANES == 0
    assert n_p % MOE_TILE == 0 and n_p % n_s == 0
    tps_p = tp // ROW_TILE
    xp = x_prompt.reshape(n_p, d)
    xs = x_sample.reshape(n_s, d)
    vec = lambda v: v.reshape(1, -1)

    mod_all = _mod_all(jnp.concatenate([c_prompt, c_sample], axis=0), w_mod, b_mod)

    def mods(l, s):
        mrow = mod_all[2 * l + s]
        return mrow[:bp].reshape(bp, 1, 3 * d), jnp.repeat(mrow[bp:], ts, axis=0).reshape(1, n_s, 3 * d)

    cos_p, sin_p = _rope_tables(jnp.arange(tp))
    cos_s, sin_s = _rope_tables(jnp.tile(past + jnp.arange(ts), bs))
    ccos_p, csin_p = _cmp_rope_tables(tp // CMP_BLOCK)
    ccos_s, csin_s = _cmp_rope_tables(past // CMP_BLOCK)
    pt_flat = page_table.reshape(-1).astype(I32)
    n_even = cache_nsa_kv.shape[0]
    win_len = state_nsa_win.shape[2]
    cache_fm = cache_nsa_kv.transpose(0, 1, 3, 4, 5, 2).reshape(n_even, cache_nsa_kv.shape[1], 4 * LANES, PAGE_SIZE)
    win_fm = state_nsa_win.transpose(0, 1, 3, 4, 5, 2).reshape(n_even, bs, 2 * LANES, win_len)
    tm_c = min(Q_TILE, n_s)
    assert n_p % tm_c == 0 and n_tok % tm_c == 0
    assert n_tok < (1 << ROW_SRC_BITS) and 2 * n_tok < (1 << (31 - ROW_SRC_BITS))

    moe_w1_bf, moe_w3_bf, moe_w2_bf = moe_w1.astype(BF16), moe_w3.astype(BF16), moe_w2.astype(BF16)

    kv_p, kv_s, win_p, win_s, rh_p, rh_s, rc_p, rc_s, cc_p, cc_s = ([] for _ in range(10))
    for l in range(DEPTH):
        i = l // 2
        mp0, ms0 = mods(l, 0)
        mp1, ms1 = mods(l, 1)
        g0, b0, g1, b1 = vec(ln_g[l, 0]), vec(ln_b[l, 0]), vec(ln_g[l, 1]), vec(ln_b[l, 1])
        zero_bias = jnp.zeros((1, d), F32)
        if l % 2 == 0:
            w_in = _w_in_layout(w_in_even[i])
            w4, pe4 = _cmp_weights(w_cmp_k[i], w_cmp_v[i], pe_cmp_k[i], pe_cmp_v[i])
            wo_a = w_out_even[i][:NSA_WIDTH].astype(BF16)
            wo_r = w_out_even[i][NSA_WIDTH:].astype(BF16)
            wa = _block_diag(rg_wa[i]).astype(BF16)
            wx = _block_diag(rg_wx[i]).astype(BF16)
            rg_args = (rg_conv_w[i], vec(rg_conv_b[i]), wa, vec(rg_ba[i]), wx, vec(rg_bx[i]), vec(rg_lam[i]))

            q, kv4, wr, rx, ry, gl, kvt, wrt = _win_call(xp, mp0, w_in, cos_p, sin_p, ROW_TILE, tps_p, tp)
            cmp = _compress_prompt(kv4, bp, tp, w4, pe4, ccos_p, csin_p)
            o = _pattn(q, gl, cmp, kv4, kvt, wr, wrt, bp, tp)
            st = jnp.zeros((bp, RG_STATE_ROWS, RG_WIDTH), F32)
            rg, h_last, new_buf = _rglru(rx, ry, st, *rg_args, bp, tp)
            xp = _proj_ln([o, rg], [wo_a, wo_r], xp, mp0, zero_bias, g0, b0, ROW_TILE, tps_p)
            kv_p.append(kvt.reshape(bp, 4, N_KV_A, HEAD_DIM, tp).transpose(0, 4, 1, 2, 3))
            win_p.append(wrt.reshape(bp, 2, N_KV_A, HEAD_DIM, tp)[..., tp - min(WINDOW, tp):]
                         .transpose(0, 4, 1, 2, 3))
            rh_p.append(h_last.reshape(bp, RG_WIDTH))
            rc_p.append(new_buf)

            q, kv4, wr, rx, ry, gl = _win_call(xs, ms0, w_in, cos_s, sin_s, n_s, 1)
            cmp = _compress_sample(pt_flat, cache_fm, i, bs, npages, w4, pe4, ccos_s, csin_s)
            o = _sattn(pt_flat, cache_fm, i, q, gl, cmp, kv4, wr, win_fm, bs, ts, npages)
            st = jnp.concatenate([state_rglru_h[i][:, None, :],
                                  jnp.zeros((bs, RG_STATE_ROWS - RG_CONV, RG_WIDTH), F32),
                                  state_rglru_conv[i]], axis=1)
            rg, h_last, new_buf = _rglru(rx, ry, st, *rg_args, bs, ts)
            xs = _proj_ln([o, rg], [wo_a, wo_r], xs, ms0, zero_bias, g0, b0, n_s, 1)
            kv_s.append(kv4.reshape(bs, ts, 4, N_KV_A, HEAD_DIM))
            wfull = jnp.concatenate([state_nsa_win[i], wr.reshape(bs, ts, 2, N_KV_A, HEAD_DIM)], axis=1)
            win_s.append(wfull[:, wfull.shape[1] - min(WINDOW, wfull.shape[1]):])
            rh_s.append(h_last.reshape(bs, RG_WIDTH))
            rc_s.append(new_buf)

            w1, w3, w2 = ff_w1[i].astype(BF16), ff_w3[i].astype(BF16), ff_w2[i].astype(BF16)
            xp = _ffn(xp, mp1, w1, w3, w2, g1, b1, ROW_TILE, tps_p)
            xs = _ffn(xs, ms1, w1, w3, w2, g1, b1, n_s, 1)
        else:
            cw1 = cf_w1[i].astype(BF16)
            cw2 = cf_w2[i].astype(BF16)
            dw = jnp.broadcast_to(cf_dw[i][:, None, :], (CF_KERNEL, SUBLANES, d))
            conv_args = (dw, vec(cf_db[i]), vec(cf_ln_g[i]), vec(cf_ln_b[i]))

            glu = _cf_in(xp, mp0, cw1, vec(cf_b1[i]), ROW_TILE, tps_p)
            per = CONV_TILE // CF_HALO
            z = _cf_conv(glu, glu, lambda bi, ti: (jnp.maximum((bi * (tp // CONV_TILE) + ti) * per - 1, 0), 0),
                         *conv_args, bp, tp, CONV_TILE, True)
            xp = _proj_ln([z], [cw2], xp, mp0, vec(cf_b2[i]), g0, b0, ROW_TILE, tps_p)
            cc_p.append(glu.reshape(bp, tp, d)[:, tp - (CF_KERNEL - 1):])

            glu = _cf_in(xs, ms0, cw1, vec(cf_b1[i]), n_s, 1)
            halo = jnp.concatenate([jnp.zeros((bs, CF_HALO - (CF_KERNEL - 1), d), F32),
                                    state_conformer_conv[i]], axis=1).reshape(bs * CF_HALO, d)
            z = _cf_conv(glu, halo, lambda bi, ti: (bi, 0), *conv_args, bs, ts, ts, False)
            xs = _proj_ln([z], [cw2], xs, ms0, vec(cf_b2[i]), g0, b0, n_s, 1)
            cc_s.append(jnp.concatenate([state_conformer_conv[i], glu.reshape(bs, ts, d)],
                                        axis=1)[:, -(CF_KERNEL - 1):])

            rw = jnp.concatenate([moe_router[i], jnp.zeros((d, LANES - N_EXPERTS), F32)], axis=1)
            h_all, ids_p, gates_p = _router(xp, mp1, rw, ROW_TILE, tps_p, n_tok, 0, None)
            h_all, ids_s, gates_s = _router(xs, ms1, rw, n_s, 1, n_tok, n_p, h_all)
            plan, blk_e, cnt, nused, n_blk = _route_plan(jnp.concatenate([ids_p, ids_s], axis=0), n_tok)
            ys = _experts(blk_e, plan, cnt, nused, h_all, n_tok, moe_w1_bf, moe_w3_bf, moe_w2_bf, i, n_blk)
            xp = _combine(ys, gates_p, xp, mp1, g1, b1, tm_c, tp // tm_c, 0, n_tok)
            xs = _combine(ys, gates_s, xs, ms1, g1, b1, tm_c, 1, n_p, n_tok)
    return (xp.reshape(bp, tp, d), xs.reshape(bs, ts, d), jnp.stack(kv_p), jnp.stack(kv_s),
            jnp.stack(win_p), jnp.stack(win_s), jnp.stack(rh_p), jnp.stack(rh_s),
            jnp.stack(rc_p), jnp.stack(rc_s), jnp.stack(cc_p), jnp.stack(cc_s))
```

```python
import functools

import jax
import jax.numpy as jnp
from jax import lax
from jax.experimental import pallas as pl
from jax.experimental.pallas import tpu as pltpu

F32 = jnp.float32
BF16 = jnp.bfloat16
I32 = jnp.int32

D_MODEL = 1024
HEAD_DIM = 64
N_HEADS_A = 8
N_KV_A = 2
HEADS_PER_GROUP = N_HEADS_A // N_KV_A
GROUP_COLS = N_KV_A * HEAD_DIM
CMP_BLOCK = 32
SEL_BLOCK = 64
TOP_N = 16
WINDOW = 512
ROPE_THETA = 10000.0
PAGE_SIZE = 128
RG_WIDTH = 512
RG_BLOCKS = 8
RG_CONV = 4
RG_C = 8.0
NSA_WIDTH = 512
Q_COLS = NSA_WIDTH
KV_COLS = 6 * GROUP_COLS
GATE_COLS = 3 * N_HEADS_A
CF_KERNEL = 31
CF_HALO = 32
N_EXPERTS = 8
DEPTH = 4
ALPHA = (2.0 * DEPTH) ** 0.25
LN_EPS = 1e-5
NEG = -1e30
FORCE_SCORE = 1e4
ATTN_SCALE = HEAD_DIM ** -0.5

LANES = 128
SUBLANES = 8
VMEM_LIMIT = 56 * 1024 * 1024

ROW_TILE = 512
Q_TILE = 256
CONV_TILE = 256
CONV_SUB = 32
MOE_TILE = 512
FF_CHUNK = 256
MOE_FF_CHUNK = 896
MOE_STEPS = 4


def _cparams(n_axes, **kw):
    return pltpu.CompilerParams(dimension_semantics=("arbitrary",) * n_axes,
                                vmem_limit_bytes=VMEM_LIMIT, **kw)


def _const_spec(shape, single_buffer=False):
    n = len(shape)
    if single_buffer:
        return pl.BlockSpec(shape, lambda *a: (0,) * n, pipeline_mode=pl.Buffered(1))
    return pl.BlockSpec(shape, lambda *a: (0,) * n)


def _mod_spec(mod3, tm, tps):
    if mod3.shape[1] == 1:
        return pl.BlockSpec((1, 1, mod3.shape[2]), lambda i: (i // tps, 0, 0))
    return pl.BlockSpec((1, tm, mod3.shape[2]), lambda i: (0, i, 0))


def _dot(a, b):
    return jnp.dot(a, b, preferred_element_type=F32)


def _dot_nt(a, b):
    return lax.dot_general(a, b, (((1,), (1,)), ((), ())), preferred_element_type=F32)


def _layer_norm(y, g, b):
    mu = jnp.mean(y, axis=-1, keepdims=True)
    yc = y - mu
    var = jnp.mean(yc * yc, axis=-1, keepdims=True)
    return yc * lax.rsqrt(var + LN_EPS) * g + b


def _post_norm(x, mod, out, g, b):
    gate = mod[:, 2 * D_MODEL:]
    return _layer_norm(ALPHA * x + (1.0 + gate) * out, g, b)


def _modulate(x, mod):
    return x * (1.0 + mod[:, D_MODEL:2 * D_MODEL]) + mod[:, :D_MODEL]


def _silu(x):
    return x * jax.nn.sigmoid(x)


def _rope128(v, cos, sin_signed):
    lane = lax.broadcasted_iota(I32, v.shape, 1)
    from_hi = pltpu.roll(v, LANES - HEAD_DIM // 2, 1)
    from_lo = pltpu.roll(v, HEAD_DIM // 2, 1)
    swapped = jnp.where((lane & (HEAD_DIM - 1)) < HEAD_DIM // 2, from_hi, from_lo)
    return v * cos + swapped * sin_signed


def _rope(v, cos, sin_signed):
    k = v.shape[1] // LANES
    parts = [_rope128(v[:, i * LANES:(i + 1) * LANES], cos, sin_signed) for i in range(k)]
    return parts[0] if k == 1 else jnp.concatenate(parts, axis=1)


def _softmax_parts(parts):
    masked = [jnp.where(m, s, NEG) for s, m in parts]
    mx = functools.reduce(jnp.maximum, [jnp.max(s, axis=-1, keepdims=True) for s in masked])
    es = [jnp.where(m, jnp.exp(s - mx), 0.0) for s, (_, m) in zip(masked, parts)]
    den = functools.reduce(jnp.add, [jnp.sum(e, axis=-1, keepdims=True) for e in es])
    return es, 1.0 / jnp.maximum(den, 1e-30)


def _attend(qb, pieces):
    es, inv = _softmax_parts([(_dot(qb, k) if fm else _dot_nt(qb, k), m) for k, _, m, fm in pieces])
    o = functools.reduce(jnp.add, [_dot_nt(e.astype(BF16), v) if fm else _dot(e.astype(BF16), v)
                                   for e, (_, v, _, fm) in zip(es, pieces)])
    return o * inv


def _topk_mask(vals, n_valid, kk):
    lane = lax.broadcasted_iota(I32, vals.shape, 1)
    rank = jnp.zeros(vals.shape, I32)
    for m in range(n_valid):
        col = vals[:, m:m + 1]
        later = jnp.where(lane > m, 1, 0)
        rank = rank + jnp.where(col > vals, 1, jnp.where(col == vals, later, 0))
    return (rank < kk) & (lane < n_valid)


def _mod_kernel(c_ref, w_ref, b_ref, o_ref):
    s = _silu(c_ref[...]).astype(BF16)
    o_ref[0] = _dot(s, w_ref[0].astype(BF16)) + b_ref[0]


def _mod_all(c_all, w_mod, b_mod):
    n = c_all.shape[0]
    nl = w_mod.shape[0] * w_mod.shape[1]
    w = w_mod.reshape(nl, D_MODEL, 3 * D_MODEL)
    b = b_mod.reshape(nl, 1, 3 * D_MODEL)
    return pl.pallas_call(
        _mod_kernel, grid=(nl, 3),
        in_specs=[pl.BlockSpec((n, D_MODEL), lambda l, j: (0, 0)),
                  pl.BlockSpec((1, D_MODEL, D_MODEL), lambda l, j: (l, 0, j)),
                  pl.BlockSpec((1, 1, D_MODEL), lambda l, j: (l, 0, j))],
        out_specs=pl.BlockSpec((1, n, D_MODEL), lambda l, j: (l, 0, j)),
        out_shape=jax.ShapeDtypeStruct((nl, n, 3 * D_MODEL), F32),
        compiler_params=_cparams(2), name="mod_all")(c_all, w, b)


_O_KV = Q_COLS
_O_RX = _O_KV + KV_COLS
_O_RY = _O_RX + RG_WIDTH
_O_GL = _O_RY + RG_WIDTH
W_IN_COLS = _O_GL + LANES


def _win_kernel(x_ref, mod_ref, w_ref, cos_ref, sin_ref, q_ref, kv_ref, wr_ref, rx_ref, ry_ref, gl_ref,
                *fm_refs):
    h = _modulate(x_ref[...], mod_ref[0]).astype(BF16)
    u = _dot(h, w_ref[...])
    cos = cos_ref[...]
    sin = sin_ref[...]
    q_ref[...] = _rope(u[:, :Q_COLS], cos, sin)
    c = _O_KV
    kv = jnp.concatenate([u[:, c:c + 2 * LANES],
                          _rope128(u[:, c + 2 * LANES:c + 3 * LANES], cos, sin),
                          u[:, c + 3 * LANES:c + 4 * LANES]], axis=1)
    wr = jnp.concatenate([_rope128(u[:, c + 4 * LANES:c + 5 * LANES], cos, sin),
                          u[:, c + 5 * LANES:c + 6 * LANES]], axis=1)
    kv_ref[...] = kv
    wr_ref[...] = wr
    rx_ref[...] = u[:, _O_RX:_O_RY]
    ry_ref[...] = u[:, _O_RY:_O_GL]
    gl_ref[...] = u[:, _O_GL:]
    if fm_refs:
        kvt_ref, wrt_ref = fm_refs
        kvt_ref[...] = kv.T
        wrt_ref[...] = wr.T


def _win_call(x2d, mod3, w, cos, sin, tm, tps, t_len=None):
    m = x2d.shape[0]
    row = lambda width: pl.BlockSpec((tm, width), lambda i: (i, 0))
    widths = (Q_COLS, 4 * LANES, 2 * LANES, RG_WIDTH, RG_WIDTH, LANES)
    out_specs = [row(wd) for wd in widths]
    out_shape = [jax.ShapeDtypeStruct((m, wd), F32) for wd in widths]
    if t_len is not None:
        for feat in (4 * LANES, 2 * LANES):
            out_specs.append(pl.BlockSpec((feat, tm), lambda i: (i // tps, i % tps)))
            out_shape.append(jax.ShapeDtypeStruct((m // t_len * feat, t_len), F32))
    return pl.pallas_call(
        _win_kernel, grid=(m // tm,),
        in_specs=[row(D_MODEL), _mod_spec(mod3, tm, tps), _const_spec(w.shape),
                  pl.BlockSpec((tm, LANES), lambda i: (i % tps, 0)),
                  pl.BlockSpec((tm, LANES), lambda i: (i % tps, 0))],
        out_specs=out_specs, out_shape=out_shape,
        compiler_params=_cparams(1), name="mixer_in")(x2d, mod3, w, cos, sin)


def _compress_core(k_ref, v_ref, w4_ref, pe4_ref, cos_ref, sin_ref, out_ref, nb):
    half = nb // 2
    acc = jnp.zeros((nb, 2 * LANES), F32)
    for l in range(CMP_BLOCK):
        even = pl.ds(l, half, stride=2 * CMP_BLOCK)
        odd = pl.ds(CMP_BLOCK + l, half, stride=2 * CMP_BLOCK)
        xk = jnp.concatenate([k_ref[even, :], k_ref[odd, :]], axis=0)
        xv = jnp.concatenate([v_ref[even, :], v_ref[odd, :]], axis=0)
        x = (jnp.concatenate([xk, xv], axis=1) + pe4_ref[pl.ds(l, 1), :]).astype(BF16)
        acc = acc + _dot(x, w4_ref[l])
    out_ref[:, 0:LANES] = _rope128(acc[:, 0:LANES], cos_ref[...], sin_ref[...])
    out_ref[:, LANES:2 * LANES] = acc[:, LANES:2 * LANES]


def _compress_prompt_kernel(k_ref, v_ref, w4_ref, pe4_ref, cos_ref, sin_ref, out_ref, *, nb):
    _compress_core(k_ref, v_ref, w4_ref, pe4_ref, cos_ref, sin_ref, out_ref, nb)


def _compress_prompt(kv4, bsz, t, w4, pe4, ccos, csin):
    nb = t // CMP_BLOCK
    return pl.pallas_call(
        functools.partial(_compress_prompt_kernel, nb=nb), grid=(bsz,),
        in_specs=[pl.BlockSpec((t, LANES), lambda b: (b, 0)), pl.BlockSpec((t, LANES), lambda b: (b, 1)),
                  _const_spec(w4.shape), _const_spec(pe4.shape),
                  _const_spec(ccos.shape), _const_spec(csin.shape)],
        out_specs=pl.BlockSpec((nb, 2 * LANES), lambda b: (b, 0)),
        out_shape=jax.ShapeDtypeStruct((bsz * nb, 2 * LANES), F32),
        compiler_params=_cparams(1), name="compress_prompt")(kv4, kv4, w4, pe4, ccos, csin)


def _prefetch_pages(pt_ref, cache_ref, sems, layer, feat0, npages, dst_fn):
    b = pl.program_id(0)

    def copy(seq, j):
        slot = seq % 2
        src = cache_ref.at[layer, pt_ref[seq * npages + j], pl.ds(feat0, 2 * LANES), :]
        return pltpu.make_async_copy(src, dst_fn(slot, j), sems.at[slot])

    def start_seq(seq):
        def body(j, c):
            copy(seq, j).start()
            return c
        lax.fori_loop(0, npages, body, 0)

    @pl.when(b == 0)
    def _():
        start_seq(b)

    @pl.when(b + 1 < pl.num_programs(0))
    def _():
        start_seq(b + 1)

    def wait(j, c):
        copy(b, j).wait()
        return c
    lax.fori_loop(0, npages, wait, 0)
    return b % 2


def _compress_sample_kernel(pt_ref, cache_ref, w4_ref, pe4_ref, cos_ref, sin_ref, out_ref, raw, kbuf, vbuf, sems,
                            *, layer, npages):
    def dst(slot, j):
        return raw.at[slot, pl.ds(pl.multiple_of(j * 2 * LANES, 2 * LANES), 2 * LANES), :]
    slot = _prefetch_pages(pt_ref, cache_ref, sems, layer, 0, npages, dst)

    def to_token_major(j, c):
        r0 = pl.multiple_of(j * 2 * LANES, 2 * LANES)
        t0 = pl.multiple_of(j * PAGE_SIZE, PAGE_SIZE)
        kbuf[pl.ds(t0, PAGE_SIZE), :] = raw[slot, pl.ds(r0, LANES), :].T
        vbuf[pl.ds(t0, PAGE_SIZE), :] = raw[slot, pl.ds(r0 + LANES, LANES), :].T
        return c
    lax.fori_loop(0, npages, to_token_major, 0, unroll=4)
    _compress_core(kbuf, vbuf, w4_ref, pe4_ref, cos_ref, sin_ref, out_ref, npages * PAGE_SIZE // CMP_BLOCK)


def _compress_sample(pt_flat, cache_fm, layer, bsz, npages, w4, pe4, ccos, csin):
    past = npages * PAGE_SIZE
    nb = past // CMP_BLOCK
    grid_spec = pltpu.PrefetchScalarGridSpec(
        num_scalar_prefetch=1, grid=(bsz,),
        in_specs=[pl.BlockSpec(memory_space=pl.ANY),
                  _const_spec(w4.shape), _const_spec(pe4.shape),
                  _const_spec(ccos.shape), _const_spec(csin.shape)],
        out_specs=pl.BlockSpec((nb, 2 * LANES), lambda b, pt: (b, 0)),
        scratch_shapes=[pltpu.VMEM((2, npages * 2 * LANES, PAGE_SIZE), F32),
                        pltpu.VMEM((past, LANES), F32), pltpu.VMEM((past, LANES), F32),
                        pltpu.SemaphoreType.DMA((2,))])
    return pl.pallas_call(
        functools.partial(_compress_sample_kernel, layer=layer, npages=npages),
        grid_spec=grid_spec,
        out_shape=jax.ShapeDtypeStruct((bsz * nb, 2 * LANES), F32),
        compiler_params=_cparams(1, disable_bounds_checks=True),
        name="compress_sample")(pt_flat, cache_fm, w4, pe4, ccos, csin)


def _cmp_positions(nc):
    n = lax.broadcasted_iota(I32, (1, nc), 1)
    half = nc // 2
    blk = jnp.where(n < half, 2 * n, 2 * (n - half) + 1)
    return blk * CMP_BLOCK + (CMP_BLOCK - 1)


def _softmax_cols(s, mask):
    s = jnp.where(mask, s, NEG)
    e = jnp.where(mask, jnp.exp(s - jnp.max(s, axis=0, keepdims=True)), 0.0)
    return e, 1.0 / jnp.maximum(jnp.sum(e, axis=0, keepdims=True), 1e-30)


def _pattn_kernel(q_ref, gl_ref, cmp_ref, ks_ref, vst_ref, kw0_ref, kw1_ref, kw2_ref, vwt0_ref, vwt1_ref,
                  vwt2_ref, o_ref, ks_bf, vst_bf, bias_ref, acc_ref, ot_ref, *, tq, t_len):
    ti = pl.program_id(1)
    nc = t_len // CMP_BLOCK
    nsel = t_len // SEL_BLOCK
    half = nc // 2
    blk_per_chunk = tq // SEL_BLOCK
    q0 = ti * tq

    @pl.when(ti == 0)
    def _():
        for g in range(N_KV_A):
            ks_bf[g] = ks_ref[:, g * HEAD_DIM:(g + 1) * HEAD_DIM].astype(BF16)
        vst_bf[...] = vst_ref[...].astype(BF16)

    qj = lax.broadcasted_iota(I32, (1, tq), 1)
    ki = lax.broadcasted_iota(I32, (tq, 1), 0)
    qpos = q0 + qj
    qs = (q_ref[...] * ATTN_SCALE).astype(BF16)
    gates_t = jax.nn.sigmoid(gl_ref[...]).T
    n = lax.broadcasted_iota(I32, (nc, 1), 0)
    cpos = jnp.where(n < half, 2 * n, 2 * (n - half) + 1) * CMP_BLOCK + (CMP_BLOCK - 1)
    m_c = cpos <= qpos
    blk = lax.broadcasted_iota(I32, (nsel, 1), 0)
    cur = qpos // SEL_BLOCK
    forced = (blk == cur) | (blk == 0)
    future = blk > cur
    later_blk = [jnp.where(blk > m, 1, 0) for m in range(nsel)]
    diag_bias = jnp.where(ki <= qj, 0.0, NEG)
    w_masks = []
    for k in range(3):
        dpos = qj - ki + (2 - k) * tq
        w_masks.append((dpos >= 0) & (dpos < WINDOW) & (ti + k - 2 >= 0))
    w_bias = jnp.where(jnp.concatenate(w_masks, axis=0), 0.0, NEG)
    kw_refs = (kw0_ref, kw1_ref, kw2_ref)
    vwt_refs = (vwt0_ref, vwt1_ref, vwt2_ref)

    for g in range(N_KV_A):
        ck = slice(g * HEAD_DIM, (g + 1) * HEAD_DIM)
        kc = cmp_ref[:, ck].astype(BF16)
        vc = cmp_ref[:, LANES + g * HEAD_DIM:LANES + (g + 1) * HEAD_DIM].astype(BF16)
        heads = [g * HEADS_PER_GROUP + r for r in range(HEADS_PER_GROUP)]
        q_heads = [qs[:, h * HEAD_DIM:(h + 1) * HEAD_DIM] for h in heads]
        o_cmp = []
        imp = jnp.zeros((nc, tq), F32)
        for s in [_dot_nt(kc, qh) for qh in q_heads]:
            e, inv = _softmax_cols(s, m_c)
            p = e * inv
            o_cmp.append(lax.dot_general(vc, p.astype(BF16), (((0,), (0,)), ((), ())),
                                         preferred_element_type=F32))
            imp = imp + p
        imp = imp[:half] + imp[half:]
        vals = jnp.where(forced, FORCE_SCORE, jnp.where(future, -1.0, imp))
        rank = jnp.zeros((nsel, tq), I32)
        for m in range(nsel):
            row = vals[m:m + 1, :]
            rank = rank + jnp.where(row > vals, 1, jnp.where(row == vals, later_blk[m], 0))
        bias_ref[g] = jnp.where(rank < min(TOP_N, nsel), 0.0, NEG)

        kw = jnp.concatenate([r[:, ck] for r in kw_refs], axis=0).astype(BF16)
        vwt = [r[ck, :].astype(BF16) for r in vwt_refs]

        def block_bias(c):
            rows = [jnp.broadcast_to(bias_ref[g, pl.ds(c * blk_per_chunk + j, 1), :], (SEL_BLOCK, tq))
                    for j in range(blk_per_chunk)]
            return jnp.concatenate(rows, axis=0)

        def chunk(c, carry, extra_bias=None):
            r0 = pl.multiple_of(c * tq, tq)
            k_chunk = ks_bf[g, pl.ds(r0, tq), :]
            v_chunk = vst_bf[ck, pl.ds(r0, tq)]
            bias = block_bias(c)
            if extra_bias is not None:
                bias = bias + extra_bias
            scores = [_dot_nt(k_chunk, qh) for qh in q_heads]
            out, probs, alphas = [], [], []
            for r in range(HEADS_PER_GROUP):
                m_run, l_run = carry[r]
                s = scores[r] + bias
                m_new = jnp.maximum(m_run, jnp.max(s, axis=0, keepdims=True))
                alpha = jnp.exp(m_run - m_new)
                p = jnp.exp(s - m_new)
                out.append((m_new, alpha * l_run + jnp.sum(p, axis=0, keepdims=True)))
                probs.append(p.astype(BF16))
                alphas.append(alpha)
            for r in range(HEADS_PER_GROUP):
                acc_ref[r] = alphas[r] * acc_ref[r] + _dot(v_chunk, probs[r])
            return tuple(out)

        acc_ref[...] = jnp.zeros(acc_ref.shape, F32)
        init = tuple((jnp.full((1, tq), -jnp.inf, F32), jnp.zeros((1, tq), F32)) for _ in heads)
        stats = chunk(ti, lax.fori_loop(0, ti, chunk, init), diag_bias)

        w_scores = [_dot_nt(kw, qh) + w_bias for qh in q_heads]
        w_probs = [jnp.exp(s - jnp.max(s, axis=0, keepdims=True)) for s in w_scores]
        for r, h in enumerate(heads):
            o_s = acc_ref[r] * (1.0 / jnp.maximum(stats[r][1], 1e-30))
            e = w_probs[r]
            inv = 1.0 / jnp.maximum(jnp.sum(e, axis=0, keepdims=True), 1e-30)
            e = e.astype(BF16)
            o_w = functools.reduce(jnp.add, [_dot(vwt[k], e[k * tq:(k + 1) * tq]) for k in range(3)]) * inv
            ot_ref[h * HEAD_DIM:(h + 1) * HEAD_DIM, :] = (
                gates_t[3 * h:3 * h + 1, :] * o_cmp[r] + gates_t[3 * h + 1:3 * h + 2, :] * o_s
                + gates_t[3 * h + 2:3 * h + 3, :] * o_w)
    o_ref[...] = ot_ref[...].T


def _block_expand(nblk, nkeys):
    return (jnp.arange(nkeys)[None, :] // SEL_BLOCK == jnp.arange(nblk)[:, None]).astype(BF16)


def _pattn(q, gl, cmp, kv4, kvt, wr, wrt, bsz, t):
    tq = Q_TILE
    nt = t // tq
    nc = t // CMP_BLOCK
    nsel = t // SEL_BLOCK
    row = lambda width: pl.BlockSpec((tq, width), lambda b, i: (b * nt + i, 0))
    kw = lambda back: pl.BlockSpec((tq, LANES), lambda b, i: (b * nt + jnp.maximum(i - back, 0), 0))
    vwt = lambda back: pl.BlockSpec((LANES, tq), lambda b, i: (b * 2 + 1, jnp.maximum(i - back, 0)))
    return pl.pallas_call(
        functools.partial(_pattn_kernel, tq=tq, t_len=t), grid=(bsz, nt),
        in_specs=[row(Q_COLS), row(LANES),
                  pl.BlockSpec((nc, 2 * LANES), lambda b, i: (b, 0)),
                  pl.BlockSpec((t, LANES), lambda b, i: (b, 2)),
                  pl.BlockSpec((LANES, t), lambda b, i: (b * 4 + 3, 0)),
                  kw(2), kw(1), kw(0), vwt(2), vwt(1), vwt(0)],
        out_specs=row(NSA_WIDTH),
        out_shape=jax.ShapeDtypeStruct((bsz * t, NSA_WIDTH), F32),
        scratch_shapes=[pltpu.VMEM((N_KV_A, t, HEAD_DIM), BF16), pltpu.VMEM((LANES, t), BF16),
                        pltpu.VMEM((N_KV_A, nsel, tq), F32), pltpu.VMEM((HEADS_PER_GROUP, HEAD_DIM, tq), F32),
                        pltpu.VMEM((NSA_WIDTH, tq), F32)],
        compiler_params=_cparams(2), name="nsa_prompt")(q, gl, cmp, kv4, kvt, wr, wr, wr, wrt, wrt, wrt)


def _sattn_kernel(pt_ref, cache_ref, q_ref, gl_ref, cmp_ref, kvn_ref, wrn_ref, win_ref, e_ref, o_ref,
                  buf, sems, *, layer, npages, ts):
    past = npages * PAGE_SIZE

    def dst(slot, j):
        return buf.at[slot, :, pl.ds(pl.multiple_of(j * PAGE_SIZE, PAGE_SIZE), PAGE_SIZE)]
    slot = _prefetch_pages(pt_ref, cache_ref, sems, layer, 2 * LANES, npages, dst)
    nc = past // CMP_BLOCK
    nblk_past = past // SEL_BLOCK
    nsel = nblk_past + 1
    lanes_sel = 2 * nblk_past
    rows = HEADS_PER_GROUP * ts
    tok1 = lax.broadcasted_iota(I32, (ts, 1), 0)
    tok = jnp.concatenate([tok1] * HEADS_PER_GROUP, axis=0)
    qpos1 = past + tok1
    qpos = past + tok
    qs = q_ref[...] * ATTN_SCALE
    gates = jax.nn.sigmoid(gl_ref[...])
    m_c = _cmp_positions(nc) <= qpos
    blk = lax.broadcasted_iota(I32, (1, lanes_sel), 1)
    cur = qpos1 // SEL_BLOCK
    forced = (blk == cur) | (blk == 0)
    future = blk > cur
    m_past = lax.broadcasted_iota(I32, (1, past), 1) <= qpos
    tkey = lax.broadcasted_iota(I32, (1, ts), 1)
    m_new = tkey <= tok
    win_len = win_ref.shape[1]
    wpos = past - win_len + lax.broadcasted_iota(I32, (1, win_len), 1)
    dpos = qpos - wpos
    m_wstate = (dpos >= 0) & (dpos < WINDOW) & (wpos >= 0)
    dnew = tok - tkey
    m_wnew = (dnew >= 0) & (dnew < WINDOW)
    for g in range(N_KV_A):
        ck = slice(g * HEAD_DIM, (g + 1) * HEAD_DIM)
        cv = slice(LANES + g * HEAD_DIM, LANES + (g + 1) * HEAD_DIM)
        heads = [g * HEADS_PER_GROUP + r for r in range(HEADS_PER_GROUP)]
        qg = jnp.concatenate([qs[:, h * HEAD_DIM:(h + 1) * HEAD_DIM] for h in heads], axis=0).astype(BF16)
        kc = cmp_ref[:, ck].astype(BF16)
        vc = cmp_ref[:, cv].astype(BF16)
        (e,), inv = _softmax_parts([(_dot_nt(qg, kc), m_c)])
        p = e * inv
        o_c = _dot(p.astype(BF16), vc)
        imp = p[0:ts]
        for r in range(1, HEADS_PER_GROUP):
            imp = imp + p[r * ts:(r + 1) * ts]
        imp = imp[:, :nc // 2] + imp[:, nc // 2:]
        imp = jnp.concatenate([imp, jnp.zeros((ts, lanes_sel - nc // 2), F32)], axis=1)
        vals = jnp.where(forced, FORCE_SCORE, jnp.where(future, -1.0, imp))
        sel = jnp.where(_topk_mask(vals, nsel, min(TOP_N, nsel)), 1.0, 0.0)
        sel = jnp.concatenate([sel] * HEADS_PER_GROUP, axis=0)
        sel_keys = _dot(sel[:, :nblk_past].astype(BF16), e_ref[...])
        m_s_past = (sel_keys > 0.5) & m_past
        m_s_new = (sel[:, nblk_past:nblk_past + 1] > 0.5) & m_new
        kn = kvn_ref[:, 2 * LANES:4 * LANES]
        o_s = _attend(qg, [(buf[slot, ck, :].astype(BF16), buf[slot, cv, :].astype(BF16), m_s_past, True),
                           (kn[:, ck].astype(BF16), kn[:, cv].astype(BF16), m_s_new, False)])
        o_w = _attend(qg, [(win_ref[ck, :].astype(BF16), win_ref[cv, :].astype(BF16), m_wstate, True),
                           (wrn_ref[:, ck].astype(BF16), wrn_ref[:, cv].astype(BF16), m_wnew, False)])
        for r, h in enumerate(heads):
            rs = slice(r * ts, (r + 1) * ts)
            o_ref[:, h * HEAD_DIM:(h + 1) * HEAD_DIM] = (
                gates[:, 3 * h:3 * h + 1] * o_c[rs] + gates[:, 3 * h + 1:3 * h + 2] * o_s[rs]
                + gates[:, 3 * h + 2:3 * h + 3] * o_w[rs])


def _sattn(pt_flat, cache_fm, layer, q, gl, cmp, kv4, wr, win_fm, bsz, ts, npages):
    past = npages * PAGE_SIZE
    nc = past // CMP_BLOCK
    win_len = win_fm.shape[3]
    expand = _block_expand(past // SEL_BLOCK, past)
    row = lambda width: pl.BlockSpec((ts, width), lambda b, pt: (b, 0))
    grid_spec = pltpu.PrefetchScalarGridSpec(
        num_scalar_prefetch=1, grid=(bsz,),
        in_specs=[pl.BlockSpec(memory_space=pl.ANY), row(Q_COLS), row(LANES),
                  pl.BlockSpec((nc, 2 * LANES), lambda b, pt: (b, 0)),
                  row(4 * LANES), row(2 * LANES),
                  pl.BlockSpec((None, None, 2 * LANES, win_len), lambda b, pt: (layer, b, 0, 0)),
                  _const_spec(expand.shape)],
        out_specs=row(NSA_WIDTH),
        scratch_shapes=[pltpu.VMEM((2, 2 * LANES, past), F32), pltpu.SemaphoreType.DMA((2,))])
    return pl.pallas_call(
        functools.partial(_sattn_kernel, layer=layer, npages=npages, ts=ts),
        grid_spec=grid_spec,
        out_shape=jax.ShapeDtypeStruct((bsz * ts, NSA_WIDTH), F32),
        compiler_params=_cparams(1, disable_bounds_checks=True),
        name="nsa_sample")(pt_flat, cache_fm, q, gl, cmp, kv4, wr, win_fm, expand)


RG_STATE_ROWS = SUBLANES
RG_ROWS = 256


def _rglru_kernel(rx_ref, ry_ref, st_ref, cw_ref, cb_ref, wa_ref, ba_ref, wx_ref, bx_ref, lam_ref,
                  out_ref, hl_ref, nb_ref, xs, a_s, b_s, *, t_len):
    ch = min(RG_ROWS, t_len)
    xs[0:RG_STATE_ROWS, :] = st_ref[0]
    xs[RG_STATE_ROWS:RG_STATE_ROWS + t_len, :] = rx_ref[...]
    lam = lam_ref[...]
    softplus_neg_lam = jnp.maximum(-lam, 0.0) + jnp.log1p(jnp.exp(-jnp.abs(lam)))
    sub = lax.broadcasted_iota(I32, (ch, RG_WIDTH), 0) & (SUBLANES - 1)
    for c in range(t_len // ch):
        r0 = c * ch
        xc = cb_ref[...]
        for j in range(RG_CONV):
            xc = xc + cw_ref[j:j + 1, :] * xs[r0 + RG_STATE_ROWS - (RG_CONV - 1) + j:
                                              r0 + RG_STATE_ROWS - (RG_CONV - 1) + j + ch, :]
        xb = xc.astype(BF16)
        r = jax.nn.sigmoid(_dot(xb, wa_ref[...]) + ba_ref[...])
        i = jax.nn.sigmoid(_dot(xb, wx_ref[...]) + bx_ref[...])
        log_a = -RG_C * r * softplus_neg_lam
        a = jnp.exp(log_a)
        one_minus_a2 = -jnp.tanh(log_a) * (jnp.exp(2.0 * log_a) + 1.0)
        bb = jnp.sqrt(one_minus_a2) * (i * xc)
        for s in (1, 2, 4):
            ok = sub >= s
            a_prev = pltpu.roll(a, s, 0)
            b_prev = pltpu.roll(bb, s, 0)
            bb = jnp.where(ok, a * b_prev + bb, bb)
            a = jnp.where(ok, a * a_prev, a)
        a_s[r0:r0 + ch, :] = a
        b_s[r0:r0 + ch, :] = bb

    def step(k, h):
        r0 = pl.multiple_of(k * SUBLANES, SUBLANES)
        hk = b_s[pl.ds(r0, SUBLANES), :] + a_s[pl.ds(r0, SUBLANES), :] * h
        b_s[pl.ds(r0, SUBLANES), :] = hk
        return jnp.broadcast_to(hk[SUBLANES - 1:SUBLANES, :], (SUBLANES, RG_WIDTH))

    h0 = jnp.broadcast_to(st_ref[0, 0:1, :], (SUBLANES, RG_WIDTH))
    h_fin = lax.fori_loop(0, t_len // SUBLANES, step, h0)
    hl_ref[0] = h_fin[0:1, :]
    nb_ref[0] = xs[RG_STATE_ROWS + t_len - (RG_CONV - 1):RG_STATE_ROWS + t_len, :]
    for c in range(t_len // ch):
        r0 = c * ch
        out_ref[r0:r0 + ch, :] = b_s[r0:r0 + ch, :] * jax.nn.gelu(ry_ref[r0:r0 + ch, :])


def _rglru(rx, ry, st, cw, cb, wa, ba, wx, bx, lam, bsz, t):
    row = pl.BlockSpec((t, RG_WIDTH), lambda b: (b, 0))
    vec = _const_spec((1, RG_WIDTH))
    return pl.pallas_call(
        functools.partial(_rglru_kernel, t_len=t), grid=(bsz,),
        in_specs=[row, row, pl.BlockSpec((1, RG_STATE_ROWS, RG_WIDTH), lambda b: (b, 0, 0)),
                  _const_spec(cw.shape), vec, _const_spec(wa.shape), vec, _const_spec(wx.shape), vec, vec],
        out_specs=[row, pl.BlockSpec((1, 1, RG_WIDTH), lambda b: (b, 0, 0)),
                   pl.BlockSpec((1, RG_CONV - 1, RG_WIDTH), lambda b: (b, 0, 0))],
        out_shape=[jax.ShapeDtypeStruct((bsz * t, RG_WIDTH), F32),
                   jax.ShapeDtypeStruct((bsz, 1, RG_WIDTH), F32),
                   jax.ShapeDtypeStruct((bsz, RG_CONV - 1, RG_WIDTH), F32)],
        scratch_shapes=[pltpu.VMEM((RG_STATE_ROWS + t, RG_WIDTH), F32),
                        pltpu.VMEM((t, RG_WIDTH), F32), pltpu.VMEM((t, RG_WIDTH), F32)],
        compiler_params=_cparams(1), name="rglru")(rx, ry, st, cw, cb, wa, ba, wx, bx, lam)


def _proj_ln_kernel(*refs, n_in):
    a_refs = refs[:n_in]
    w_refs = refs[n_in:2 * n_in]
    x_ref, mod_ref, bias_ref, g_ref, b_ref, o_ref = refs[2 * n_in:]
    out = bias_ref[...]
    for a_ref, w_ref in zip(a_refs, w_refs):
        out = out + _dot(a_ref[...].astype(BF16), w_ref[...])
    o_ref[...] = _post_norm(x_ref[...], mod_ref[0], out, g_ref[...], b_ref[...])


def _proj_ln(a_list, w_list, x2d, mod3, bias, g, b, tm, tps):
    m = x2d.shape[0]
    n_in = len(a_list)
    row = lambda width: pl.BlockSpec((tm, width), lambda i: (i, 0))
    vec = _const_spec((1, D_MODEL))
    return pl.pallas_call(
        functools.partial(_proj_ln_kernel, n_in=n_in), grid=(m // tm,),
        in_specs=[row(a.shape[1]) for a in a_list] + [_const_spec(w.shape) for w in w_list]
        + [row(D_MODEL), _mod_spec(mod3, tm, tps), vec, vec, vec],
        out_specs=row(D_MODEL),
        out_shape=jax.ShapeDtypeStruct((m, D_MODEL), F32),
        compiler_params=_cparams(1), name="proj_postnorm")(*a_list, *w_list, x2d, mod3, bias, g, b)


def _ffn_kernel(x_ref, mod_ref, w1_ref, w3_ref, w2_ref, g_ref, b_ref, o_ref, *, ff):
    x = x_ref[...]
    mod = mod_ref[0]
    h = _modulate(x, mod).astype(BF16)
    acc = jnp.zeros(x.shape, F32)
    for c in range(ff // FF_CHUNK):
        cs = slice(c * FF_CHUNK, (c + 1) * FF_CHUNK)
        z = _silu(_dot(h, w1_ref[:, cs])) * _dot(h, w3_ref[:, cs])
        acc = acc + _dot(z.astype(BF16), w2_ref[cs, :])
    o_ref[...] = _post_norm(x, mod, acc, g_ref[...], b_ref[...])


def _ffn(x2d, mod3, w1, w3, w2, g, b, tm, tps):
    m = x2d.shape[0]
    row = pl.BlockSpec((tm, D_MODEL), lambda i: (i, 0))
    vec = _const_spec((1, D_MODEL))
    return pl.pallas_call(
        functools.partial(_ffn_kernel, ff=w1.shape[1]), grid=(m // tm,),
        in_specs=[row, _mod_spec(mod3, tm, tps), _const_spec(w1.shape, True), _const_spec(w3.shape, True),
                  _const_spec(w2.shape, True), vec, vec],
        out_specs=row,
        out_shape=jax.ShapeDtypeStruct((m, D_MODEL), F32),
        compiler_params=_cparams(1), name="dense_ffn")(x2d, mod3, w1, w3, w2, g, b)


def _cf_in_kernel(x_ref, mod_ref, w_ref, b_ref, o_ref):
    h = _modulate(x_ref[...], mod_ref[0]).astype(BF16)
    u = _dot(h, w_ref[...]) + b_ref[...]
    o_ref[...] = u[:, :D_MODEL] * jax.nn.sigmoid(u[:, D_MODEL:])


def _cf_in(x2d, mod3, w, bias, tm, tps):
    m = x2d.shape[0]
    row = pl.BlockSpec((tm, D_MODEL), lambda i: (i, 0))
    return pl.pallas_call(
        _cf_in_kernel, grid=(m // tm,),
        in_specs=[row, _mod_spec(mod3, tm, tps), _const_spec(w.shape), _const_spec(bias.shape)],
        out_specs=row,
        out_shape=jax.ShapeDtypeStruct((m, D_MODEL), F32),
        compiler_params=_cparams(1), name="conformer_in")(x2d, mod3, w, bias)


def _cf_conv_kernel(x_ref, halo_ref, dw_ref, db_ref, g_ref, b_ref, z_ref, s_ref, *, tt, zero_first):
    halo = halo_ref[...]
    if zero_first:
        halo = jnp.where(pl.program_id(1) == 0, 0.0, halo)
    s_ref[0, 0:CF_HALO, :] = halo
    s_ref[0, CF_HALO:CF_HALO + tt, :] = x_ref[...]
    n_sh = CF_HALO + tt - SUBLANES
    for k in range(1, SUBLANES):
        s_ref[k, 0:n_sh, :] = s_ref[0, k:k + n_sh, :]
    sub = min(CONV_SUB, tt)
    first = CF_HALO - (CF_KERNEL - 1)
    for c in range(tt // sub):
        r0 = c * sub
        y = jnp.broadcast_to(db_ref[...], (sub // SUBLANES, SUBLANES, D_MODEL))
        for j in range(CF_KERNEL):
            k = (first + j) % SUBLANES
            a0 = r0 + first + j - k
            y = y + dw_ref[j] * s_ref[k, a0:a0 + sub, :].reshape(sub // SUBLANES, SUBLANES, D_MODEL)
        z_ref[r0:r0 + sub, :] = _silu(_layer_norm(y.reshape(sub, D_MODEL), g_ref[...], b_ref[...]))


def _cf_conv(glu, halo_src, halo_map, dw, db, g, b, bsz, t, tt, zero_first):
    nt = t // tt
    vec = _const_spec((1, D_MODEL))
    return pl.pallas_call(
        functools.partial(_cf_conv_kernel, tt=tt, zero_first=zero_first), grid=(bsz, nt),
        in_specs=[pl.BlockSpec((tt, D_MODEL), lambda bi, ti: (bi * nt + ti, 0)),
                  pl.BlockSpec((CF_HALO, D_MODEL), halo_map),
                  _const_spec(dw.shape), vec, vec, vec],
        out_specs=pl.BlockSpec((tt, D_MODEL), lambda bi, ti: (bi * nt + ti, 0)),
        out_shape=jax.ShapeDtypeStruct((bsz * t, D_MODEL), F32),
        scratch_shapes=[pltpu.VMEM((SUBLANES, CF_HALO + tt, D_MODEL), F32)],
        compiler_params=_cparams(2), name="conformer_conv")(glu, halo_src, dw, db, g, b)


def _router_kernel(x_ref, mod_ref, rw_ref, h_ref, ids_ref, gates_ref, *, n_real):
    h = _modulate(x_ref[...], mod_ref[0])
    hz = jnp.where(pl.program_id(0) < n_real, h, 0.0)
    for s in range(ROW_PARTS):
        h_ref[pl.ds(s, h.shape[0], stride=ROW_PARTS), :] = hz[:, s * LANES:(s + 1) * LANES]
    logits = lax.dot_general(h, rw_ref[...], (((1,), (0,)), ((), ())), precision=lax.Precision.HIGHEST,
                             preferred_element_type=F32)
    lane = lax.broadcasted_iota(I32, logits.shape, 1)
    logits = jnp.where(lane < N_EXPERTS, logits, -jnp.inf)
    m1 = jnp.max(logits, axis=-1, keepdims=True)
    i1 = jnp.min(jnp.where(logits == m1, lane, LANES), axis=-1, keepdims=True)
    rest = jnp.where(lane == i1, -jnp.inf, logits)
    m2 = jnp.max(rest, axis=-1, keepdims=True)
    i2 = jnp.min(jnp.where(rest == m2, lane, LANES), axis=-1, keepdims=True)
    e2 = jnp.exp(m2 - m1)
    inv = 1.0 / (1.0 + e2)
    col = lax.broadcasted_iota(I32, ids_ref.shape, 1)
    ids_ref[...] = jnp.where(col == 0, i1, jnp.where(col == 1, i2, 0))
    gates_ref[...] = jnp.where(col == 0, inv, jnp.where(col == 1, e2 * inv, 0.0))


def _router(x2d, mod3, rw, tm, tps, n_total, row_off, h_prev):
    m = x2d.shape[0]
    n_real = m // tm
    off = row_off // tm
    if h_prev is None:
        steps = -(-n_total // tm)
        clamp = lambda i: jnp.minimum(i, n_real - 1)
    else:
        steps = n_real
        clamp = lambda i: i
    if mod3.shape[1] == 1:
        mspec = pl.BlockSpec((1, 1, mod3.shape[2]), lambda i: (clamp(i) // tps, 0, 0))
    else:
        mspec = pl.BlockSpec((1, tm, mod3.shape[2]), lambda i: (0, clamp(i), 0))
    small = lambda: pl.BlockSpec((tm, SUBLANES), lambda i: (clamp(i), 0))
    in_specs = [pl.BlockSpec((tm, D_MODEL), lambda i: (clamp(i), 0)), mspec, _const_spec(rw.shape)]
    args = [x2d, mod3, rw]
    kern = functools.partial(_router_kernel, n_real=n_real)
    aliases = {}
    if h_prev is not None:
        in_specs.append(pl.BlockSpec(memory_space=pl.ANY))
        args.append(h_prev)
        aliases = {3: 0}
        kern = lambda x, md, rw_, hp, h, ids, gt: _router_kernel(x, md, rw_, h, ids, gt, n_real=n_real)
    return pl.pallas_call(
        kern, grid=(steps,), in_specs=in_specs,
        out_specs=[pl.BlockSpec((tm * ROW_PARTS, LANES), lambda i: (i + off, 0)), small(), small()],
        out_shape=[jax.ShapeDtypeStruct((n_total * ROW_PARTS, LANES), F32), jax.ShapeDtypeStruct((m, SUBLANES), I32),
                   jax.ShapeDtypeStruct((m, SUBLANES), F32)],
        input_output_aliases=aliases,
        compiler_params=_cparams(1), name="moe_router")(*args)


ROW_SRC_BITS = 15
ROW_PARTS = D_MODEL // LANES


DMA_GROUP = 8


def _for_rows(n, fn):
    full = n // DMA_GROUP

    def group(gi, c):
        for u in range(DMA_GROUP):
            fn(gi * DMA_GROUP + u)
        return c
    lax.fori_loop(0, full, group, 0)
    if isinstance(n, int):
        for r in range(full * DMA_GROUP, n):
            fn(r)
    else:
        for u in range(DMA_GROUP - 1):
            r = full * DMA_GROUP + u

            @pl.when(r < n)
            def _():
                fn(r)


def _experts_kernel(blk_e_ref, plan_ref, cnt_ref, nused_ref, h_ref, w1_ref, w3_ref, w2_ref, y_ref,
                    xs, xb, acc, ybuf, gsem, ssem, *, n_dst):
    i = pl.program_id(0)
    j = pl.program_id(1)
    nused = nused_ref[0]
    used = i < nused
    last = j == pl.num_programs(1) - 1
    src_mask = (1 << ROW_SRC_BITS) - 1
    rows_per_step = MOE_TILE // MOE_STEPS

    def tile_rows(row):
        return pl.ds(pl.multiple_of(row * ROW_PARTS, ROW_PARTS), ROW_PARTS)

    def gather_row(blk, r):
        tok = plan_ref[blk * MOE_TILE + r] & src_mask
        pltpu.make_async_copy(h_ref.at[tile_rows(tok), :], xs.at[blk % 2, tile_rows(r), :], gsem.at[blk % 2]).start()

    def wait_gather(slot):
        pltpu.make_async_copy(h_ref.at[pl.ds(0, MOE_TILE * ROW_PARTS), :], xs.at[slot], gsem.at[slot]).wait()

    def scatter_row(blk, n_valid, r):
        dst = plan_ref[blk * MOE_TILE + r] >> ROW_SRC_BITS
        dst = jnp.where(r < n_valid, dst, n_dst + r)
        pltpu.make_async_copy(ybuf.at[tile_rows(r), :], y_ref.at[tile_rows(dst), :], ssem).start()

    def wait_scatter():
        pltpu.make_async_copy(ybuf, y_ref.at[pl.ds(0, MOE_TILE * ROW_PARTS), :], ssem).wait()

    def part(s):
        return pl.ds(s, MOE_TILE, stride=ROW_PARTS)

    @pl.when(used & (j == 0))
    def _():
        @pl.when(i == 0)
        def _():
            _for_rows(MOE_TILE, lambda r: gather_row(i, r))
            ybuf[...] = jnp.zeros(ybuf.shape, F32)
        wait_gather(i % 2)
        for s in range(ROW_PARTS):
            xb[:, s * LANES:(s + 1) * LANES] = xs[i % 2, part(s), :].astype(BF16)
        acc[...] = jnp.zeros(acc.shape, F32)

    @pl.when(used)
    def _():
        nxt = jnp.minimum(i + 1, nused - 1)
        prev = jnp.maximum(i - 1, 0)
        n_prev = jnp.where(i >= 1, cnt_ref[prev], 0)
        r0 = j * rows_per_step
        for u in range(rows_per_step):
            gather_row(nxt, r0 + u)
            scatter_row(prev, n_prev, r0 + u)
        x = xb[...]
        z = _silu(_dot(x, w1_ref[...])) * _dot(x, w3_ref[...])
        acc[...] += _dot(z.astype(BF16), w2_ref[...])

    @pl.when(used & last)
    def _():
        wait_scatter()
        for s in range(ROW_PARTS):
            ybuf[part(s), :] = acc[:, s * LANES:(s + 1) * LANES]

        @pl.when(i == nused - 1)
        def _():
            _for_rows(MOE_TILE, lambda r: scatter_row(i, cnt_ref[i], r))
            wait_scatter()
            wait_gather(i % 2)


def _experts(blk_e, plan, cnt, nused, h_all, n_tok, w1, w3, w2, layer, n_blk):
    ff = w1.shape[3]
    nj = ff // MOE_FF_CHUNK

    def jj(i, j, nu):
        return jnp.where(i < nu[0], j, nj - 1)

    wspec = lambda shape, imap: pl.BlockSpec(shape, imap)
    up = (None, None, D_MODEL, MOE_FF_CHUNK)
    grid_spec = pltpu.PrefetchScalarGridSpec(
        num_scalar_prefetch=4, grid=(n_blk, nj),
        in_specs=[pl.BlockSpec(memory_space=pl.ANY),
                  wspec(up, lambda i, j, be, pn, ct, nu: (layer, be[i], 0, jj(i, j, nu))),
                  wspec(up, lambda i, j, be, pn, ct, nu: (layer, be[i], 0, jj(i, j, nu))),
                  wspec((None, None, MOE_FF_CHUNK, D_MODEL),
                        lambda i, j, be, pn, ct, nu: (layer, be[i], jj(i, j, nu), 0))],
        out_specs=pl.BlockSpec(memory_space=pl.ANY),
        scratch_shapes=[pltpu.VMEM((2, MOE_TILE * ROW_PARTS, LANES), F32), pltpu.VMEM((MOE_TILE, D_MODEL), BF16),
                        pltpu.VMEM((MOE_TILE, D_MODEL), F32), pltpu.VMEM((MOE_TILE * ROW_PARTS, LANES), F32),
                        pltpu.SemaphoreType.DMA((2,)), pltpu.SemaphoreType.DMA(())])
    assert nj == MOE_STEPS and ff == nj * MOE_FF_CHUNK
    return pl.pallas_call(
        functools.partial(_experts_kernel, n_dst=2 * n_tok), grid_spec=grid_spec,
        out_shape=jax.ShapeDtypeStruct(((2 * n_tok + MOE_TILE) * ROW_PARTS, LANES), F32),
        compiler_params=_cparams(2, disable_bounds_checks=True),
        name="moe_experts")(blk_e, plan, cnt, nused, h_all, w1, w3, w2)


def _combine_kernel(y0_ref, y1_ref, gates_ref, x_ref, mod_ref, g_ref, b_ref, o_ref):
    tm = x_ref.shape[0]

    def rows(y_ref):
        return jnp.concatenate([y_ref[pl.ds(s, tm, stride=ROW_PARTS), :] for s in range(ROW_PARTS)], axis=1)

    gates = gates_ref[...]
    out = gates[:, 0:1] * rows(y0_ref) + gates[:, 1:2] * rows(y1_ref)
    o_ref[...] = _post_norm(x_ref[...], mod_ref[0], out, g_ref[...], b_ref[...])


def _combine(ys, gates, x2d, mod3, g, b, tm, tps, tok_off, n_tok):
    m = x2d.shape[0]
    off0 = tok_off // tm
    off1 = (n_tok + tok_off) // tm
    row = lambda width: pl.BlockSpec((tm, width), lambda i: (i, 0))
    vec = _const_spec((1, D_MODEL))
    return pl.pallas_call(
        _combine_kernel, grid=(m // tm,),
        in_specs=[pl.BlockSpec((tm * ROW_PARTS, LANES), lambda i: (i + off0, 0)),
                  pl.BlockSpec((tm * ROW_PARTS, LANES), lambda i: (i + off1, 0)),
                  row(SUBLANES), row(D_MODEL), _mod_spec(mod3, tm, tps), vec, vec],
        out_specs=row(D_MODEL),
        out_shape=jax.ShapeDtypeStruct((m, D_MODEL), F32),
        compiler_params=_cparams(1), name="moe_combine")(ys, ys, gates, x2d, mod3, g, b)


def _route_plan(ids_all, n_tok):
    fe = ids_all[:, :2].reshape(-1)
    onehot = (fe[:, None] == jnp.arange(N_EXPERTS, dtype=I32)[None, :]).astype(I32)
    csum = jnp.cumsum(onehot, axis=0)
    rank = jnp.take_along_axis(csum, fe[:, None], axis=1)[:, 0] - 1
    counts = csum[-1]
    padded = (counts + MOE_TILE - 1) // MOE_TILE * MOE_TILE
    pend = jnp.cumsum(padded)
    pstart = pend - padded
    dest = (pstart[fe] + rank).astype(I32)
    n_blk = -(-(2 * n_tok) // MOE_TILE) + N_EXPERTS
    a = jnp.arange(2 * n_tok, dtype=I32)
    word = (a // 2) | (((a % 2) * n_tok + a // 2) << ROW_SRC_BITS)
    plan = jnp.zeros((n_blk * MOE_TILE,), I32).at[dest].set(word)
    nused = (pend[-1] // MOE_TILE).astype(I32)
    blk = jnp.arange(n_blk, dtype=I32)
    blk_e = jnp.searchsorted(pend, jnp.minimum(blk, nused - 1) * MOE_TILE, side='right').astype(I32)
    blk_e = jnp.minimum(blk_e, N_EXPERTS - 1)
    cnt = jnp.clip(pstart[blk_e] + counts[blk_e] - blk * MOE_TILE, 0, MOE_TILE)
    cnt = jnp.where(blk < nused, cnt, 0).astype(I32)
    return plan, blk_e, cnt, nused.reshape(1), n_blk


def _rope_tables(pos):
    half = HEAD_DIM // 2
    inv = 1.0 / (ROPE_THETA ** (jnp.arange(half, dtype=F32) * (2.0 / HEAD_DIM)))
    ang = pos.astype(F32)[:, None] * inv[None, :]
    c = jnp.cos(ang)
    s = jnp.sin(ang)
    return jnp.concatenate([c, c, c, c], axis=1), jnp.concatenate([-s, s, -s, s], axis=1)


def _cmp_rope_tables(nc):
    blk = jnp.concatenate([jnp.arange(0, nc, 2), jnp.arange(1, nc, 2)])
    return _rope_tables(blk * CMP_BLOCK + (CMP_BLOCK - 1))


def _w_in_layout(w_in):
    o1 = Q_COLS
    o2 = o1 + KV_COLS
    o3 = o2 + GATE_COLS
    pad = jnp.zeros((D_MODEL, LANES - GATE_COLS), w_in.dtype)
    return jnp.concatenate([w_in[:, :o2], w_in[:, o3:], w_in[:, o2:o3], pad], axis=1).astype(BF16)


def _block_diag(blocks):
    n, a, b = blocks.shape
    eye = jnp.eye(n, dtype=blocks.dtype)
    return (eye[:, None, :, None] * blocks[:, :, None, :]).reshape(n * a, n * b)


def _cmp_weights(w_ck, w_cv, pe_k, pe_v):
    wk = w_ck.reshape(CMP_BLOCK, HEAD_DIM, HEAD_DIM)
    wv = w_cv.reshape(CMP_BLOCK, HEAD_DIM, HEAD_DIM)
    w4 = jax.vmap(lambda a, b: _block_diag(jnp.stack([a, a, b, b])))(wk, wv).astype(BF16)
    pe4 = jnp.concatenate([pe_k, pe_k, pe_v, pe_v], axis=1)
    return w4, pe4


def kernel(x_prompt, x_sample, c_prompt, c_sample, page_table, cache_nsa_kv, state_nsa_win, state_rglru_h, state_rglru_conv, state_conformer_conv, w_mod, b_mod, ln_g, ln_b, w_in_even, w_out_even, w_cmp_k, w_cmp_v, pe_cmp_k, pe_cmp_v, rg_conv_w, rg_conv_b, rg_wa, rg_ba, rg_wx, rg_bx, rg_lam, cf_w1, cf_b1, cf_dw, cf_db, cf_ln_g, cf_ln_b, cf_w2, cf_b2, ff_w1, ff_w3, ff_w2, moe_router, moe_w1, moe_w3, moe_w2):
    bp, tp, d = x_prompt.shape
    bs, ts, _ = x_sample.shape
    npages = page_table.shape[1]
    past = npages * PAGE_SIZE
    n_p = bp * tp
    n_s = bs * ts
    n_tok = n_p + n_s
    assert d == D_MODEL and tp % ROW_TILE == 0 and tp % Q_TILE == 0 and tp >= WINDOW
    assert past % SEL_BLOCK == 0 and ts <= CMP_BLOCK and ts % SUBLANES == 0 and n_s % SUBLANES == 0
    assert n_p % MOE_TILE == 0 and n_p % n_s == 0
    tps_p = tp // ROW_TILE
    xp = x_prompt.reshape(n_p, d)
    xs = x_sample.reshape(n_s, d)
    vec = lambda v: v.reshape(1, -1)

    mod_all = _mod_all(jnp.concatenate([c_prompt, c_sample], axis=0), w_mod, b_mod)

    def mods(l, s):
        mrow = mod_all[2 * l + s]
        return mrow[:bp].reshape(bp, 1, 3 * d), jnp.repeat(mrow[bp:], ts, axis=0).reshape(1, n_s, 3 * d)

    cos_p, sin_p = _rope_tables(jnp.arange(tp))
    cos_s, sin_s = _rope_tables(jnp.tile(past + jnp.arange(ts), bs))
    ccos_p, csin_p = _cmp_rope_tables(tp // CMP_BLOCK)
    ccos_s, csin_s = _cmp_rope_tables(past // CMP_BLOCK)
    pt_flat = page_table.reshape(-1).astype(I32)
    n_even = cache_nsa_kv.shape[0]
    win_len = state_nsa_win.shape[2]
    cache_fm = cache_nsa_kv.transpose(0, 1, 3, 4, 5, 2).reshape(n_even, cache_nsa_kv.shape[1], 4 * LANES, PAGE_SIZE)
    win_fm = state_nsa_win.transpose(0, 1, 3, 4, 5, 2).reshape(n_even, bs, 2 * LANES, win_len)
    tm_c = min(Q_TILE, n_s)
    assert n_p % tm_c == 0 and n_tok % tm_c == 0
    assert n_tok < (1 << ROW_SRC_BITS) and 2 * n_tok < (1 << (31 - ROW_SRC_BITS))

    moe_w1_bf, moe_w3_bf, moe_w2_bf = moe_w1.astype(BF16), moe_w3.astype(BF16), moe_w2.astype(BF16)

    kv_p, kv_s, win_p, win_s, rh_p, rh_s, rc_p, rc_s, cc_p, cc_s = ([] for _ in range(10))
    for l in range(DEPTH):
        i = l // 2
        mp0, ms0 = mods(l, 0)
        mp1, ms1 = mods(l, 1)
        g0, b0, g1, b1 = vec(ln_g[l, 0]), vec(ln_b[l, 0]), vec(ln_g[l, 1]), vec(ln_b[l, 1])
        zero_bias = jnp.zeros((1, d), F32)
        if l % 2 == 0:
            w_in = _w_in_layout(w_in_even[i])
            w4, pe4 = _cmp_weights(w_cmp_k[i], w_cmp_v[i], pe_cmp_k[i], pe_cmp_v[i])
            wo_a = w_out_even[i][:NSA_WIDTH].astype(BF16)
            wo_r = w_out_even[i][NSA_WIDTH:].astype(BF16)
            wa = _block_diag(rg_wa[i]).astype(BF16)
            wx = _block_diag(rg_wx[i]).astype(BF16)
            rg_args = (rg_conv_w[i], vec(rg_conv_b[i]), wa, vec(rg_ba[i]), wx, vec(rg_bx[i]), vec(rg_lam[i]))

            q, kv4, wr, rx, ry, gl, kvt, wrt = _win_call(xp, mp0, w_in, cos_p, sin_p, ROW_TILE, tps_p, tp)
            cmp = _compress_prompt(kv4, bp, tp, w4, pe4, ccos_p, csin_p)
            o = _pattn(q, gl, cmp, kv4, kvt, wr, wrt, bp, tp)
            st = jnp.zeros((bp, RG_STATE_ROWS, RG_WIDTH), F32)
            rg, h_last, new_buf = _rglru(rx, ry, st, *rg_args, bp, tp)
            xp = _proj_ln([o, rg], [wo_a, wo_r], xp, mp0, zero_bias, g0, b0, ROW_TILE, tps_p)
            kv_p.append(kvt.reshape(bp, 4, N_KV_A, HEAD_DIM, tp).transpose(0, 4, 1, 2, 3))
            win_p.append(wrt.reshape(bp, 2, N_KV_A, HEAD_DIM, tp)[..., tp - min(WINDOW, tp):]
                         .transpose(0, 4, 1, 2, 3))
            rh_p.append(h_last.reshape(bp, RG_WIDTH))
            rc_p.append(new_buf)

            q, kv4, wr, rx, ry, gl = _win_call(xs, ms0, w_in, cos_s, sin_s, n_s, 1)
            cmp = _compress_sample(pt_flat, cache_fm, i, bs, npages, w4, pe4, ccos_s, csin_s)
            o = _sattn(pt_flat, cache_fm, i, q, gl, cmp, kv4, wr, win_fm, bs, ts, npages)
            st = jnp.concatenate([state_rglru_h[i][:, None, :],
                                  jnp.zeros((bs, RG_STATE_ROWS - RG_CONV, RG_WIDTH), F32),
                                  state_rglru_conv[i]], axis=1)
            rg, h_last, new_buf = _rglru(rx, ry, st, *rg_args, bs, ts)
            xs = _proj_ln([o, rg], [wo_a, wo_r], xs, ms0, zero_bias, g0, b0, n_s, 1)
            kv_s.append(kv4.reshape(bs, ts, 4, N_KV_A, HEAD_DIM))
            wfull = jnp.concatenate([state_nsa_win[i], wr.reshape(bs, ts, 2, N_KV_A, HEAD_DIM)], axis=1)
            win_s.append(wfull[:, wfull.shape[1] - min(WINDOW, wfull.shape[1]):])
            rh_s.append(h_last.reshape(bs, RG_WIDTH))
            rc_s.append(new_buf)

            w1, w3, w2 = ff_w1[i].astype(BF16), ff_w3[i].astype(BF16), ff_w2[i].astype(BF16)
            xp = _ffn(xp, mp1, w1, w3, w2, g1, b1, ROW_TILE, tps_p)
            xs = _ffn(xs, ms1, w1, w3, w2, g1, b1, n_s, 1)
        else:
            cw1 = cf_w1[i].astype(BF16)
            cw2 = cf_w2[i].astype(BF16)
            dw = jnp.broadcast_to(cf_dw[i][:, None, :], (CF_KERNEL, SUBLANES, d))
            conv_args = (dw, vec(cf_db[i]), vec(cf_ln_g[i]), vec(cf_ln_b[i]))

            glu = _cf_in(xp, mp0, cw1, vec(cf_b1[i]), ROW_TILE, tps_p)
            per = CONV_TILE // CF_HALO
            z = _cf_conv(glu, glu, lambda bi, ti: (jnp.maximum((bi * (tp // CONV_TILE) + ti) * per - 1, 0), 0),
                         *conv_args, bp, tp, CONV_TILE, True)
            xp = _proj_ln([z], [cw2], xp, mp0, vec(cf_b2[i]), g0, b0, ROW_TILE, tps_p)
            cc_p.append(glu.reshape(bp, tp, d)[:, tp - (CF_KERNEL - 1):])

            glu = _cf_in(xs, ms0, cw1, vec(cf_b1[i]), n_s, 1)
            halo = jnp.concatenate([jnp.zeros((bs, CF_HALO - (CF_KERNEL - 1), d), F32),
                                    state_conformer_conv[i]], axis=1).reshape(bs * CF_HALO, d)
            z = _cf_conv(glu, halo, lambda bi, ti: (bi, 0), *conv_args, bs, ts, ts, False)
            xs = _proj_ln([z], [cw2], xs, ms0, vec(cf_b2[i]), g0, b0, n_s, 1)
            cc_s.append(jnp.concatenate([state_conformer_conv[i], glu.reshape(bs, ts, d)],
                                        axis=1)[:, -(CF_KERNEL - 1):])

            rw = jnp.concatenate([moe_router[i], jnp.zeros((d, LANES - N_EXPERTS), F32)], axis=1)
            h_all, ids_p, gates_p = _router(xp, mp1, rw, ROW_TILE, tps_p, n_tok, 0, None)
            h_all, ids_s, gates_s = _router(xs, ms1, rw, n_s, 1, n_tok, n_p, h_all)
            plan, blk_e, cnt, nused, n_blk = _route_plan(jnp.concatenate([ids_p, ids_s], axis=0), n_tok)
            ys = _experts(blk_e, plan, cnt, nused, h_all, n_tok, moe_w1_bf, moe_w3_bf, moe_w2_bf, i, n_blk)
            xp = _combine(ys, gates_p, xp, mp1, g1, b1, tm_c, tp // tm_c, 0, n_tok)
            xs = _combine(ys, gates_s, xs, ms1, g1, b1, tm_c, 1, n_p, n_tok)
    return (xp.reshape(bp, tp, d), xs.reshape(bs, ts, d), jnp.stack(kv_p), jnp.stack(kv_s),
            jnp.stack(win_p), jnp.stack(win_s), jnp.stack(rh_p), jnp.stack(rh_s),
            jnp.stack(rc_p), jnp.stack(rc_s), jnp.stack(cc_p), jnp.stack(cc_s))
```

```python
import functools

import jax
import jax.numpy as jnp
from jax import lax
from jax.experimental import pallas as pl
from jax.experimental.pallas import tpu as pltpu

F32 = jnp.float32
BF16 = jnp.bfloat16
I32 = jnp.int32

D_MODEL = 1024
HEAD_DIM = 64
N_HEADS_A = 8
N_KV_A = 2
HEADS_PER_GROUP = N_HEADS_A // N_KV_A
GROUP_COLS = N_KV_A * HEAD_DIM
CMP_BLOCK = 32
SEL_BLOCK = 64
TOP_N = 16
WINDOW = 512
ROPE_THETA = 10000.0
PAGE_SIZE = 128
RG_WIDTH = 512
RG_BLOCKS = 8
RG_CONV = 4
RG_C = 8.0
NSA_WIDTH = 512
Q_COLS = NSA_WIDTH
KV_COLS = 6 * GROUP_COLS
GATE_COLS = 3 * N_HEADS_A
CF_KERNEL = 31
CF_HALO = 32
N_EXPERTS = 8
DEPTH = 4
ALPHA = (2.0 * DEPTH) ** 0.25
LN_EPS = 1e-5
NEG = -1e30
FORCE_SCORE = 1e4
ATTN_SCALE = HEAD_DIM ** -0.5

LANES = 128
SUBLANES = 8
VMEM_LIMIT = 56 * 1024 * 1024

ROW_TILE = 512
Q_TILE = 256
CONV_TILE = 256
CONV_SUB = 32
MOE_TILE = 512
FF_CHUNK = 256
MOE_FF_CHUNK = 896
MOE_STEPS = 4
MOE_DMA_STEPS = 2


def _cparams(n_axes, **kw):
    return pltpu.CompilerParams(dimension_semantics=("arbitrary",) * n_axes,
                                vmem_limit_bytes=VMEM_LIMIT, **kw)


def _const_spec(shape, single_buffer=False):
    n = len(shape)
    if single_buffer:
        return pl.BlockSpec(shape, lambda *a: (0,) * n, pipeline_mode=pl.Buffered(1))
    return pl.BlockSpec(shape, lambda *a: (0,) * n)


def _mod_spec(mod3, tm, tps):
    if mod3.shape[1] == 1:
        return pl.BlockSpec((1, 1, mod3.shape[2]), lambda i: (i // tps, 0, 0))
    return pl.BlockSpec((1, tm, mod3.shape[2]), lambda i: (0, i, 0))


def _dot(a, b):
    return jnp.dot(a, b, preferred_element_type=F32)


def _dot_nt(a, b):
    return lax.dot_general(a, b, (((1,), (1,)), ((), ())), preferred_element_type=F32)


def _layer_norm(y, g, b):
    mu = jnp.mean(y, axis=-1, keepdims=True)
    yc = y - mu
    var = jnp.mean(yc * yc, axis=-1, keepdims=True)
    return yc * lax.rsqrt(var + LN_EPS) * g + b


def _post_norm(x, mod, out, g, b):
    gate = mod[:, 2 * D_MODEL:]
    return _layer_norm(ALPHA * x + (1.0 + gate) * out, g, b)


def _modulate(x, mod):
    return x * (1.0 + mod[:, D_MODEL:2 * D_MODEL]) + mod[:, :D_MODEL]


def _silu(x):
    return x * jax.nn.sigmoid(x)


def _rope128(v, cos, sin_signed):
    lane = lax.broadcasted_iota(I32, v.shape, 1)
    from_hi = pltpu.roll(v, LANES - HEAD_DIM // 2, 1)
    from_lo = pltpu.roll(v, HEAD_DIM // 2, 1)
    swapped = jnp.where((lane & (HEAD_DIM - 1)) < HEAD_DIM // 2, from_hi, from_lo)
    return v * cos + swapped * sin_signed


def _rope(v, cos, sin_signed):
    k = v.shape[1] // LANES
    parts = [_rope128(v[:, i * LANES:(i + 1) * LANES], cos, sin_signed) for i in range(k)]
    return parts[0] if k == 1 else jnp.concatenate(parts, axis=1)


def _softmax_parts(parts):
    masked = [jnp.where(m, s, NEG) for s, m in parts]
    mx = functools.reduce(jnp.maximum, [jnp.max(s, axis=-1, keepdims=True) for s in masked])
    es = [jnp.where(m, jnp.exp(s - mx), 0.0) for s, (_, m) in zip(masked, parts)]
    den = functools.reduce(jnp.add, [jnp.sum(e, axis=-1, keepdims=True) for e in es])
    return es, 1.0 / jnp.maximum(den, 1e-30)


def _attend(qb, pieces):
    es, inv = _softmax_parts([(_dot(qb, k) if fm else _dot_nt(qb, k), m) for k, _, m, fm in pieces])
    o = functools.reduce(jnp.add, [_dot_nt(e.astype(BF16), v) if fm else _dot(e.astype(BF16), v)
                                   for e, (_, v, _, fm) in zip(es, pieces)])
    return o * inv


def _topk_mask(vals, n_valid, kk):
    lane = lax.broadcasted_iota(I32, vals.shape, 1)
    rank = jnp.zeros(vals.shape, I32)
    for m in range(n_valid):
        col = vals[:, m:m + 1]
        later = jnp.where(lane > m, 1, 0)
        rank = rank + jnp.where(col > vals, 1, jnp.where(col == vals, later, 0))
    return (rank < kk) & (lane < n_valid)


def _mod_kernel(c_ref, w_ref, b_ref, o_ref):
    s = _silu(c_ref[...]).astype(BF16)
    o_ref[0] = _dot(s, w_ref[0].astype(BF16)) + b_ref[0]


def _mod_all(c_all, w_mod, b_mod):
    n = c_all.shape[0]
    nl = w_mod.shape[0] * w_mod.shape[1]
    w = w_mod.reshape(nl, D_MODEL, 3 * D_MODEL)
    b = b_mod.reshape(nl, 1, 3 * D_MODEL)
    return pl.pallas_call(
        _mod_kernel, grid=(nl, 3),
        in_specs=[pl.BlockSpec((n, D_MODEL), lambda l, j: (0, 0)),
                  pl.BlockSpec((1, D_MODEL, D_MODEL), lambda l, j: (l, 0, j)),
                  pl.BlockSpec((1, 1, D_MODEL), lambda l, j: (l, 0, j))],
        out_specs=pl.BlockSpec((1, n, D_MODEL), lambda l, j: (l, 0, j)),
        out_shape=jax.ShapeDtypeStruct((nl, n, 3 * D_MODEL), F32),
        compiler_params=_cparams(2), name="mod_all")(c_all, w, b)


_O_KV = Q_COLS
_O_RX = _O_KV + KV_COLS
_O_RY = _O_RX + RG_WIDTH
_O_GL = _O_RY + RG_WIDTH
W_IN_COLS = _O_GL + LANES


def _win_kernel(x_ref, mod_ref, w_ref, cos_ref, sin_ref, q_ref, kv_ref, wr_ref, rx_ref, ry_ref, gl_ref,
                *fm_refs):
    h = _modulate(x_ref[...], mod_ref[0]).astype(BF16)
    u = _dot(h, w_ref[...])
    cos = cos_ref[...]
    sin = sin_ref[...]
    q_ref[...] = _rope(u[:, :Q_COLS], cos, sin)
    c = _O_KV
    kv = jnp.concatenate([u[:, c:c + 2 * LANES],
                          _rope128(u[:, c + 2 * LANES:c + 3 * LANES], cos, sin),
                          u[:, c + 3 * LANES:c + 4 * LANES]], axis=1)
    wr = jnp.concatenate([_rope128(u[:, c + 4 * LANES:c + 5 * LANES], cos, sin),
                          u[:, c + 5 * LANES:c + 6 * LANES]], axis=1)
    kv_ref[...] = kv
    wr_ref[...] = wr
    rx_ref[...] = u[:, _O_RX:_O_RY]
    ry_ref[...] = u[:, _O_RY:_O_GL]
    gl_ref[...] = u[:, _O_GL:]
    if fm_refs:
        kvt_ref, wrt_ref = fm_refs
        kvt_ref[...] = kv.T
        wrt_ref[...] = wr.T


def _win_call(x2d, mod3, w, cos, sin, tm, tps, t_len=None):
    m = x2d.shape[0]
    row = lambda width: pl.BlockSpec((tm, width), lambda i: (i, 0))
    widths = (Q_COLS, 4 * LANES, 2 * LANES, RG_WIDTH, RG_WIDTH, LANES)
    out_specs = [row(wd) for wd in widths]
    out_shape = [jax.ShapeDtypeStruct((m, wd), F32) for wd in widths]
    if t_len is not None:
        for feat in (4 * LANES, 2 * LANES):
            out_specs.append(pl.BlockSpec((feat, tm), lambda i: (i // tps, i % tps)))
            out_shape.append(jax.ShapeDtypeStruct((m // t_len * feat, t_len), F32))
    return pl.pallas_call(
        _win_kernel, grid=(m // tm,),
        in_specs=[row(D_MODEL), _mod_spec(mod3, tm, tps), _const_spec(w.shape),
                  pl.BlockSpec((tm, LANES), lambda i: (i % tps, 0)),
                  pl.BlockSpec((tm, LANES), lambda i: (i % tps, 0))],
        out_specs=out_specs, out_shape=out_shape,
        compiler_params=_cparams(1), name="mixer_in")(x2d, mod3, w, cos, sin)


def _compress_core(k_ref, v_ref, w4_ref, pe4_ref, cos_ref, sin_ref, out_ref, nb):
    half = nb // 2
    acc = jnp.zeros((nb, 2 * LANES), F32)
    for l in range(CMP_BLOCK):
        even = pl.ds(l, half, stride=2 * CMP_BLOCK)
        odd = pl.ds(CMP_BLOCK + l, half, stride=2 * CMP_BLOCK)
        xk = jnp.concatenate([k_ref[even, :], k_ref[odd, :]], axis=0)
        xv = jnp.concatenate([v_ref[even, :], v_ref[odd, :]], axis=0)
        x = (jnp.concatenate([xk, xv], axis=1) + pe4_ref[pl.ds(l, 1), :]).astype(BF16)
        acc = acc + _dot(x, w4_ref[l])
    out_ref[:, 0:LANES] = _rope128(acc[:, 0:LANES], cos_ref[...], sin_ref[...])
    out_ref[:, LANES:2 * LANES] = acc[:, LANES:2 * LANES]


def _compress_prompt_kernel(k_ref, v_ref, w4_ref, pe4_ref, cos_ref, sin_ref, out_ref, *, nb):
    _compress_core(k_ref, v_ref, w4_ref, pe4_ref, cos_ref, sin_ref, out_ref, nb)


def _compress_prompt(kv4, bsz, t, w4, pe4, ccos, csin):
    nb = t // CMP_BLOCK
    return pl.pallas_call(
        functools.partial(_compress_prompt_kernel, nb=nb), grid=(bsz,),
        in_specs=[pl.BlockSpec((t, LANES), lambda b: (b, 0)), pl.BlockSpec((t, LANES), lambda b: (b, 1)),
                  _const_spec(w4.shape), _const_spec(pe4.shape),
                  _const_spec(ccos.shape), _const_spec(csin.shape)],
        out_specs=pl.BlockSpec((nb, 2 * LANES), lambda b: (b, 0)),
        out_shape=jax.ShapeDtypeStruct((bsz * nb, 2 * LANES), F32),
        compiler_params=_cparams(1), name="compress_prompt")(kv4, kv4, w4, pe4, ccos, csin)


def _prefetch_pages(pt_ref, cache_ref, sems, layer, feat0, npages, dst_fn):
    b = pl.program_id(0)

    def copy(seq, j):
        slot = seq % 2
        src = cache_ref.at[layer, pt_ref[seq * npages + j], pl.ds(feat0, 2 * LANES), :]
        return pltpu.make_async_copy(src, dst_fn(slot, j), sems.at[slot])

    def start_seq(seq):
        def body(j, c):
            copy(seq, j).start()
            return c
        lax.fori_loop(0, npages, body, 0)

    @pl.when(b == 0)
    def _():
        start_seq(b)

    @pl.when(b + 1 < pl.num_programs(0))
    def _():
        start_seq(b + 1)

    def wait(j, c):
        copy(b, j).wait()
        return c
    lax.fori_loop(0, npages, wait, 0)
    return b % 2


def _compress_sample_kernel(pt_ref, cache_ref, w4_ref, pe4_ref, cos_ref, sin_ref, out_ref, raw, kbuf, vbuf, sems,
                            *, layer, npages):
    def dst(slot, j):
        return raw.at[slot, pl.ds(pl.multiple_of(j * 2 * LANES, 2 * LANES), 2 * LANES), :]
    slot = _prefetch_pages(pt_ref, cache_ref, sems, layer, 0, npages, dst)

    def to_token_major(j, c):
        r0 = pl.multiple_of(j * 2 * LANES, 2 * LANES)
        t0 = pl.multiple_of(j * PAGE_SIZE, PAGE_SIZE)
        kbuf[pl.ds(t0, PAGE_SIZE), :] = raw[slot, pl.ds(r0, LANES), :].T
        vbuf[pl.ds(t0, PAGE_SIZE), :] = raw[slot, pl.ds(r0 + LANES, LANES), :].T
        return c
    lax.fori_loop(0, npages, to_token_major, 0, unroll=4)
    _compress_core(kbuf, vbuf, w4_ref, pe4_ref, cos_ref, sin_ref, out_ref, npages * PAGE_SIZE // CMP_BLOCK)


def _compress_sample(pt_flat, cache_fm, layer, bsz, npages, w4, pe4, ccos, csin):
    past = npages * PAGE_SIZE
    nb = past // CMP_BLOCK
    grid_spec = pltpu.PrefetchScalarGridSpec(
        num_scalar_prefetch=1, grid=(bsz,),
        in_specs=[pl.BlockSpec(memory_space=pl.ANY),
                  _const_spec(w4.shape), _const_spec(pe4.shape),
                  _const_spec(ccos.shape), _const_spec(csin.shape)],
        out_specs=pl.BlockSpec((nb, 2 * LANES), lambda b, pt: (b, 0)),
        scratch_shapes=[pltpu.VMEM((2, npages * 2 * LANES, PAGE_SIZE), F32),
                        pltpu.VMEM((past, LANES), F32), pltpu.VMEM((past, LANES), F32),
                        pltpu.SemaphoreType.DMA((2,))])
    return pl.pallas_call(
        functools.partial(_compress_sample_kernel, layer=layer, npages=npages),
        grid_spec=grid_spec,
        out_shape=jax.ShapeDtypeStruct((bsz * nb, 2 * LANES), F32),
        compiler_params=_cparams(1, disable_bounds_checks=True),
        name="compress_sample")(pt_flat, cache_fm, w4, pe4, ccos, csin)


def _cmp_positions(nc):
    n = lax.broadcasted_iota(I32, (1, nc), 1)
    half = nc // 2
    blk = jnp.where(n < half, 2 * n, 2 * (n - half) + 1)
    return blk * CMP_BLOCK + (CMP_BLOCK - 1)


def _softmax_cols(s, mask):
    s = jnp.where(mask, s, NEG)
    e = jnp.where(mask, jnp.exp(s - jnp.max(s, axis=0, keepdims=True)), 0.0)
    return e, 1.0 / jnp.maximum(jnp.sum(e, axis=0, keepdims=True), 1e-30)


def _pattn_kernel(q_ref, gl_ref, cmp_ref, ks_ref, vst_ref, kw0_ref, kw1_ref, kw2_ref, vwt0_ref, vwt1_ref,
                  vwt2_ref, o_ref, ks_bf, vst_bf, bias_ref, acc_ref, ot_ref, *, tq, t_len):
    ti = pl.program_id(1)
    nc = t_len // CMP_BLOCK
    nsel = t_len // SEL_BLOCK
    half = nc // 2
    blk_per_chunk = tq // SEL_BLOCK
    q0 = ti * tq

    @pl.when(ti == 0)
    def _():
        for g in range(N_KV_A):
            ks_bf[g] = ks_ref[:, g * HEAD_DIM:(g + 1) * HEAD_DIM].astype(BF16)
        vst_bf[...] = vst_ref[...].astype(BF16)

    qj = lax.broadcasted_iota(I32, (1, tq), 1)
    ki = lax.broadcasted_iota(I32, (tq, 1), 0)
    qpos = q0 + qj
    qs = (q_ref[...] * ATTN_SCALE).astype(BF16)
    gates_t = jax.nn.sigmoid(gl_ref[...]).T
    n = lax.broadcasted_iota(I32, (nc, 1), 0)
    cpos = jnp.where(n < half, 2 * n, 2 * (n - half) + 1) * CMP_BLOCK + (CMP_BLOCK - 1)
    m_c = cpos <= qpos
    blk = lax.broadcasted_iota(I32, (nsel, 1), 0)
    cur = qpos // SEL_BLOCK
    forced = (blk == cur) | (blk == 0)
    future = blk > cur
    later_blk = [jnp.where(blk > m, 1, 0) for m in range(nsel)]
    diag_bias = jnp.where(ki <= qj, 0.0, NEG)
    w_masks = []
    for k in range(3):
        dpos = qj - ki + (2 - k) * tq
        w_masks.append((dpos >= 0) & (dpos < WINDOW) & (ti + k - 2 >= 0))
    w_bias = jnp.where(jnp.concatenate(w_masks, axis=0), 0.0, NEG)
    kw_refs = (kw0_ref, kw1_ref, kw2_ref)
    vwt_refs = (vwt0_ref, vwt1_ref, vwt2_ref)

    for g in range(N_KV_A):
        ck = slice(g * HEAD_DIM, (g + 1) * HEAD_DIM)
        kc = cmp_ref[:, ck].astype(BF16)
        vc = cmp_ref[:, LANES + g * HEAD_DIM:LANES + (g + 1) * HEAD_DIM].astype(BF16)
        heads = [g * HEADS_PER_GROUP + r for r in range(HEADS_PER_GROUP)]
        q_heads = [qs[:, h * HEAD_DIM:(h + 1) * HEAD_DIM] for h in heads]
        o_cmp = []
        imp = jnp.zeros((nc, tq), F32)
        for s in [_dot_nt(kc, qh) for qh in q_heads]:
            e, inv = _softmax_cols(s, m_c)
            p = e * inv
            o_cmp.append(lax.dot_general(vc, p.astype(BF16), (((0,), (0,)), ((), ())),
                                         preferred_element_type=F32))
            imp = imp + p
        imp = imp[:half] + imp[half:]
        vals = jnp.where(forced, FORCE_SCORE, jnp.where(future, -1.0, imp))
        rank = jnp.zeros((nsel, tq), I32)
        for m in range(nsel):
            row = vals[m:m + 1, :]
            rank = rank + jnp.where(row > vals, 1, jnp.where(row == vals, later_blk[m], 0))
        bias_ref[g] = jnp.where(rank < min(TOP_N, nsel), 0.0, NEG)

        kw = jnp.concatenate([r[:, ck] for r in kw_refs], axis=0).astype(BF16)
        vwt = [r[ck, :].astype(BF16) for r in vwt_refs]

        def block_bias(c):
            rows = [jnp.broadcast_to(bias_ref[g, pl.ds(c * blk_per_chunk + j, 1), :], (SEL_BLOCK, tq))
                    for j in range(blk_per_chunk)]
            return jnp.concatenate(rows, axis=0)

        def chunk(c, carry, extra_bias=None):
            r0 = pl.multiple_of(c * tq, tq)
            k_chunk = ks_bf[g, pl.ds(r0, tq), :]
            v_chunk = vst_bf[ck, pl.ds(r0, tq)]
            bias = block_bias(c)
            if extra_bias is not None:
                bias = bias + extra_bias
            scores = [_dot_nt(k_chunk, qh) for qh in q_heads]
            out, probs, alphas = [], [], []
            for r in range(HEADS_PER_GROUP):
                m_run, l_run = carry[r]
                s = scores[r] + bias
                m_new = jnp.maximum(m_run, jnp.max(s, axis=0, keepdims=True))
                alpha = jnp.exp(m_run - m_new)
                p = jnp.exp(s - m_new)
                out.append((m_new, alpha * l_run + jnp.sum(p, axis=0, keepdims=True)))
                probs.append(p.astype(BF16))
                alphas.append(alpha)
            for r in range(HEADS_PER_GROUP):
                acc_ref[r] = alphas[r] * acc_ref[r] + _dot(v_chunk, probs[r])
            return tuple(out)

        acc_ref[...] = jnp.zeros(acc_ref.shape, F32)
        init = tuple((jnp.full((1, tq), -jnp.inf, F32), jnp.zeros((1, tq), F32)) for _ in heads)
        stats = chunk(ti, lax.fori_loop(0, ti, chunk, init), diag_bias)

        w_scores = [_dot_nt(kw, qh) + w_bias for qh in q_heads]
        w_probs = [jnp.exp(s - jnp.max(s, axis=0, keepdims=True)) for s in w_scores]
        for r, h in enumerate(heads):
            o_s = acc_ref[r] * (1.0 / jnp.maximum(stats[r][1], 1e-30))
            e = w_probs[r]
            inv = 1.0 / jnp.maximum(jnp.sum(e, axis=0, keepdims=True), 1e-30)
            e = e.astype(BF16)
            o_w = functools.reduce(jnp.add, [_dot(vwt[k], e[k * tq:(k + 1) * tq]) for k in range(3)]) * inv
            ot_ref[h * HEAD_DIM:(h + 1) * HEAD_DIM, :] = (
                gates_t[3 * h:3 * h + 1, :] * o_cmp[r] + gates_t[3 * h + 1:3 * h + 2, :] * o_s
                + gates_t[3 * h + 2:3 * h + 3, :] * o_w)
    o_ref[...] = ot_ref[...].T


def _block_expand(nblk, nkeys):
    return (jnp.arange(nkeys)[None, :] // SEL_BLOCK == jnp.arange(nblk)[:, None]).astype(BF16)


def _pattn(q, gl, cmp, kv4, kvt, wr, wrt, bsz, t):
    tq = Q_TILE
    nt = t // tq
    nc = t // CMP_BLOCK
    nsel = t // SEL_BLOCK
    row = lambda width: pl.BlockSpec((tq, width), lambda b, i: (b * nt + i, 0))
    kw = lambda back: pl.BlockSpec((tq, LANES), lambda b, i: (b * nt + jnp.maximum(i - back, 0), 0))
    vwt = lambda back: pl.BlockSpec((LANES, tq), lambda b, i: (b * 2 + 1, jnp.maximum(i - back, 0)))
    return pl.pallas_call(
        functools.partial(_pattn_kernel, tq=tq, t_len=t), grid=(bsz, nt),
        in_specs=[row(Q_COLS), row(LANES),
                  pl.BlockSpec((nc, 2 * LANES), lambda b, i: (b, 0)),
                  pl.BlockSpec((t, LANES), lambda b, i: (b, 2)),
                  pl.BlockSpec((LANES, t), lambda b, i: (b * 4 + 3, 0)),
                  kw(2), kw(1), kw(0), vwt(2), vwt(1), vwt(0)],
        out_specs=row(NSA_WIDTH),
        out_shape=jax.ShapeDtypeStruct((bsz * t, NSA_WIDTH), F32),
        scratch_shapes=[pltpu.VMEM((N_KV_A, t, HEAD_DIM), BF16), pltpu.VMEM((LANES, t), BF16),
                        pltpu.VMEM((N_KV_A, nsel, tq), F32), pltpu.VMEM((HEADS_PER_GROUP, HEAD_DIM, tq), F32),
                        pltpu.VMEM((NSA_WIDTH, tq), F32)],
        compiler_params=_cparams(2), name="nsa_prompt")(q, gl, cmp, kv4, kvt, wr, wr, wr, wrt, wrt, wrt)


def _sattn_kernel(pt_ref, cache_ref, q_ref, gl_ref, cmp_ref, kvn_ref, wrn_ref, win_ref, e_ref, o_ref,
                  buf, sems, *, layer, npages, ts):
    past = npages * PAGE_SIZE

    def dst(slot, j):
        return buf.at[slot, :, pl.ds(pl.multiple_of(j * PAGE_SIZE, PAGE_SIZE), PAGE_SIZE)]
    slot = _prefetch_pages(pt_ref, cache_ref, sems, layer, 2 * LANES, npages, dst)
    nc = past // CMP_BLOCK
    nblk_past = past // SEL_BLOCK
    nsel = nblk_past + 1
    lanes_sel = 2 * nblk_past
    rows = HEADS_PER_GROUP * ts
    tok1 = lax.broadcasted_iota(I32, (ts, 1), 0)
    tok = jnp.concatenate([tok1] * HEADS_PER_GROUP, axis=0)
    qpos1 = past + tok1
    qpos = past + tok
    qs = q_ref[...] * ATTN_SCALE
    gates = jax.nn.sigmoid(gl_ref[...])
    m_c = _cmp_positions(nc) <= qpos
    blk = lax.broadcasted_iota(I32, (1, lanes_sel), 1)
    cur = qpos1 // SEL_BLOCK
    forced = (blk == cur) | (blk == 0)
    future = blk > cur
    m_past = lax.broadcasted_iota(I32, (1, past), 1) <= qpos
    tkey = lax.broadcasted_iota(I32, (1, ts), 1)
    m_new = tkey <= tok
    win_len = win_ref.shape[1]
    wpos = past - win_len + lax.broadcasted_iota(I32, (1, win_len), 1)
    dpos = qpos - wpos
    m_wstate = (dpos >= 0) & (dpos < WINDOW) & (wpos >= 0)
    dnew = tok - tkey
    m_wnew = (dnew >= 0) & (dnew < WINDOW)
    for g in range(N_KV_A):
        ck = slice(g * HEAD_DIM, (g + 1) * HEAD_DIM)
        cv = slice(LANES + g * HEAD_DIM, LANES + (g + 1) * HEAD_DIM)
        heads = [g * HEADS_PER_GROUP + r for r in range(HEADS_PER_GROUP)]
        qg = jnp.concatenate([qs[:, h * HEAD_DIM:(h + 1) * HEAD_DIM] for h in heads], axis=0).astype(BF16)
        kc = cmp_ref[:, ck].astype(BF16)
        vc = cmp_ref[:, cv].astype(BF16)
        (e,), inv = _softmax_parts([(_dot_nt(qg, kc), m_c)])
        p = e * inv
        o_c = _dot(p.astype(BF16), vc)
        imp = p[0:ts]
        for r in range(1, HEADS_PER_GROUP):
            imp = imp + p[r * ts:(r + 1) * ts]
        imp = imp[:, :nc // 2] + imp[:, nc // 2:]
        imp = jnp.concatenate([imp, jnp.zeros((ts, lanes_sel - nc // 2), F32)], axis=1)
        vals = jnp.where(forced, FORCE_SCORE, jnp.where(future, -1.0, imp))
        sel = jnp.where(_topk_mask(vals, nsel, min(TOP_N, nsel)), 1.0, 0.0)
        sel = jnp.concatenate([sel] * HEADS_PER_GROUP, axis=0)
        sel_keys = _dot(sel[:, :nblk_past].astype(BF16), e_ref[...])
        m_s_past = (sel_keys > 0.5) & m_past
        m_s_new = (sel[:, nblk_past:nblk_past + 1] > 0.5) & m_new
        kn = kvn_ref[:, 2 * LANES:4 * LANES]
        o_s = _attend(qg, [(buf[slot, ck, :].astype(BF16), buf[slot, cv, :].astype(BF16), m_s_past, True),
                           (kn[:, ck].astype(BF16), kn[:, cv].astype(BF16), m_s_new, False)])
        o_w = _attend(qg, [(win_ref[ck, :].astype(BF16), win_ref[cv, :].astype(BF16), m_wstate, True),
                           (wrn_ref[:, ck].astype(BF16), wrn_ref[:, cv].astype(BF16), m_wnew, False)])
        for r, h in enumerate(heads):
            rs = slice(r * ts, (r + 1) * ts)
            o_ref[:, h * HEAD_DIM:(h + 1) * HEAD_DIM] = (
                gates[:, 3 * h:3 * h + 1] * o_c[rs] + gates[:, 3 * h + 1:3 * h + 2] * o_s[rs]
                + gates[:, 3 * h + 2:3 * h + 3] * o_w[rs])


def _sattn(pt_flat, cache_fm, layer, q, gl, cmp, kv4, wr, win_fm, bsz, ts, npages):
    past = npages * PAGE_SIZE
    nc = past // CMP_BLOCK
    win_len = win_fm.shape[3]
    expand = _block_expand(past // SEL_BLOCK, past)
    row = lambda width: pl.BlockSpec((ts, width), lambda b, pt: (b, 0))
    grid_spec = pltpu.PrefetchScalarGridSpec(
        num_scalar_prefetch=1, grid=(bsz,),
        in_specs=[pl.BlockSpec(memory_space=pl.ANY), row(Q_COLS), row(LANES),
                  pl.BlockSpec((nc, 2 * LANES), lambda b, pt: (b, 0)),
                  row(4 * LANES), row(2 * LANES),
                  pl.BlockSpec((None, None, 2 * LANES, win_len), lambda b, pt: (layer, b, 0, 0)),
                  _const_spec(expand.shape)],
        out_specs=row(NSA_WIDTH),
        scratch_shapes=[pltpu.VMEM((2, 2 * LANES, past), F32), pltpu.SemaphoreType.DMA((2,))])
    return pl.pallas_call(
        functools.partial(_sattn_kernel, layer=layer, npages=npages, ts=ts),
        grid_spec=grid_spec,
        out_shape=jax.ShapeDtypeStruct((bsz * ts, NSA_WIDTH), F32),
        compiler_params=_cparams(1, disable_bounds_checks=True),
        name="nsa_sample")(pt_flat, cache_fm, q, gl, cmp, kv4, wr, win_fm, expand)


RG_STATE_ROWS = SUBLANES
RG_ROWS = 256


def _rglru_kernel(rx_ref, ry_ref, st_ref, cw_ref, cb_ref, wa_ref, ba_ref, wx_ref, bx_ref, lam_ref,
                  out_ref, hl_ref, nb_ref, xs, a_s, b_s, *, t_len):
    ch = min(RG_ROWS, t_len)
    xs[0:RG_STATE_ROWS, :] = st_ref[0]
    xs[RG_STATE_ROWS:RG_STATE_ROWS + t_len, :] = rx_ref[...]
    lam = lam_ref[...]
    softplus_neg_lam = jnp.maximum(-lam, 0.0) + jnp.log1p(jnp.exp(-jnp.abs(lam)))
    sub = lax.broadcasted_iota(I32, (ch, RG_WIDTH), 0) & (SUBLANES - 1)
    for c in range(t_len // ch):
        r0 = c * ch
        xc = cb_ref[...]
        for j in range(RG_CONV):
            xc = xc + cw_ref[j:j + 1, :] * xs[r0 + RG_STATE_ROWS - (RG_CONV - 1) + j:
                                              r0 + RG_STATE_ROWS - (RG_CONV - 1) + j + ch, :]
        xb = xc.astype(BF16)
        r = jax.nn.sigmoid(_dot(xb, wa_ref[...]) + ba_ref[...])
        i = jax.nn.sigmoid(_dot(xb, wx_ref[...]) + bx_ref[...])
        log_a = -RG_C * r * softplus_neg_lam
        a = jnp.exp(log_a)
        one_minus_a2 = -jnp.tanh(log_a) * (jnp.exp(2.0 * log_a) + 1.0)
        bb = jnp.sqrt(one_minus_a2) * (i * xc)
        for s in (1, 2, 4):
            ok = sub >= s
            a_prev = pltpu.roll(a, s, 0)
            b_prev = pltpu.roll(bb, s, 0)
            bb = jnp.where(ok, a * b_prev + bb, bb)
            a = jnp.where(ok, a * a_prev, a)
        a_s[r0:r0 + ch, :] = a
        b_s[r0:r0 + ch, :] = bb

    def step(k, h):
        r0 = pl.multiple_of(k * SUBLANES, SUBLANES)
        hk = b_s[pl.ds(r0, SUBLANES), :] + a_s[pl.ds(r0, SUBLANES), :] * h
        b_s[pl.ds(r0, SUBLANES), :] = hk
        return jnp.broadcast_to(hk[SUBLANES - 1:SUBLANES, :], (SUBLANES, RG_WIDTH))

    h0 = jnp.broadcast_to(st_ref[0, 0:1, :], (SUBLANES, RG_WIDTH))
    h_fin = lax.fori_loop(0, t_len // SUBLANES, step, h0)
    hl_ref[0] = h_fin[0:1, :]
    nb_ref[0] = xs[RG_STATE_ROWS + t_len - (RG_CONV - 1):RG_STATE_ROWS + t_len, :]
    for c in range(t_len // ch):
        r0 = c * ch
        out_ref[r0:r0 + ch, :] = b_s[r0:r0 + ch, :] * jax.nn.gelu(ry_ref[r0:r0 + ch, :])


def _rglru(rx, ry, st, cw, cb, wa, ba, wx, bx, lam, bsz, t):
    row = pl.BlockSpec((t, RG_WIDTH), lambda b: (b, 0))
    vec = _const_spec((1, RG_WIDTH))
    return pl.pallas_call(
        functools.partial(_rglru_kernel, t_len=t), grid=(bsz,),
        in_specs=[row, row, pl.BlockSpec((1, RG_STATE_ROWS, RG_WIDTH), lambda b: (b, 0, 0)),
                  _const_spec(cw.shape), vec, _const_spec(wa.shape), vec, _const_spec(wx.shape), vec, vec],
        out_specs=[row, pl.BlockSpec((1, 1, RG_WIDTH), lambda b: (b, 0, 0)),
                   pl.BlockSpec((1, RG_CONV - 1, RG_WIDTH), lambda b: (b, 0, 0))],
        out_shape=[jax.ShapeDtypeStruct((bsz * t, RG_WIDTH), F32),
                   jax.ShapeDtypeStruct((bsz, 1, RG_WIDTH), F32),
                   jax.ShapeDtypeStruct((bsz, RG_CONV - 1, RG_WIDTH), F32)],
        scratch_shapes=[pltpu.VMEM((RG_STATE_ROWS + t, RG_WIDTH), F32),
                        pltpu.VMEM((t, RG_WIDTH), F32), pltpu.VMEM((t, RG_WIDTH), F32)],
        compiler_params=_cparams(1), name="rglru")(rx, ry, st, cw, cb, wa, ba, wx, bx, lam)


def _proj_ln_kernel(*refs, n_in):
    a_refs = refs[:n_in]
    w_refs = refs[n_in:2 * n_in]
    x_ref, mod_ref, bias_ref, g_ref, b_ref, o_ref = refs[2 * n_in:]
    out = bias_ref[...]
    for a_ref, w_ref in zip(a_refs, w_refs):
        out = out + _dot(a_ref[...].astype(BF16), w_ref[...])
    o_ref[...] = _post_norm(x_ref[...], mod_ref[0], out, g_ref[...], b_ref[...])


def _proj_ln(a_list, w_list, x2d, mod3, bias, g, b, tm, tps):
    m = x2d.shape[0]
    n_in = len(a_list)
    row = lambda width: pl.BlockSpec((tm, width), lambda i: (i, 0))
    vec = _const_spec((1, D_MODEL))
    return pl.pallas_call(
        functools.partial(_proj_ln_kernel, n_in=n_in), grid=(m // tm,),
        in_specs=[row(a.shape[1]) for a in a_list] + [_const_spec(w.shape) for w in w_list]
        + [row(D_MODEL), _mod_spec(mod3, tm, tps), vec, vec, vec],
        out_specs=row(D_MODEL),
        out_shape=jax.ShapeDtypeStruct((m, D_MODEL), F32),
        compiler_params=_cparams(1), name="proj_postnorm")(*a_list, *w_list, x2d, mod3, bias, g, b)


def _ffn_kernel(x_ref, mod_ref, w1_ref, w3_ref, w2_ref, g_ref, b_ref, o_ref, *, ff):
    x = x_ref[...]
    mod = mod_ref[0]
    h = _modulate(x, mod).astype(BF16)
    acc = jnp.zeros(x.shape, F32)
    for c in range(ff // FF_CHUNK):
        cs = slice(c * FF_CHUNK, (c + 1) * FF_CHUNK)
        z = _silu(_dot(h, w1_ref[:, cs])) * _dot(h, w3_ref[:, cs])
        acc = acc + _dot(z.astype(BF16), w2_ref[cs, :])
    o_ref[...] = _post_norm(x, mod, acc, g_ref[...], b_ref[...])


def _ffn(x2d, mod3, w1, w3, w2, g, b, tm, tps):
    m = x2d.shape[0]
    row = pl.BlockSpec((tm, D_MODEL), lambda i: (i, 0))
    vec = _const_spec((1, D_MODEL))
    return pl.pallas_call(
        functools.partial(_ffn_kernel, ff=w1.shape[1]), grid=(m // tm,),
        in_specs=[row, _mod_spec(mod3, tm, tps), _const_spec(w1.shape, True), _const_spec(w3.shape, True),
                  _const_spec(w2.shape, True), vec, vec],
        out_specs=row,
        out_shape=jax.ShapeDtypeStruct((m, D_MODEL), F32),
        compiler_params=_cparams(1), name="dense_ffn")(x2d, mod3, w1, w3, w2, g, b)


def _cf_in_kernel(x_ref, mod_ref, w_ref, b_ref, o_ref):
    h = _modulate(x_ref[...], mod_ref[0]).astype(BF16)
    u = _dot(h, w_ref[...]) + b_ref[...]
    o_ref[...] = u[:, :D_MODEL] * jax.nn.sigmoid(u[:, D_MODEL:])


def _cf_in(x2d, mod3, w, bias, tm, tps):
    m = x2d.shape[0]
    row = pl.BlockSpec((tm, D_MODEL), lambda i: (i, 0))
    return pl.pallas_call(
        _cf_in_kernel, grid=(m // tm,),
        in_specs=[row, _mod_spec(mod3, tm, tps), _const_spec(w.shape), _const_spec(bias.shape)],
        out_specs=row,
        out_shape=jax.ShapeDtypeStruct((m, D_MODEL), F32),
        compiler_params=_cparams(1), name="conformer_in")(x2d, mod3, w, bias)


def _cf_conv_kernel(x_ref, halo_ref, dw_ref, db_ref, g_ref, b_ref, z_ref, s_ref, *, tt, zero_first):
    halo = halo_ref[...]
    if zero_first:
        halo = jnp.where(pl.program_id(1) == 0, 0.0, halo)
    s_ref[0, 0:CF_HALO, :] = halo
    s_ref[0, CF_HALO:CF_HALO + tt, :] = x_ref[...]
    n_sh = CF_HALO + tt - SUBLANES
    for k in range(1, SUBLANES):
        s_ref[k, 0:n_sh, :] = s_ref[0, k:k + n_sh, :]
    sub = min(CONV_SUB, tt)
    first = CF_HALO - (CF_KERNEL - 1)
    for c in range(tt // sub):
        r0 = c * sub
        y = jnp.broadcast_to(db_ref[...], (sub // SUBLANES, SUBLANES, D_MODEL))
        for j in range(CF_KERNEL):
            k = (first + j) % SUBLANES
            a0 = r0 + first + j - k
            y = y + dw_ref[j] * s_ref[k, a0:a0 + sub, :].reshape(sub // SUBLANES, SUBLANES, D_MODEL)
        z_ref[r0:r0 + sub, :] = _silu(_layer_norm(y.reshape(sub, D_MODEL), g_ref[...], b_ref[...]))


def _cf_conv(glu, halo_src, halo_map, dw, db, g, b, bsz, t, tt, zero_first):
    nt = t // tt
    vec = _const_spec((1, D_MODEL))
    return pl.pallas_call(
        functools.partial(_cf_conv_kernel, tt=tt, zero_first=zero_first), grid=(bsz, nt),
        in_specs=[pl.BlockSpec((tt, D_MODEL), lambda bi, ti: (bi * nt + ti, 0)),
                  pl.BlockSpec((CF_HALO, D_MODEL), halo_map),
                  _const_spec(dw.shape), vec, vec, vec],
        out_specs=pl.BlockSpec((tt, D_MODEL), lambda bi, ti: (bi * nt + ti, 0)),
        out_shape=jax.ShapeDtypeStruct((bsz * t, D_MODEL), F32),
        scratch_shapes=[pltpu.VMEM((SUBLANES, CF_HALO + tt, D_MODEL), F32)],
        compiler_params=_cparams(2), name="conformer_conv")(glu, halo_src, dw, db, g, b)


def _router_kernel(x_ref, mod_ref, rw_ref, h_ref, ids_ref, gates_ref, *, n_real):
    h = _modulate(x_ref[...], mod_ref[0])
    hz = jnp.where(pl.program_id(0) < n_real, h, 0.0)
    for s in range(ROW_PARTS):
        h_ref[pl.ds(s, h.shape[0], stride=ROW_PARTS), :] = hz[:, s * LANES:(s + 1) * LANES]
    logits = lax.dot_general(h, rw_ref[...], (((1,), (0,)), ((), ())), precision=lax.Precision.HIGHEST,
                             preferred_element_type=F32)
    lane = lax.broadcasted_iota(I32, logits.shape, 1)
    logits = jnp.where(lane < N_EXPERTS, logits, -jnp.inf)
    m1 = jnp.max(logits, axis=-1, keepdims=True)
    i1 = jnp.min(jnp.where(logits == m1, lane, LANES), axis=-1, keepdims=True)
    rest = jnp.where(lane == i1, -jnp.inf, logits)
    m2 = jnp.max(rest, axis=-1, keepdims=True)
    i2 = jnp.min(jnp.where(rest == m2, lane, LANES), axis=-1, keepdims=True)
    e2 = jnp.exp(m2 - m1)
    inv = 1.0 / (1.0 + e2)
    col = lax.broadcasted_iota(I32, ids_ref.shape, 1)
    ids_ref[...] = jnp.where(col == 0, i1, jnp.where(col == 1, i2, 0))
    gates_ref[...] = jnp.where(col == 0, inv, jnp.where(col == 1, e2 * inv, 0.0))


def _router(x2d, mod3, rw, tm, tps, n_total, row_off, h_prev):
    m = x2d.shape[0]
    n_real = m // tm
    off = row_off // tm
    if h_prev is None:
        steps = -(-n_total // tm)
        clamp = lambda i: jnp.minimum(i, n_real - 1)
    else:
        steps = n_real
        clamp = lambda i: i
    if mod3.shape[1] == 1:
        mspec = pl.BlockSpec((1, 1, mod3.shape[2]), lambda i: (clamp(i) // tps, 0, 0))
    else:
        mspec = pl.BlockSpec((1, tm, mod3.shape[2]), lambda i: (0, clamp(i), 0))
    small = lambda: pl.BlockSpec((tm, SUBLANES), lambda i: (clamp(i), 0))
    in_specs = [pl.BlockSpec((tm, D_MODEL), lambda i: (clamp(i), 0)), mspec, _const_spec(rw.shape)]
    args = [x2d, mod3, rw]
    kern = functools.partial(_router_kernel, n_real=n_real)
    aliases = {}
    if h_prev is not None:
        in_specs.append(pl.BlockSpec(memory_space=pl.ANY))
        args.append(h_prev)
        aliases = {3: 0}
        kern = lambda x, md, rw_, hp, h, ids, gt: _router_kernel(x, md, rw_, h, ids, gt, n_real=n_real)
    return pl.pallas_call(
        kern, grid=(steps,), in_specs=in_specs,
        out_specs=[pl.BlockSpec((tm * ROW_PARTS, LANES), lambda i: (i + off, 0)), small(), small()],
        out_shape=[jax.ShapeDtypeStruct((n_total * ROW_PARTS, LANES), F32), jax.ShapeDtypeStruct((m, SUBLANES), I32),
                   jax.ShapeDtypeStruct((m, SUBLANES), F32)],
        input_output_aliases=aliases,
        compiler_params=_cparams(1), name="moe_router")(*args)


ROW_SRC_BITS = 15
ROW_PARTS = D_MODEL // LANES


DMA_GROUP = 8


def _for_rows(n, fn):
    full = n // DMA_GROUP

    def group(gi, c):
        for u in range(DMA_GROUP):
            fn(gi * DMA_GROUP + u)
        return c
    lax.fori_loop(0, full, group, 0)
    if isinstance(n, int):
        for r in range(full * DMA_GROUP, n):
            fn(r)
    else:
        for u in range(DMA_GROUP - 1):
            r = full * DMA_GROUP + u

            @pl.when(r < n)
            def _():
                fn(r)


def _experts_kernel(blk_e_ref, plan_ref, cnt_ref, nused_ref, h_ref, w1_ref, w3_ref, w2_ref, y_ref,
                    xs, xb, acc, ybuf, gsem, ssem, *, n_dst):
    i = pl.program_id(0)
    j = pl.program_id(1)
    nused = nused_ref[0]
    used = i < nused
    last = j == pl.num_programs(1) - 1
    src_mask = (1 << ROW_SRC_BITS) - 1
    rows_per_step = MOE_TILE // MOE_DMA_STEPS

    def tile_rows(row):
        return pl.ds(pl.multiple_of(row * ROW_PARTS, ROW_PARTS), ROW_PARTS)

    def gather_row(blk, r):
        tok = plan_ref[blk * MOE_TILE + r] & src_mask
        pltpu.make_async_copy(h_ref.at[tile_rows(tok), :], xs.at[blk % 2, tile_rows(r), :], gsem.at[blk % 2]).start()

    def wait_gather(slot):
        pltpu.make_async_copy(h_ref.at[pl.ds(0, MOE_TILE * ROW_PARTS), :], xs.at[slot], gsem.at[slot]).wait()

    def scatter_row(blk, n_valid, r):
        dst = plan_ref[blk * MOE_TILE + r] >> ROW_SRC_BITS
        dst = jnp.where(r < n_valid, dst, n_dst + r)
        pltpu.make_async_copy(ybuf.at[tile_rows(r), :], y_ref.at[tile_rows(dst), :], ssem).start()

    def wait_scatter():
        pltpu.make_async_copy(ybuf, y_ref.at[pl.ds(0, MOE_TILE * ROW_PARTS), :], ssem).wait()

    def part(s):
        return pl.ds(s, MOE_TILE, stride=ROW_PARTS)

    @pl.when(used & (j == 0))
    def _():
        @pl.when(i == 0)
        def _():
            _for_rows(MOE_TILE, lambda r: gather_row(i, r))
            ybuf[...] = jnp.zeros(ybuf.shape, F32)
        wait_gather(i % 2)
        for s in range(ROW_PARTS):
            xb[:, s * LANES:(s + 1) * LANES] = xs[i % 2, part(s), :].astype(BF16)
        acc[...] = jnp.zeros(acc.shape, F32)

    def compute():
        x = xb[...]
        z = _silu(_dot(x, w1_ref[...])) * _dot(x, w3_ref[...])
        acc[...] += _dot(z.astype(BF16), w2_ref[...])

    @pl.when(used & (j < MOE_DMA_STEPS))
    def _():
        nxt = jnp.minimum(i + 1, nused - 1)
        prev = jnp.maximum(i - 1, 0)
        n_prev = jnp.where(i >= 1, cnt_ref[prev], 0)
        r0 = j * rows_per_step
        for u in range(rows_per_step):
            gather_row(nxt, r0 + u)
            scatter_row(prev, n_prev, r0 + u)
        compute()

    @pl.when(used & (j >= MOE_DMA_STEPS))
    def _():
        compute()

    @pl.when(used & last)
    def _():
        wait_scatter()
        for s in range(ROW_PARTS):
            ybuf[part(s), :] = acc[:, s * LANES:(s + 1) * LANES]

        @pl.when(i == nused - 1)
        def _():
            _for_rows(MOE_TILE, lambda r: scatter_row(i, cnt_ref[i], r))
            wait_scatter()
            wait_gather(i % 2)


def _experts(blk_e, plan, cnt, nused, h_all, n_tok, w1, w3, w2, layer, n_blk):
    ff = w1.shape[3]
    nj = ff // MOE_FF_CHUNK

    def jj(i, j, nu):
        return jnp.where(i < nu[0], j, nj - 1)

    wspec = lambda shape, imap: pl.BlockSpec(shape, imap)
    up = (None, None, D_MODEL, MOE_FF_CHUNK)
    grid_spec = pltpu.PrefetchScalarGridSpec(
        num_scalar_prefetch=4, grid=(n_blk, nj),
        in_specs=[pl.BlockSpec(memory_space=pl.ANY),
                  wspec(up, lambda i, j, be, pn, ct, nu: (layer, be[i], 0, jj(i, j, nu))),
                  wspec(up, lambda i, j, be, pn, ct, nu: (layer, be[i], 0, jj(i, j, nu))),
                  wspec((None, None, MOE_FF_CHUNK, D_MODEL),
                        lambda i, j, be, pn, ct, nu: (layer, be[i], jj(i, j, nu), 0))],
        out_specs=pl.BlockSpec(memory_space=pl.ANY),
        scratch_shapes=[pltpu.VMEM((2, MOE_TILE * ROW_PARTS, LANES), F32), pltpu.VMEM((MOE_TILE, D_MODEL), BF16),
                        pltpu.VMEM((MOE_TILE, D_MODEL), F32), pltpu.VMEM((MOE_TILE * ROW_PARTS, LANES), F32),
                        pltpu.SemaphoreType.DMA((2,)), pltpu.SemaphoreType.DMA(())])
    assert nj == MOE_STEPS and ff == nj * MOE_FF_CHUNK
    return pl.pallas_call(
        functools.partial(_experts_kernel, n_dst=2 * n_tok), grid_spec=grid_spec,
        out_shape=jax.ShapeDtypeStruct(((2 * n_tok + MOE_TILE) * ROW_PARTS, LANES), F32),
        compiler_params=_cparams(2, disable_bounds_checks=True),
        name="moe_experts")(blk_e, plan, cnt, nused, h_all, w1, w3, w2)


def _combine_kernel(y0_ref, y1_ref, gates_ref, x_ref, mod_ref, g_ref, b_ref, o_ref):
    tm = x_ref.shape[0]

    def rows(y_ref):
        return jnp.concatenate([y_ref[pl.ds(s, tm, stride=ROW_PARTS), :] for s in range(ROW_PARTS)], axis=1)

    gates = gates_ref[...]
    out = gates[:, 0:1] * rows(y0_ref) + gates[:, 1:2] * rows(y1_ref)
    o_ref[...] = _post_norm(x_ref[...], mod_ref[0], out, g_ref[...], b_ref[...])


def _combine(ys, gates, x2d, mod3, g, b, tm, tps, tok_off, n_tok):
    m = x2d.shape[0]
    off0 = tok_off // tm
    off1 = (n_tok + tok_off) // tm
    row = lambda width: pl.BlockSpec((tm, width), lambda i: (i, 0))
    vec = _const_spec((1, D_MODEL))
    return pl.pallas_call(
        _combine_kernel, grid=(m // tm,),
        in_specs=[pl.BlockSpec((tm * ROW_PARTS, LANES), lambda i: (i + off0, 0)),
                  pl.BlockSpec((tm * ROW_PARTS, LANES), lambda i: (i + off1, 0)),
                  row(SUBLANES), row(D_MODEL), _mod_spec(mod3, tm, tps), vec, vec],
        out_specs=row(D_MODEL),
        out_shape=jax.ShapeDtypeStruct((m, D_MODEL), F32),
        compiler_params=_cparams(1), name="moe_combine")(ys, ys, gates, x2d, mod3, g, b)


def _route_plan(ids_all, n_tok):
    fe = ids_all[:, :2].reshape(-1)
    onehot = (fe[:, None] == jnp.arange(N_EXPERTS, dtype=I32)[None, :]).astype(I32)
    csum = jnp.cumsum(onehot, axis=0)
    rank = jnp.take_along_axis(csum, fe[:, None], axis=1)[:, 0] - 1
    counts = csum[-1]
    padded = (counts + MOE_TILE - 1) // MOE_TILE * MOE_TILE
    pend = jnp.cumsum(padded)
    pstart = pend - padded
    dest = (pstart[fe] + rank).astype(I32)
    n_blk = -(-(2 * n_tok) // MOE_TILE) + N_EXPERTS
    a = jnp.arange(2 * n_tok, dtype=I32)
    word = (a // 2) | (((a % 2) * n_tok + a // 2) << ROW_SRC_BITS)
    plan = jnp.zeros((n_blk * MOE_TILE,), I32).at[dest].set(word)
    nused = (pend[-1] // MOE_TILE).astype(I32)
    blk = jnp.arange(n_blk, dtype=I32)
    first_row = jnp.minimum(blk, nused - 1) * MOE_TILE
    blk_e = jnp.sum((first_row[:, None] >= pend[None, :]).astype(I32), axis=1)
    blk_e = jnp.minimum(blk_e, N_EXPERTS - 1)
    cnt = jnp.clip(pstart[blk_e] + counts[blk_e] - blk * MOE_TILE, 0, MOE_TILE)
    cnt = jnp.where(blk < nused, cnt, 0).astype(I32)
    return plan, blk_e, cnt, nused.reshape(1), n_blk


def _rope_tables(pos):
    half = HEAD_DIM // 2
    inv = 1.0 / (ROPE_THETA ** (jnp.arange(half, dtype=F32) * (2.0 / HEAD_DIM)))
    ang = pos.astype(F32)[:, None] * inv[None, :]
    c = jnp.cos(ang)
    s = jnp.sin(ang)
    return jnp.concatenate([c, c, c, c], axis=1), jnp.concatenate([-s, s, -s, s], axis=1)


def _cmp_rope_tables(nc):
    blk = jnp.concatenate([jnp.arange(0, nc, 2), jnp.arange(1, nc, 2)])
    return _rope_tables(blk * CMP_BLOCK + (CMP_BLOCK - 1))


def _w_in_layout(w_in):
    o1 = Q_COLS
    o2 = o1 + KV_COLS
    o3 = o2 + GATE_COLS
    pad = jnp.zeros((D_MODEL, LANES - GATE_COLS), w_in.dtype)
    return jnp.concatenate([w_in[:, :o2], w_in[:, o3:], w_in[:, o2:o3], pad], axis=1).astype(BF16)


def _block_diag(blocks):
    n, a, b = blocks.shape
    eye = jnp.eye(n, dtype=blocks.dtype)
    return (eye[:, None, :, None] * blocks[:, :, None, :]).reshape(n * a, n * b)


def _cmp_weights(w_ck, w_cv, pe_k, pe_v):
    wk = w_ck.reshape(CMP_BLOCK, HEAD_DIM, HEAD_DIM)
    wv = w_cv.reshape(CMP_BLOCK, HEAD_DIM, HEAD_DIM)
    w4 = jax.vmap(lambda a, b: _block_diag(jnp.stack([a, a, b, b])))(wk, wv).astype(BF16)
    pe4 = jnp.concatenate([pe_k, pe_k, pe_v, pe_v], axis=1)
    return w4, pe4


def kernel(x_prompt, x_sample, c_prompt, c_sample, page_table, cache_nsa_kv, state_nsa_win, state_rglru_h, state_rglru_conv, state_conformer_conv, w_mod, b_mod, ln_g, ln_b, w_in_even, w_out_even, w_cmp_k, w_cmp_v, pe_cmp_k, pe_cmp_v, rg_conv_w, rg_conv_b, rg_wa, rg_ba, rg_wx, rg_bx, rg_lam, cf_w1, cf_b1, cf_dw, cf_db, cf_ln_g, cf_ln_b, cf_w2, cf_b2, ff_w1, ff_w3, ff_w2, moe_router, moe_w1, moe_w3, moe_w2):
    bp, tp, d = x_prompt.shape
    bs, ts, _ = x_sample.shape
    npages = page_table.shape[1]
    past = npages * PAGE_SIZE
    n_p = bp * tp
    n_s = bs * ts
    n_tok = n_p + n_s
    assert d == D_MODEL and tp % ROW_TILE == 0 and tp % Q_TILE == 0 and tp >= WINDOW
    assert past % SEL_BLOCK == 0 and ts <= CMP_BLOCK and ts % SUBLANES == 0 and n_s % SUBLANES == 0
    assert n_p % MOE_TILE == 0 and n_p % n_s == 0
    tps_p = tp // ROW_TILE
    xp = x_prompt.reshape(n_p, d)
    xs = x_sample.reshape(n_s, d)
    vec = lambda v: v.reshape(1, -1)

    mod_all = _mod_all(jnp.concatenate([c_prompt, c_sample], axis=0), w_mod, b_mod)

    def mods(l, s):
        mrow = mod_all[2 * l + s]
        return mrow[:bp].reshape(bp, 1, 3 * d), jnp.repeat(mrow[bp:], ts, axis=0).reshape(1, n_s, 3 * d)

    cos_p, sin_p = _rope_tables(jnp.arange(tp))
    cos_s, sin_s = _rope_tables(jnp.tile(past + jnp.arange(ts), bs))
    ccos_p, csin_p = _cmp_rope_tables(tp // CMP_BLOCK)
    ccos_s, csin_s = _cmp_rope_tables(past // CMP_BLOCK)
    pt_flat = page_table.reshape(-1).astype(I32)
    n_even = cache_nsa_kv.shape[0]
    win_len = state_nsa_win.shape[2]
    cache_fm = cache_nsa_kv.transpose(0, 1, 3, 4, 5, 2).reshape(n_even, cache_nsa_kv.shape[1], 4 * LANES, PAGE_SIZE)
    win_fm = state_nsa_win.transpose(0, 1, 3, 4, 5, 2).reshape(n_even, bs, 2 * LANES, win_len)
    tm_c = min(Q_TILE, n_s)
    assert n_p % tm_c == 0 and n_tok % tm_c == 0
    assert n_tok < (1 << ROW_SRC_BITS) and 2 * n_tok < (1 << (31 - ROW_SRC_BITS))

    moe_w1_bf, moe_w3_bf, moe_w2_bf = moe_w1.astype(BF16), moe_w3.astype(BF16), moe_w2.astype(BF16)

    kv_p, kv_s, win_p, win_s, rh_p, rh_s, rc_p, rc_s, cc_p, cc_s = ([] for _ in range(10))
    for l in range(DEPTH):
        i = l // 2
        mp0, ms0 = mods(l, 0)
        mp1, ms1 = mods(l, 1)
        g0, b0, g1, b1 = vec(ln_g[l, 0]), vec(ln_b[l, 0]), vec(ln_g[l, 1]), vec(ln_b[l, 1])
        zero_bias = jnp.zeros((1, d), F32)
        if l % 2 == 0:
            w_in = _w_in_layout(w_in_even[i])
            w4, pe4 = _cmp_weights(w_cmp_k[i], w_cmp_v[i], pe_cmp_k[i], pe_cmp_v[i])
            wo_a = w_out_even[i][:NSA_WIDTH].astype(BF16)
            wo_r = w_out_even[i][NSA_WIDTH:].astype(BF16)
            wa = _block_diag(rg_wa[i]).astype(BF16)
            wx = _block_diag(rg_wx[i]).astype(BF16)
            rg_args = (rg_conv_w[i], vec(rg_conv_b[i]), wa, vec(rg_ba[i]), wx, vec(rg_bx[i]), vec(rg_lam[i]))

            q, kv4, wr, rx, ry, gl, kvt, wrt = _win_call(xp, mp0, w_in, cos_p, sin_p, ROW_TILE, tps_p, tp)
            cmp = _compress_prompt(kv4, bp, tp, w4, pe4, ccos_p, csin_p)
            o = _pattn(q, gl, cmp, kv4, kvt, wr, wrt, bp, tp)
            st = jnp.zeros((bp, RG_STATE_ROWS, RG_WIDTH), F32)
            rg, h_last, new_buf = _rglru(rx, ry, st, *rg_args, bp, tp)
            xp = _proj_ln([o, rg], [wo_a, wo_r], xp, mp0, zero_bias, g0, b0, ROW_TILE, tps_p)
            kv_p.append(kvt.reshape(bp, 4, N_KV_A, HEAD_DIM, tp).transpose(0, 4, 1, 2, 3))
            win_p.append(wrt.reshape(bp, 2, N_KV_A, HEAD_DIM, tp)[..., tp - min(WINDOW, tp):]
                         .transpose(0, 4, 1, 2, 3))
            rh_p.append(h_last.reshape(bp, RG_WIDTH))
            rc_p.append(new_buf)

            q, kv4, wr, rx, ry, gl = _win_call(xs, ms0, w_in, cos_s, sin_s, n_s, 1)
            cmp = _compress_sample(pt_flat, cache_fm, i, bs, npages, w4, pe4, ccos_s, csin_s)
            o = _sattn(pt_flat, cache_fm, i, q, gl, cmp, kv4, wr, win_fm, bs, ts, npages)
            st = jnp.concatenate([state_rglru_h[i][:, None, :],
                                  jnp.zeros((bs, RG_STATE_ROWS - RG_CONV, RG_WIDTH), F32),
                                  state_rglru_conv[i]], axis=1)
            rg, h_last, new_buf = _rglru(rx, ry, st, *rg_args, bs, ts)
            xs = _proj_ln([o, rg], [wo_a, wo_r], xs, ms0, zero_bias, g0, b0, n_s, 1)
            kv_s.append(kv4.reshape(bs, ts, 4, N_KV_A, HEAD_DIM))
            wfull = jnp.concatenate([state_nsa_win[i], wr.reshape(bs, ts, 2, N_KV_A, HEAD_DIM)], axis=1)
            win_s.append(wfull[:, wfull.shape[1] - min(WINDOW, wfull.shape[1]):])
            rh_s.append(h_last.reshape(bs, RG_WIDTH))
            rc_s.append(new_buf)

            w1, w3, w2 = ff_w1[i].astype(BF16), ff_w3[i].astype(BF16), ff_w2[i].astype(BF16)
            xp = _ffn(xp, mp1, w1, w3, w2, g1, b1, ROW_TILE, tps_p)
            xs = _ffn(xs, ms1, w1, w3, w2, g1, b1, n_s, 1)
        else:
            cw1 = cf_w1[i].astype(BF16)
            cw2 = cf_w2[i].astype(BF16)
            dw = jnp.broadcast_to(cf_dw[i][:, None, :], (CF_KERNEL, SUBLANES, d))
            conv_args = (dw, vec(cf_db[i]), vec(cf_ln_g[i]), vec(cf_ln_b[i]))

            glu = _cf_in(xp, mp0, cw1, vec(cf_b1[i]), ROW_TILE, tps_p)
            per = CONV_TILE // CF_HALO
            z = _cf_conv(glu, glu, lambda bi, ti: (jnp.maximum((bi * (tp // CONV_TILE) + ti) * per - 1, 0), 0),
                         *conv_args, bp, tp, CONV_TILE, True)
            xp = _proj_ln([z], [cw2], xp, mp0, vec(cf_b2[i]), g0, b0, ROW_TILE, tps_p)
            cc_p.append(glu.reshape(bp, tp, d)[:, tp - (CF_KERNEL - 1):])

            glu = _cf_in(xs, ms0, cw1, vec(cf_b1[i]), n_s, 1)
            halo = jnp.concatenate([jnp.zeros((bs, CF_HALO - (CF_KERNEL - 1), d), F32),
                                    state_conformer_conv[i]], axis=1).reshape(bs * CF_HALO, d)
            z = _cf_conv(glu, halo, lambda bi, ti: (bi, 0), *conv_args, bs, ts, ts, False)
            xs = _proj_ln([z], [cw2], xs, ms0, vec(cf_b2[i]), g0, b0, n_s, 1)
            cc_s.append(jnp.concatenate([state_conformer_conv[i], glu.reshape(bs, ts, d)],
                                        axis=1)[:, -(CF_KERNEL - 1):])

            rw = jnp.concatenate([moe_router[i], jnp.zeros((d, LANES - N_EXPERTS), F32)], axis=1)
            h_all, ids_p, gates_p = _router(xp, mp1, rw, ROW_TILE, tps_p, n_tok, 0, None)
            h_all, ids_s, gates_s = _router(xs, ms1, rw, n_s, 1, n_tok, n_p, h_all)
            plan, blk_e, cnt, nused, n_blk = _route_plan(jnp.concatenate([ids_p, ids_s], axis=0), n_tok)
            ys = _experts(blk_e, plan, cnt, nused, h_all, n_tok, moe_w1_bf, moe_w3_bf, moe_w2_bf, i, n_blk)
            xp = _combine(ys, gates_p, xp, mp1, g1, b1, tm_c, tp // tm_c, 0, n_tok)
            xs = _combine(ys, gates_s, xs, ms1, g1, b1, tm_c, 1, n_p, n_tok)
    return (xp.reshape(bp, tp, d), xs.reshape(bs, ts, d), jnp.stack(kv_p), jnp.stack(kv_s),
            jnp.stack(win_p), jnp.stack(win_s), jnp.stack(rh_p), jnp.stack(rh_s),
            jnp.stack(rc_p), jnp.stack(rc_s), jnp.stack(cc_p), jnp.stack(cc_s))
```

```python
import functools

import jax
import jax.numpy as jnp
from jax import lax
from jax.experimental import pallas as pl
from jax.experimental.pallas import tpu as pltpu

F32 = jnp.float32
BF16 = jnp.bfloat16
I32 = jnp.int32

D_MODEL = 1024
HEAD_DIM = 64
N_HEADS_A = 8
N_KV_A = 2
HEADS_PER_GROUP = N_HEADS_A // N_KV_A
GROUP_COLS = N_KV_A * HEAD_DIM
CMP_BLOCK = 32
SEL_BLOCK = 64
TOP_N = 16
WINDOW = 512
ROPE_THETA = 10000.0
PAGE_SIZE = 128
RG_WIDTH = 512
RG_BLOCKS = 8
RG_CONV = 4
RG_C = 8.0
NSA_WIDTH = 512
Q_COLS = NSA_WIDTH
KV_COLS = 6 * GROUP_COLS
GATE_COLS = 3 * N_HEADS_A
CF_KERNEL = 31
CF_HALO = 32
N_EXPERTS = 8
DEPTH = 4
ALPHA = (2.0 * DEPTH) ** 0.25
LN_EPS = 1e-5
NEG = -1e30
FORCE_SCORE = 1e4
ATTN_SCALE = HEAD_DIM ** -0.5

LANES = 128
SUBLANES = 8
VMEM_LIMIT = 56 * 1024 * 1024

ROW_TILE = 512
Q_TILE = 256
CONV_TILE = 256
CONV_SUB = 32
MOE_TILE = 512
FF_CHUNK = 256
MOE_FF_CHUNK = 896
MOE_STEPS = 4
MOE_DMA_STEPS = 2


def _cparams(n_axes, **kw):
    return pltpu.CompilerParams(dimension_semantics=("arbitrary",) * n_axes,
                                vmem_limit_bytes=VMEM_LIMIT, **kw)


def _const_spec(shape, single_buffer=False):
    n = len(shape)
    if single_buffer:
        return pl.BlockSpec(shape, lambda *a: (0,) * n, pipeline_mode=pl.Buffered(1))
    return pl.BlockSpec(shape, lambda *a: (0,) * n)


def _mod_spec(mod3, tm, tps):
    if mod3.shape[1] == 1:
        return pl.BlockSpec((1, 1, mod3.shape[2]), lambda i: (i // tps, 0, 0))
    return pl.BlockSpec((1, tm, mod3.shape[2]), lambda i: (0, i, 0))


BF16_SUBLANES = 16


def _matmul_input_dtype(block_rows):
    return BF16 if block_rows % BF16_SUBLANES == 0 else F32


def _dot(a, b):
    return jnp.dot(a, b, preferred_element_type=F32)


def _dot_nt(a, b):
    return lax.dot_general(a, b, (((1,), (1,)), ((), ())), preferred_element_type=F32)


def _layer_norm(y, g, b):
    mu = jnp.mean(y, axis=-1, keepdims=True)
    yc = y - mu
    var = jnp.mean(yc * yc, axis=-1, keepdims=True)
    return yc * lax.rsqrt(var + LN_EPS) * g + b


def _post_norm(x, mod, out, g, b):
    gate = mod[:, 2 * D_MODEL:]
    return _layer_norm(ALPHA * x + (1.0 + gate) * out, g, b)


def _modulate(x, mod):
    return x * (1.0 + mod[:, D_MODEL:2 * D_MODEL]) + mod[:, :D_MODEL]


def _silu(x):
    return x * jax.nn.sigmoid(x)


def _rope128(v, cos, sin_signed):
    lane = lax.broadcasted_iota(I32, v.shape, 1)
    from_hi = pltpu.roll(v, LANES - HEAD_DIM // 2, 1)
    from_lo = pltpu.roll(v, HEAD_DIM // 2, 1)
    swapped = jnp.where((lane & (HEAD_DIM - 1)) < HEAD_DIM // 2, from_hi, from_lo)
    return v * cos + swapped * sin_signed


def _rope(v, cos, sin_signed):
    k = v.shape[1] // LANES
    parts = [_rope128(v[:, i * LANES:(i + 1) * LANES], cos, sin_signed) for i in range(k)]
    return parts[0] if k == 1 else jnp.concatenate(parts, axis=1)


def _softmax_parts(parts):
    masked = [jnp.where(m, s, NEG) for s, m in parts]
    mx = functools.reduce(jnp.maximum, [jnp.max(s, axis=-1, keepdims=True) for s in masked])
    es = [jnp.where(m, jnp.exp(s - mx), 0.0) for s, (_, m) in zip(masked, parts)]
    den = functools.reduce(jnp.add, [jnp.sum(e, axis=-1, keepdims=True) for e in es])
    return es, 1.0 / jnp.maximum(den, 1e-30)


def _attend(qb, pieces):
    es, inv = _softmax_parts([(_dot(qb, k) if fm else _dot_nt(qb, k), m) for k, _, m, fm in pieces])
    o = functools.reduce(jnp.add, [_dot_nt(e.astype(BF16), v) if fm else _dot(e.astype(BF16), v)
                                   for e, (_, v, _, fm) in zip(es, pieces)])
    return o * inv


def _topk_mask(vals, n_valid, kk):
    lane = lax.broadcasted_iota(I32, vals.shape, 1)
    rank = jnp.zeros(vals.shape, I32)
    for m in range(n_valid):
        col = vals[:, m:m + 1]
        later = jnp.where(lane > m, 1, 0)
        rank = rank + jnp.where(col > vals, 1, jnp.where(col == vals, later, 0))
    return (rank < kk) & (lane < n_valid)


def _mod_kernel(c_ref, w_ref, b_ref, o_ref):
    s = _silu(c_ref[...]).astype(BF16)
    o_ref[0] = _dot(s, w_ref[0].astype(BF16)) + b_ref[0]


def _mod_all(c_all, w_mod, b_mod):
    n = c_all.shape[0]
    nl = w_mod.shape[0] * w_mod.shape[1]
    w = w_mod.reshape(nl, D_MODEL, 3 * D_MODEL)
    b = b_mod.reshape(nl, 1, 3 * D_MODEL)
    return pl.pallas_call(
        _mod_kernel, grid=(nl, 3),
        in_specs=[pl.BlockSpec((n, D_MODEL), lambda l, j: (0, 0)),
                  pl.BlockSpec((1, D_MODEL, D_MODEL), lambda l, j: (l, 0, j)),
                  pl.BlockSpec((1, 1, D_MODEL), lambda l, j: (l, 0, j))],
        out_specs=pl.BlockSpec((1, n, D_MODEL), lambda l, j: (l, 0, j)),
        out_shape=jax.ShapeDtypeStruct((nl, n, 3 * D_MODEL), F32),
        compiler_params=_cparams(2), name="mod_all")(c_all, w, b)


_O_KV = Q_COLS
_O_RX = _O_KV + KV_COLS
_O_RY = _O_RX + RG_WIDTH
_O_GL = _O_RY + RG_WIDTH
W_IN_COLS = _O_GL + LANES


def _win_kernel(x_ref, mod_ref, w_ref, cos_ref, sin_ref, q_ref, kv_ref, wr_ref, rx_ref, ry_ref, gl_ref,
                *fm_refs):
    h = _modulate(x_ref[...], mod_ref[0]).astype(BF16)
    u = _dot(h, w_ref[...])
    cos = cos_ref[...]
    sin = sin_ref[...]
    q_ref[...] = _rope(u[:, :Q_COLS], cos, sin)
    c = _O_KV
    kv = jnp.concatenate([u[:, c:c + 2 * LANES],
                          _rope128(u[:, c + 2 * LANES:c + 3 * LANES], cos, sin),
                          u[:, c + 3 * LANES:c + 4 * LANES]], axis=1)
    wr = jnp.concatenate([_rope128(u[:, c + 4 * LANES:c + 5 * LANES], cos, sin),
                          u[:, c + 5 * LANES:c + 6 * LANES]], axis=1)
    kv_ref[...] = kv
    wr_ref[...] = wr
    rx_ref[...] = u[:, _O_RX:_O_RY]
    ry_ref[...] = u[:, _O_RY:_O_GL]
    gl_ref[...] = u[:, _O_GL:]
    if fm_refs:
        kvt_ref, wrt_ref = fm_refs
        kvt_ref[...] = kv.T
        wrt_ref[...] = wr.T


def _win_call(x2d, mod3, w, cos, sin, tm, tps, t_len=None):
    m = x2d.shape[0]
    row = lambda width: pl.BlockSpec((tm, width), lambda i: (i, 0))
    widths = (Q_COLS, 4 * LANES, 2 * LANES, RG_WIDTH, RG_WIDTH, LANES)
    out_specs = [row(wd) for wd in widths]
    out_shape = [jax.ShapeDtypeStruct((m, wd), F32) for wd in widths]
    if t_len is not None:
        for feat in (4 * LANES, 2 * LANES):
            out_specs.append(pl.BlockSpec((feat, tm), lambda i: (i // tps, i % tps)))
            out_shape.append(jax.ShapeDtypeStruct((m // t_len * feat, t_len), F32))
    return pl.pallas_call(
        _win_kernel, grid=(m // tm,),
        in_specs=[row(D_MODEL), _mod_spec(mod3, tm, tps), _const_spec(w.shape),
                  pl.BlockSpec((tm, LANES), lambda i: (i % tps, 0)),
                  pl.BlockSpec((tm, LANES), lambda i: (i % tps, 0))],
        out_specs=out_specs, out_shape=out_shape,
        compiler_params=_cparams(1), name="mixer_in")(x2d, mod3, w, cos, sin)


def _compress_core(k_ref, v_ref, w4_ref, pe4_ref, cos_ref, sin_ref, out_ref, nb):
    half = nb // 2
    acc = jnp.zeros((nb, 2 * LANES), F32)
    for l in range(CMP_BLOCK):
        even = pl.ds(l, half, stride=2 * CMP_BLOCK)
        odd = pl.ds(CMP_BLOCK + l, half, stride=2 * CMP_BLOCK)
        xk = jnp.concatenate([k_ref[even, :], k_ref[odd, :]], axis=0)
        xv = jnp.concatenate([v_ref[even, :], v_ref[odd, :]], axis=0)
        x = (jnp.concatenate([xk, xv], axis=1) + pe4_ref[pl.ds(l, 1), :]).astype(BF16)
        acc = acc + _dot(x, w4_ref[l])
    out_ref[:, 0:LANES] = _rope128(acc[:, 0:LANES], cos_ref[...], sin_ref[...])
    out_ref[:, LANES:2 * LANES] = acc[:, LANES:2 * LANES]


def _compress_prompt_kernel(k_ref, v_ref, w4_ref, pe4_ref, cos_ref, sin_ref, out_ref, *, nb):
    _compress_core(k_ref, v_ref, w4_ref, pe4_ref, cos_ref, sin_ref, out_ref, nb)


def _compress_prompt(kv4, bsz, t, w4, pe4, ccos, csin):
    nb = t // CMP_BLOCK
    return pl.pallas_call(
        functools.partial(_compress_prompt_kernel, nb=nb), grid=(bsz,),
        in_specs=[pl.BlockSpec((t, LANES), lambda b: (b, 0)), pl.BlockSpec((t, LANES), lambda b: (b, 1)),
                  _const_spec(w4.shape), _const_spec(pe4.shape),
                  _const_spec(ccos.shape), _const_spec(csin.shape)],
        out_specs=pl.BlockSpec((nb, 2 * LANES), lambda b: (b, 0)),
        out_shape=jax.ShapeDtypeStruct((bsz * nb, 2 * LANES), F32),
        compiler_params=_cparams(1), name="compress_prompt")(kv4, kv4, w4, pe4, ccos, csin)


def _prefetch_pages(pt_ref, cache_ref, sems, layer, feat0, npages, dst_fn):
    b = pl.program_id(0)

    def copy(seq, j):
        slot = seq % 2
        src = cache_ref.at[layer, pt_ref[seq * npages + j], pl.ds(feat0, 2 * LANES), :]
        return pltpu.make_async_copy(src, dst_fn(slot, j), sems.at[slot])

    def start_seq(seq):
        def body(j, c):
            copy(seq, j).start()
            return c
        lax.fori_loop(0, npages, body, 0)

    @pl.when(b == 0)
    def _():
        start_seq(b)

    @pl.when(b + 1 < pl.num_programs(0))
    def _():
        start_seq(b + 1)

    def wait(j, c):
        copy(b, j).wait()
        return c
    lax.fori_loop(0, npages, wait, 0)
    return b % 2


def _compress_sample_kernel(pt_ref, cache_ref, w4_ref, pe4_ref, cos_ref, sin_ref, out_ref, raw, kbuf, vbuf, sems,
                            *, layer, npages):
    def dst(slot, j):
        return raw.at[slot, pl.ds(pl.multiple_of(j * 2 * LANES, 2 * LANES), 2 * LANES), :]
    slot = _prefetch_pages(pt_ref, cache_ref, sems, layer, 0, npages, dst)

    def to_token_major(j, c):
        r0 = pl.multiple_of(j * 2 * LANES, 2 * LANES)
        t0 = pl.multiple_of(j * PAGE_SIZE, PAGE_SIZE)
        kbuf[pl.ds(t0, PAGE_SIZE), :] = raw[slot, pl.ds(r0, LANES), :].T
        vbuf[pl.ds(t0, PAGE_SIZE), :] = raw[slot, pl.ds(r0 + LANES, LANES), :].T
        return c
    lax.fori_loop(0, npages, to_token_major, 0, unroll=4)
    _compress_core(kbuf, vbuf, w4_ref, pe4_ref, cos_ref, sin_ref, out_ref, npages * PAGE_SIZE // CMP_BLOCK)


def _compress_sample(pt_flat, cache_fm, layer, bsz, npages, w4, pe4, ccos, csin):
    past = npages * PAGE_SIZE
    nb = past // CMP_BLOCK
    grid_spec = pltpu.PrefetchScalarGridSpec(
        num_scalar_prefetch=1, grid=(bsz,),
        in_specs=[pl.BlockSpec(memory_space=pl.ANY),
                  _const_spec(w4.shape), _const_spec(pe4.shape),
                  _const_spec(ccos.shape), _const_spec(csin.shape)],
        out_specs=pl.BlockSpec((nb, 2 * LANES), lambda b, pt: (b, 0)),
        scratch_shapes=[pltpu.VMEM((2, npages * 2 * LANES, PAGE_SIZE), F32),
                        pltpu.VMEM((past, LANES), F32), pltpu.VMEM((past, LANES), F32),
                        pltpu.SemaphoreType.DMA((2,))])
    return pl.pallas_call(
        functools.partial(_compress_sample_kernel, layer=layer, npages=npages),
        grid_spec=grid_spec,
        out_shape=jax.ShapeDtypeStruct((bsz * nb, 2 * LANES), F32),
        compiler_params=_cparams(1, disable_bounds_checks=True),
        name="compress_sample")(pt_flat, cache_fm, w4, pe4, ccos, csin)


def _cmp_positions(nc):
    n = lax.broadcasted_iota(I32, (1, nc), 1)
    half = nc // 2
    blk = jnp.where(n < half, 2 * n, 2 * (n - half) + 1)
    return blk * CMP_BLOCK + (CMP_BLOCK - 1)


def _softmax_cols(s, mask):
    s = jnp.where(mask, s, NEG)
    e = jnp.where(mask, jnp.exp(s - jnp.max(s, axis=0, keepdims=True)), 0.0)
    return e, 1.0 / jnp.maximum(jnp.sum(e, axis=0, keepdims=True), 1e-30)


def _pattn_kernel(q_ref, gl_ref, cmp_ref, ks_ref, vst_ref, kw0_ref, kw1_ref, kw2_ref, vwt0_ref, vwt1_ref,
                  vwt2_ref, o_ref, ks_bf, vst_bf, bias_ref, acc_ref, ot_ref, *, tq, t_len):
    ti = pl.program_id(1)
    nc = t_len // CMP_BLOCK
    nsel = t_len // SEL_BLOCK
    half = nc // 2
    blk_per_chunk = tq // SEL_BLOCK
    q0 = ti * tq

    @pl.when(ti == 0)
    def _():
        for g in range(N_KV_A):
            ks_bf[g] = ks_ref[:, g * HEAD_DIM:(g + 1) * HEAD_DIM].astype(BF16)
        vst_bf[...] = vst_ref[...].astype(BF16)

    qj = lax.broadcasted_iota(I32, (1, tq), 1)
    ki = lax.broadcasted_iota(I32, (tq, 1), 0)
    qpos = q0 + qj
    qs = (q_ref[...] * ATTN_SCALE).astype(BF16)
    gates_t = jax.nn.sigmoid(gl_ref[...]).T
    n = lax.broadcasted_iota(I32, (nc, 1), 0)
    cpos = jnp.where(n < half, 2 * n, 2 * (n - half) + 1) * CMP_BLOCK + (CMP_BLOCK - 1)
    m_c = cpos <= qpos
    blk = lax.broadcasted_iota(I32, (nsel, 1), 0)
    cur = qpos // SEL_BLOCK
    forced = (blk == cur) | (blk == 0)
    future = blk > cur
    later_blk = [jnp.where(blk > m, 1, 0) for m in range(nsel)]
    diag_bias = jnp.where(ki <= qj, 0.0, NEG)
    w_masks = []
    for k in range(3):
        dpos = qj - ki + (2 - k) * tq
        w_masks.append((dpos >= 0) & (dpos < WINDOW) & (ti + k - 2 >= 0))
    w_bias = jnp.where(jnp.concatenate(w_masks, axis=0), 0.0, NEG)
    kw_refs = (kw0_ref, kw1_ref, kw2_ref)
    vwt_refs = (vwt0_ref, vwt1_ref, vwt2_ref)

    for g in range(N_KV_A):
        ck = slice(g * HEAD_DIM, (g + 1) * HEAD_DIM)
        kc = cmp_ref[:, ck].astype(BF16)
        vc = cmp_ref[:, LANES + g * HEAD_DIM:LANES + (g + 1) * HEAD_DIM].astype(BF16)
        heads = [g * HEADS_PER_GROUP + r for r in range(HEADS_PER_GROUP)]
        q_heads = [qs[:, h * HEAD_DIM:(h + 1) * HEAD_DIM] for h in heads]
        o_cmp = []
        imp = jnp.zeros((nc, tq), F32)
        for s in [_dot_nt(kc, qh) for qh in q_heads]:
            e, inv = _softmax_cols(s, m_c)
            p = e * inv
            o_cmp.append(lax.dot_general(vc, p.astype(BF16), (((0,), (0,)), ((), ())),
                                         preferred_element_type=F32))
            imp = imp + p
        imp = imp[:half] + imp[half:]
        vals = jnp.where(forced, FORCE_SCORE, jnp.where(future, -1.0, imp))
        rank = jnp.zeros((nsel, tq), I32)
        for m in range(nsel):
            row = vals[m:m + 1, :]
            rank = rank + jnp.where(row > vals, 1, jnp.where(row == vals, later_blk[m], 0))
        bias_ref[g] = jnp.where(rank < min(TOP_N, nsel), 0.0, NEG)

        kw = jnp.concatenate([r[:, ck] for r in kw_refs], axis=0).astype(BF16)
        vwt = [r[ck, :].astype(BF16) for r in vwt_refs]

        def block_bias(c):
            rows = [jnp.broadcast_to(bias_ref[g, pl.ds(c * blk_per_chunk + j, 1), :], (SEL_BLOCK, tq))
                    for j in range(blk_per_chunk)]
            return jnp.concatenate(rows, axis=0)

        def chunk(c, carry, extra_bias=None):
            r0 = pl.multiple_of(c * tq, tq)
            k_chunk = ks_bf[g, pl.ds(r0, tq), :]
            v_chunk = vst_bf[ck, pl.ds(r0, tq)]
            bias = block_bias(c)
            if extra_bias is not None:
                bias = bias + extra_bias
            scores = [_dot_nt(k_chunk, qh) for qh in q_heads]
            out, probs, alphas = [], [], []
            for r in range(HEADS_PER_GROUP):
                m_run, l_run = carry[r]
                s = scores[r] + bias
                m_new = jnp.maximum(m_run, jnp.max(s, axis=0, keepdims=True))
                alpha = jnp.exp(m_run - m_new)
                p = jnp.exp(s - m_new)
                out.append((m_new, alpha * l_run + jnp.sum(p, axis=0, keepdims=True)))
                probs.append(p.astype(BF16))
                alphas.append(alpha)
            for r in range(HEADS_PER_GROUP):
                acc_ref[r] = alphas[r] * acc_ref[r] + _dot(v_chunk, probs[r])
            return tuple(out)

        acc_ref[...] = jnp.zeros(acc_ref.shape, F32)
        init = tuple((jnp.full((1, tq), -jnp.inf, F32), jnp.zeros((1, tq), F32)) for _ in heads)
        stats = chunk(ti, lax.fori_loop(0, ti, chunk, init), diag_bias)

        w_scores = [_dot_nt(kw, qh) + w_bias for qh in q_heads]
        w_probs = [jnp.exp(s - jnp.max(s, axis=0, keepdims=True)) for s in w_scores]
        for r, h in enumerate(heads):
            o_s = acc_ref[r] * (1.0 / jnp.maximum(stats[r][1], 1e-30))
            e = w_probs[r]
            inv = 1.0 / jnp.maximum(jnp.sum(e, axis=0, keepdims=True), 1e-30)
            e = e.astype(BF16)
            o_w = functools.reduce(jnp.add, [_dot(vwt[k], e[k * tq:(k + 1) * tq]) for k in range(3)]) * inv
            ot_ref[h * HEAD_DIM:(h + 1) * HEAD_DIM, :] = (
                gates_t[3 * h:3 * h + 1, :] * o_cmp[r] + gates_t[3 * h + 1:3 * h + 2, :] * o_s
                + gates_t[3 * h + 2:3 * h + 3, :] * o_w)
    o_ref[...] = ot_ref[...].T.astype(o_ref.dtype)


def _block_expand(nblk, nkeys):
    return (jnp.arange(nkeys)[None, :] // SEL_BLOCK == jnp.arange(nblk)[:, None]).astype(BF16)


def _pattn(q, gl, cmp, kv4, kvt, wr, wrt, bsz, t):
    tq = Q_TILE
    nt = t // tq
    nc = t // CMP_BLOCK
    nsel = t // SEL_BLOCK
    row = lambda width: pl.BlockSpec((tq, width), lambda b, i: (b * nt + i, 0))
    kw = lambda back: pl.BlockSpec((tq, LANES), lambda b, i: (b * nt + jnp.maximum(i - back, 0), 0))
    vwt = lambda back: pl.BlockSpec((LANES, tq), lambda b, i: (b * 2 + 1, jnp.maximum(i - back, 0)))
    return pl.pallas_call(
        functools.partial(_pattn_kernel, tq=tq, t_len=t), grid=(bsz, nt),
        in_specs=[row(Q_COLS), row(LANES),
                  pl.BlockSpec((nc, 2 * LANES), lambda b, i: (b, 0)),
                  pl.BlockSpec((t, LANES), lambda b, i: (b, 2)),
                  pl.BlockSpec((LANES, t), lambda b, i: (b * 4 + 3, 0)),
                  kw(2), kw(1), kw(0), vwt(2), vwt(1), vwt(0)],
        out_specs=row(NSA_WIDTH),
        out_shape=jax.ShapeDtypeStruct((bsz * t, NSA_WIDTH), _matmul_input_dtype(tq)),
        scratch_shapes=[pltpu.VMEM((N_KV_A, t, HEAD_DIM), BF16), pltpu.VMEM((LANES, t), BF16),
                        pltpu.VMEM((N_KV_A, nsel, tq), F32), pltpu.VMEM((HEADS_PER_GROUP, HEAD_DIM, tq), F32),
                        pltpu.VMEM((NSA_WIDTH, tq), F32)],
        compiler_params=_cparams(2), name="nsa_prompt")(q, gl, cmp, kv4, kvt, wr, wr, wr, wrt, wrt, wrt)


def _sattn_kernel(pt_ref, cache_ref, q_ref, gl_ref, cmp_ref, kvn_ref, wrn_ref, win_ref, e_ref, o_ref,
                  buf, sems, *, layer, npages, ts):
    past = npages * PAGE_SIZE

    def dst(slot, j):
        return buf.at[slot, :, pl.ds(pl.multiple_of(j * PAGE_SIZE, PAGE_SIZE), PAGE_SIZE)]
    slot = _prefetch_pages(pt_ref, cache_ref, sems, layer, 2 * LANES, npages, dst)
    nc = past // CMP_BLOCK
    nblk_past = past // SEL_BLOCK
    nsel = nblk_past + 1
    lanes_sel = 2 * nblk_past
    rows = HEADS_PER_GROUP * ts
    tok1 = lax.broadcasted_iota(I32, (ts, 1), 0)
    tok = jnp.concatenate([tok1] * HEADS_PER_GROUP, axis=0)
    qpos1 = past + tok1
    qpos = past + tok
    qs = q_ref[...] * ATTN_SCALE
    gates = jax.nn.sigmoid(gl_ref[...])
    m_c = _cmp_positions(nc) <= qpos
    blk = lax.broadcasted_iota(I32, (1, lanes_sel), 1)
    cur = qpos1 // SEL_BLOCK
    forced = (blk == cur) | (blk == 0)
    future = blk > cur
    m_past = lax.broadcasted_iota(I32, (1, past), 1) <= qpos
    tkey = lax.broadcasted_iota(I32, (1, ts), 1)
    m_new = tkey <= tok
    win_len = win_ref.shape[1]
    wpos = past - win_len + lax.broadcasted_iota(I32, (1, win_len), 1)
    dpos = qpos - wpos
    m_wstate = (dpos >= 0) & (dpos < WINDOW) & (wpos >= 0)
    dnew = tok - tkey
    m_wnew = (dnew >= 0) & (dnew < WINDOW)
    for g in range(N_KV_A):
        ck = slice(g * HEAD_DIM, (g + 1) * HEAD_DIM)
        cv = slice(LANES + g * HEAD_DIM, LANES + (g + 1) * HEAD_DIM)
        heads = [g * HEADS_PER_GROUP + r for r in range(HEADS_PER_GROUP)]
        qg = jnp.concatenate([qs[:, h * HEAD_DIM:(h + 1) * HEAD_DIM] for h in heads], axis=0).astype(BF16)
        kc = cmp_ref[:, ck].astype(BF16)
        vc = cmp_ref[:, cv].astype(BF16)
        (e,), inv = _softmax_parts([(_dot_nt(qg, kc), m_c)])
        p = e * inv
        o_c = _dot(p.astype(BF16), vc)
        imp = p[0:ts]
        for r in range(1, HEADS_PER_GROUP):
            imp = imp + p[r * ts:(r + 1) * ts]
        imp = imp[:, :nc // 2] + imp[:, nc // 2:]
        imp = jnp.concatenate([imp, jnp.zeros((ts, lanes_sel - nc // 2), F32)], axis=1)
        vals = jnp.where(forced, FORCE_SCORE, jnp.where(future, -1.0, imp))
        sel = jnp.where(_topk_mask(vals, nsel, min(TOP_N, nsel)), 1.0, 0.0)
        sel = jnp.concatenate([sel] * HEADS_PER_GROUP, axis=0)
        sel_keys = _dot(sel[:, :nblk_past].astype(BF16), e_ref[...])
        m_s_past = (sel_keys > 0.5) & m_past
        m_s_new = (sel[:, nblk_past:nblk_past + 1] > 0.5) & m_new
        kn = kvn_ref[:, 2 * LANES:4 * LANES]
        o_s = _attend(qg, [(buf[slot, ck, :].astype(BF16), buf[slot, cv, :].astype(BF16), m_s_past, True),
                           (kn[:, ck].astype(BF16), kn[:, cv].astype(BF16), m_s_new, False)])
        o_w = _attend(qg, [(win_ref[ck, :].astype(BF16), win_ref[cv, :].astype(BF16), m_wstate, True),
                           (wrn_ref[:, ck].astype(BF16), wrn_ref[:, cv].astype(BF16), m_wnew, False)])
        for r, h in enumerate(heads):
            rs = slice(r * ts, (r + 1) * ts)
            o_ref[:, h * HEAD_DIM:(h + 1) * HEAD_DIM] = (
                gates[:, 3 * h:3 * h + 1] * o_c[rs] + gates[:, 3 * h + 1:3 * h + 2] * o_s[rs]
                + gates[:, 3 * h + 2:3 * h + 3] * o_w[rs])


def _sattn(pt_flat, cache_fm, layer, q, gl, cmp, kv4, wr, win_fm, bsz, ts, npages):
    past = npages * PAGE_SIZE
    nc = past // CMP_BLOCK
    win_len = win_fm.shape[3]
    expand = _block_expand(past // SEL_BLOCK, past)
    row = lambda width: pl.BlockSpec((ts, width), lambda b, pt: (b, 0))
    grid_spec = pltpu.PrefetchScalarGridSpec(
        num_scalar_prefetch=1, grid=(bsz,),
        in_specs=[pl.BlockSpec(memory_space=pl.ANY), row(Q_COLS), row(LANES),
                  pl.BlockSpec((nc, 2 * LANES), lambda b, pt: (b, 0)),
                  row(4 * LANES), row(2 * LANES),
                  pl.BlockSpec((None, None, 2 * LANES, win_len), lambda b, pt: (layer, b, 0, 0)),
                  _const_spec(expand.shape)],
        out_specs=row(NSA_WIDTH),
        scratch_shapes=[pltpu.VMEM((2, 2 * LANES, past), F32), pltpu.SemaphoreType.DMA((2,))])
    return pl.pallas_call(
        functools.partial(_sattn_kernel, layer=layer, npages=npages, ts=ts),
        grid_spec=grid_spec,
        out_shape=jax.ShapeDtypeStruct((bsz * ts, NSA_WIDTH), F32),
        compiler_params=_cparams(1, disable_bounds_checks=True),
        name="nsa_sample")(pt_flat, cache_fm, q, gl, cmp, kv4, wr, win_fm, expand)


RG_STATE_ROWS = SUBLANES
RG_ROWS = 256


def _rglru_kernel(rx_ref, ry_ref, st_ref, cw_ref, cb_ref, wa_ref, ba_ref, wx_ref, bx_ref, lam_ref,
                  out_ref, hl_ref, nb_ref, xs, a_s, b_s, *, t_len):
    ch = min(RG_ROWS, t_len)
    xs[0:RG_STATE_ROWS, :] = st_ref[0]
    xs[RG_STATE_ROWS:RG_STATE_ROWS + t_len, :] = rx_ref[...]
    lam = lam_ref[...]
    softplus_neg_lam = jnp.maximum(-lam, 0.0) + jnp.log1p(jnp.exp(-jnp.abs(lam)))
    sub = lax.broadcasted_iota(I32, (ch, RG_WIDTH), 0) & (SUBLANES - 1)
    for c in range(t_len // ch):
        r0 = c * ch
        xc = cb_ref[...]
        for j in range(RG_CONV):
            xc = xc + cw_ref[j:j + 1, :] * xs[r0 + RG_STATE_ROWS - (RG_CONV - 1) + j:
                                              r0 + RG_STATE_ROWS - (RG_CONV - 1) + j + ch, :]
        xb = xc.astype(BF16)
        r = jax.nn.sigmoid(_dot(xb, wa_ref[...]) + ba_ref[...])
        i = jax.nn.sigmoid(_dot(xb, wx_ref[...]) + bx_ref[...])
        log_a = -RG_C * r * softplus_neg_lam
        a = jnp.exp(log_a)
        one_minus_a2 = -jnp.tanh(log_a) * (jnp.exp(2.0 * log_a) + 1.0)
        bb = jnp.sqrt(one_minus_a2) * (i * xc)
        for s in (1, 2, 4):
            ok = sub >= s
            a_prev = pltpu.roll(a, s, 0)
            b_prev = pltpu.roll(bb, s, 0)
            bb = jnp.where(ok, a * b_prev + bb, bb)
            a = jnp.where(ok, a * a_prev, a)
        a_s[r0:r0 + ch, :] = a
        b_s[r0:r0 + ch, :] = bb

    def step(k, h):
        r0 = pl.multiple_of(k * SUBLANES, SUBLANES)
        hk = b_s[pl.ds(r0, SUBLANES), :] + a_s[pl.ds(r0, SUBLANES), :] * h
        b_s[pl.ds(r0, SUBLANES), :] = hk
        return jnp.broadcast_to(hk[SUBLANES - 1:SUBLANES, :], (SUBLANES, RG_WIDTH))

    h0 = jnp.broadcast_to(st_ref[0, 0:1, :], (SUBLANES, RG_WIDTH))
    h_fin = lax.fori_loop(0, t_len // SUBLANES, step, h0)
    hl_ref[0] = h_fin[0:1, :]
    nb_ref[0] = xs[RG_STATE_ROWS + t_len - (RG_CONV - 1):RG_STATE_ROWS + t_len, :]
    for c in range(t_len // ch):
        r0 = c * ch
        out_ref[r0:r0 + ch, :] = (b_s[r0:r0 + ch, :] * jax.nn.gelu(ry_ref[r0:r0 + ch, :])).astype(out_ref.dtype)


def _rglru(rx, ry, st, cw, cb, wa, ba, wx, bx, lam, bsz, t):
    row = pl.BlockSpec((t, RG_WIDTH), lambda b: (b, 0))
    vec = _const_spec((1, RG_WIDTH))
    return pl.pallas_call(
        functools.partial(_rglru_kernel, t_len=t), grid=(bsz,),
        in_specs=[row, row, pl.BlockSpec((1, RG_STATE_ROWS, RG_WIDTH), lambda b: (b, 0, 0)),
                  _const_spec(cw.shape), vec, _const_spec(wa.shape), vec, _const_spec(wx.shape), vec, vec],
        out_specs=[row, pl.BlockSpec((1, 1, RG_WIDTH), lambda b: (b, 0, 0)),
                   pl.BlockSpec((1, RG_CONV - 1, RG_WIDTH), lambda b: (b, 0, 0))],
        out_shape=[jax.ShapeDtypeStruct((bsz * t, RG_WIDTH), _matmul_input_dtype(t)),
                   jax.ShapeDtypeStruct((bsz, 1, RG_WIDTH), F32),
                   jax.ShapeDtypeStruct((bsz, RG_CONV - 1, RG_WIDTH), F32)],
        scratch_shapes=[pltpu.VMEM((RG_STATE_ROWS + t, RG_WIDTH), F32),
                        pltpu.VMEM((t, RG_WIDTH), F32), pltpu.VMEM((t, RG_WIDTH), F32)],
        compiler_params=_cparams(1), name="rglru")(rx, ry, st, cw, cb, wa, ba, wx, bx, lam)


def _proj_ln_kernel(*refs, n_in):
    a_refs = refs[:n_in]
    w_refs = refs[n_in:2 * n_in]
    x_ref, mod_ref, bias_ref, g_ref, b_ref, o_ref = refs[2 * n_in:]
    out = bias_ref[...]
    for a_ref, w_ref in zip(a_refs, w_refs):
        out = out + _dot(a_ref[...].astype(BF16), w_ref[...])
    o_ref[...] = _post_norm(x_ref[...], mod_ref[0], out, g_ref[...], b_ref[...])


def _proj_ln(a_list, w_list, x2d, mod3, bias, g, b, tm, tps):
    m = x2d.shape[0]
    n_in = len(a_list)
    row = lambda width: pl.BlockSpec((tm, width), lambda i: (i, 0))
    vec = _const_spec((1, D_MODEL))
    return pl.pallas_call(
        functools.partial(_proj_ln_kernel, n_in=n_in), grid=(m // tm,),
        in_specs=[row(a.shape[1]) for a in a_list] + [_const_spec(w.shape) for w in w_list]
        + [row(D_MODEL), _mod_spec(mod3, tm, tps), vec, vec, vec],
        out_specs=row(D_MODEL),
        out_shape=jax.ShapeDtypeStruct((m, D_MODEL), F32),
        compiler_params=_cparams(1), name="proj_postnorm")(*a_list, *w_list, x2d, mod3, bias, g, b)


def _ffn_kernel(x_ref, mod_ref, w1_ref, w3_ref, w2_ref, g_ref, b_ref, o_ref, *, ff):
    x = x_ref[...]
    mod = mod_ref[0]
    h = _modulate(x, mod).astype(BF16)
    acc = jnp.zeros(x.shape, F32)
    for c in range(ff // FF_CHUNK):
        cs = slice(c * FF_CHUNK, (c + 1) * FF_CHUNK)
        z = _silu(_dot(h, w1_ref[:, cs])) * _dot(h, w3_ref[:, cs])
        acc = acc + _dot(z.astype(BF16), w2_ref[cs, :])
    o_ref[...] = _post_norm(x, mod, acc, g_ref[...], b_ref[...])


def _ffn(x2d, mod3, w1, w3, w2, g, b, tm, tps):
    m = x2d.shape[0]
    row = pl.BlockSpec((tm, D_MODEL), lambda i: (i, 0))
    vec = _const_spec((1, D_MODEL))
    return pl.pallas_call(
        functools.partial(_ffn_kernel, ff=w1.shape[1]), grid=(m // tm,),
        in_specs=[row, _mod_spec(mod3, tm, tps), _const_spec(w1.shape, True), _const_spec(w3.shape, True),
                  _const_spec(w2.shape, True), vec, vec],
        out_specs=row,
        out_shape=jax.ShapeDtypeStruct((m, D_MODEL), F32),
        compiler_params=_cparams(1), name="dense_ffn")(x2d, mod3, w1, w3, w2, g, b)


def _cf_in_kernel(x_ref, mod_ref, w_ref, b_ref, o_ref):
    h = _modulate(x_ref[...], mod_ref[0]).astype(BF16)
    u = _dot(h, w_ref[...]) + b_ref[...]
    o_ref[...] = u[:, :D_MODEL] * jax.nn.sigmoid(u[:, D_MODEL:])


def _cf_in(x2d, mod3, w, bias, tm, tps):
    m = x2d.shape[0]
    row = pl.BlockSpec((tm, D_MODEL), lambda i: (i, 0))
    return pl.pallas_call(
        _cf_in_kernel, grid=(m // tm,),
        in_specs=[row, _mod_spec(mod3, tm, tps), _const_spec(w.shape), _const_spec(bias.shape)],
        out_specs=row,
        out_shape=jax.ShapeDtypeStruct((m, D_MODEL), F32),
        compiler_params=_cparams(1), name="conformer_in")(x2d, mod3, w, bias)


def _cf_conv_kernel(x_ref, halo_ref, dw_ref, db_ref, g_ref, b_ref, z_ref, s_ref, *, tt, zero_first):
    halo = halo_ref[...]
    if zero_first:
        halo = jnp.where(pl.program_id(1) == 0, 0.0, halo)
    s_ref[0, 0:CF_HALO, :] = halo
    s_ref[0, CF_HALO:CF_HALO + tt, :] = x_ref[...]
    n_sh = CF_HALO + tt - SUBLANES
    for k in range(1, SUBLANES):
        s_ref[k, 0:n_sh, :] = s_ref[0, k:k + n_sh, :]
    sub = min(CONV_SUB, tt)
    first = CF_HALO - (CF_KERNEL - 1)
    for c in range(tt // sub):
        r0 = c * sub
        y = jnp.broadcast_to(db_ref[...], (sub // SUBLANES, SUBLANES, D_MODEL))
        for j in range(CF_KERNEL):
            k = (first + j) % SUBLANES
            a0 = r0 + first + j - k
            y = y + dw_ref[j] * s_ref[k, a0:a0 + sub, :].reshape(sub // SUBLANES, SUBLANES, D_MODEL)
        z = _silu(_layer_norm(y.reshape(sub, D_MODEL), g_ref[...], b_ref[...]))
        z_ref[r0:r0 + sub, :] = z.astype(z_ref.dtype)


def _cf_conv(glu, halo_src, halo_map, dw, db, g, b, bsz, t, tt, zero_first):
    nt = t // tt
    vec = _const_spec((1, D_MODEL))
    return pl.pallas_call(
        functools.partial(_cf_conv_kernel, tt=tt, zero_first=zero_first), grid=(bsz, nt),
        in_specs=[pl.BlockSpec((tt, D_MODEL), lambda bi, ti: (bi * nt + ti, 0)),
                  pl.BlockSpec((CF_HALO, D_MODEL), halo_map),
                  _const_spec(dw.shape), vec, vec, vec],
        out_specs=pl.BlockSpec((tt, D_MODEL), lambda bi, ti: (bi * nt + ti, 0)),
        out_shape=jax.ShapeDtypeStruct((bsz * t, D_MODEL), _matmul_input_dtype(tt)),
        scratch_shapes=[pltpu.VMEM((SUBLANES, CF_HALO + tt, D_MODEL), F32)],
        compiler_params=_cparams(2), name="conformer_conv")(glu, halo_src, dw, db, g, b)


def _router_kernel(x_ref, mod_ref, rw_ref, h_ref, ids_ref, gates_ref, *, n_real):
    h = _modulate(x_ref[...], mod_ref[0])
    hz = jnp.where(pl.program_id(0) < n_real, h, 0.0)
    for s in range(ROW_PARTS):
        h_ref[pl.ds(s, h.shape[0], stride=ROW_PARTS), :] = hz[:, s * LANES:(s + 1) * LANES]
    logits = lax.dot_general(h, rw_ref[...], (((1,), (0,)), ((), ())), precision=lax.Precision.HIGHEST,
                             preferred_element_type=F32)
    lane = lax.broadcasted_iota(I32, logits.shape, 1)
    logits = jnp.where(lane < N_EXPERTS, logits, -jnp.inf)
    m1 = jnp.max(logits, axis=-1, keepdims=True)
    i1 = jnp.min(jnp.where(logits == m1, lane, LANES), axis=-1, keepdims=True)
    rest = jnp.where(lane == i1, -jnp.inf, logits)
    m2 = jnp.max(rest, axis=-1, keepdims=True)
    i2 = jnp.min(jnp.where(rest == m2, lane, LANES), axis=-1, keepdims=True)
    e2 = jnp.exp(m2 - m1)
    inv = 1.0 / (1.0 + e2)
    col = lax.broadcasted_iota(I32, ids_ref.shape, 1)
    ids_ref[...] = jnp.where(col == 0, i1, jnp.where(col == 1, i2, 0))
    gates_ref[...] = jnp.where(col == 0, inv, jnp.where(col == 1, e2 * inv, 0.0))


def _router(x2d, mod3, rw, tm, tps, n_total, row_off, h_prev):
    m = x2d.shape[0]
    n_real = m // tm
    off = row_off // tm
    if h_prev is None:
        steps = -(-n_total // tm)
        clamp = lambda i: jnp.minimum(i, n_real - 1)
    else:
        steps = n_real
        clamp = lambda i: i
    if mod3.shape[1] == 1:
        mspec = pl.BlockSpec((1, 1, mod3.shape[2]), lambda i: (clamp(i) // tps, 0, 0))
    else:
        mspec = pl.BlockSpec((1, tm, mod3.shape[2]), lambda i: (0, clamp(i), 0))
    small = lambda: pl.BlockSpec((tm, SUBLANES), lambda i: (clamp(i), 0))
    in_specs = [pl.BlockSpec((tm, D_MODEL), lambda i: (clamp(i), 0)), mspec, _const_spec(rw.shape)]
    args = [x2d, mod3, rw]
    kern = functools.partial(_router_kernel, n_real=n_real)
    aliases = {}
    if h_prev is not None:
        in_specs.append(pl.BlockSpec(memory_space=pl.ANY))
        args.append(h_prev)
        aliases = {3: 0}
        kern = lambda x, md, rw_, hp, h, ids, gt: _router_kernel(x, md, rw_, h, ids, gt, n_real=n_real)
    return pl.pallas_call(
        kern, grid=(steps,), in_specs=in_specs,
        out_specs=[pl.BlockSpec((tm * ROW_PARTS, LANES), lambda i: (i + off, 0)), small(), small()],
        out_shape=[jax.ShapeDtypeStruct((n_total * ROW_PARTS, LANES), F32), jax.ShapeDtypeStruct((m, SUBLANES), I32),
                   jax.ShapeDtypeStruct((m, SUBLANES), F32)],
        input_output_aliases=aliases,
        compiler_params=_cparams(1), name="moe_router")(*args)


ROW_SRC_BITS = 15
ROW_PARTS = D_MODEL // LANES


DMA_GROUP = 8


def _for_rows(n, fn):
    full = n // DMA_GROUP

    def group(gi, c):
        for u in range(DMA_GROUP):
            fn(gi * DMA_GROUP + u)
        return c
    lax.fori_loop(0, full, group, 0)
    if isinstance(n, int):
        for r in range(full * DMA_GROUP, n):
            fn(r)
    else:
        for u in range(DMA_GROUP - 1):
            r = full * DMA_GROUP + u

            @pl.when(r < n)
            def _():
                fn(r)


def _experts_kernel(blk_e_ref, plan_ref, cnt_ref, nused_ref, h_ref, w1_ref, w3_ref, w2_ref, y_ref,
                    xs, xb, acc, ybuf, gsem, ssem, *, n_dst):
    i = pl.program_id(0)
    j = pl.program_id(1)
    nused = nused_ref[0]
    used = i < nused
    last = j == pl.num_programs(1) - 1
    src_mask = (1 << ROW_SRC_BITS) - 1
    rows_per_step = MOE_TILE // MOE_DMA_STEPS

    def tile_rows(row):
        return pl.ds(pl.multiple_of(row * ROW_PARTS, ROW_PARTS), ROW_PARTS)

    def gather_row(blk, r):
        tok = plan_ref[blk * MOE_TILE + r] & src_mask
        pltpu.make_async_copy(h_ref.at[tile_rows(tok), :], xs.at[blk % 2, tile_rows(r), :], gsem.at[blk % 2]).start()

    def wait_gather(slot):
        pltpu.make_async_copy(h_ref.at[pl.ds(0, MOE_TILE * ROW_PARTS), :], xs.at[slot], gsem.at[slot]).wait()

    def scatter_row(blk, n_valid, r):
        dst = plan_ref[blk * MOE_TILE + r] >> ROW_SRC_BITS
        dst = jnp.where(r < n_valid, dst, n_dst + r)
        pltpu.make_async_copy(ybuf.at[tile_rows(r), :], y_ref.at[tile_rows(dst), :], ssem).start()

    def wait_scatter():
        pltpu.make_async_copy(ybuf, y_ref.at[pl.ds(0, MOE_TILE * ROW_PARTS), :], ssem).wait()

    def part(s):
        return pl.ds(s, MOE_TILE, stride=ROW_PARTS)

    @pl.when(used & (j == 0))
    def _():
        @pl.when(i == 0)
        def _():
            _for_rows(MOE_TILE, lambda r: gather_row(i, r))
            ybuf[...] = jnp.zeros(ybuf.shape, F32)
        wait_gather(i % 2)
        for s in range(ROW_PARTS):
            xb[:, s * LANES:(s + 1) * LANES] = xs[i % 2, part(s), :].astype(BF16)
        acc[...] = jnp.zeros(acc.shape, F32)

    def compute():
        x = xb[...]
        z = _silu(_dot(x, w1_ref[...])) * _dot(x, w3_ref[...])
        acc[...] += _dot(z.astype(BF16), w2_ref[...].astype(BF16))

    @pl.when(used & (j < MOE_DMA_STEPS))
    def _():
        nxt = jnp.minimum(i + 1, nused - 1)
        prev = jnp.maximum(i - 1, 0)
        n_prev = jnp.where(i >= 1, cnt_ref[prev], 0)
        r0 = j * rows_per_step
        for u in range(rows_per_step):
            gather_row(nxt, r0 + u)
            scatter_row(prev, n_prev, r0 + u)
        compute()

    @pl.when(used & (j >= MOE_DMA_STEPS))
    def _():
        compute()

    @pl.when(used & last)
    def _():
        wait_scatter()
        for s in range(ROW_PARTS):
            ybuf[part(s), :] = acc[:, s * LANES:(s + 1) * LANES]

        @pl.when(i == nused - 1)
        def _():
            _for_rows(MOE_TILE, lambda r: scatter_row(i, cnt_ref[i], r))
            wait_scatter()
            wait_gather(i % 2)


def _experts(blk_e, plan, cnt, nused, h_all, n_tok, w1, w3, w2, layer, n_blk):
    ff = w1.shape[3]
    nj = ff // MOE_FF_CHUNK

    def jj(i, j, nu):
        return jnp.where(i < nu[0], j, nj - 1)

    wspec = lambda shape, imap: pl.BlockSpec(shape, imap)
    up = (None, None, D_MODEL, MOE_FF_CHUNK)
    grid_spec = pltpu.PrefetchScalarGridSpec(
        num_scalar_prefetch=4, grid=(n_blk, nj),
        in_specs=[pl.BlockSpec(memory_space=pl.ANY),
                  wspec(up, lambda i, j, be, pn, ct, nu: (layer, be[i], 0, jj(i, j, nu))),
                  wspec(up, lambda i, j, be, pn, ct, nu: (layer, be[i], 0, jj(i, j, nu))),
                  wspec((None, None, MOE_FF_CHUNK, D_MODEL),
                        lambda i, j, be, pn, ct, nu: (layer, be[i], jj(i, j, nu), 0))],
        out_specs=pl.BlockSpec(memory_space=pl.ANY),
        scratch_shapes=[pltpu.VMEM((2, MOE_TILE * ROW_PARTS, LANES), F32), pltpu.VMEM((MOE_TILE, D_MODEL), BF16),
                        pltpu.VMEM((MOE_TILE, D_MODEL), F32), pltpu.VMEM((MOE_TILE * ROW_PARTS, LANES), F32),
                        pltpu.SemaphoreType.DMA((2,)), pltpu.SemaphoreType.DMA(())])
    assert nj == MOE_STEPS and ff == nj * MOE_FF_CHUNK
    return pl.pallas_call(
        functools.partial(_experts_kernel, n_dst=2 * n_tok), grid_spec=grid_spec,
        out_shape=jax.ShapeDtypeStruct(((2 * n_tok + MOE_TILE) * ROW_PARTS, LANES), F32),
        compiler_params=_cparams(2, disable_bounds_checks=True),
        name="moe_experts")(blk_e, plan, cnt, nused, h_all, w1, w3, w2)


def _combine_kernel(y0_ref, y1_ref, gates_ref, x_ref, mod_ref, g_ref, b_ref, o_ref):
    tm = x_ref.shape[0]

    def rows(y_ref):
        return jnp.concatenate([y_ref[pl.ds(s, tm, stride=ROW_PARTS), :] for s in range(ROW_PARTS)], axis=1)

    gates = gates_ref[...]
    out = gates[:, 0:1] * rows(y0_ref) + gates[:, 1:2] * rows(y1_ref)
    o_ref[...] = _post_norm(x_ref[...], mod_ref[0], out, g_ref[...], b_ref[...])


def _combine(ys, gates, x2d, mod3, g, b, tm, tps, tok_off, n_tok):
    m = x2d.shape[0]
    off0 = tok_off // tm
    off1 = (n_tok + tok_off) // tm
    row = lambda width: pl.BlockSpec((tm, width), lambda i: (i, 0))
    vec = _const_spec((1, D_MODEL))
    return pl.pallas_call(
        _combine_kernel, grid=(m // tm,),
        in_specs=[pl.BlockSpec((tm * ROW_PARTS, LANES), lambda i: (i + off0, 0)),
                  pl.BlockSpec((tm * ROW_PARTS, LANES), lambda i: (i + off1, 0)),
                  row(SUBLANES), row(D_MODEL), _mod_spec(mod3, tm, tps), vec, vec],
        out_specs=row(D_MODEL),
        out_shape=jax.ShapeDtypeStruct((m, D_MODEL), F32),
        compiler_params=_cparams(1), name="moe_combine")(ys, ys, gates, x2d, mod3, g, b)


def _route_plan(ids_all, n_tok):
    fe = ids_all[:, :2].reshape(-1)
    onehot = (fe[:, None] == jnp.arange(N_EXPERTS, dtype=I32)[None, :]).astype(I32)
    csum = jnp.cumsum(onehot, axis=0)
    rank = jnp.take_along_axis(csum, fe[:, None], axis=1)[:, 0] - 1
    counts = csum[-1]
    padded = (counts + MOE_TILE - 1) // MOE_TILE * MOE_TILE
    pend = jnp.cumsum(padded)
    pstart = pend - padded
    dest = (pstart[fe] + rank).astype(I32)
    n_blk = -(-(2 * n_tok) // MOE_TILE) + N_EXPERTS
    a = jnp.arange(2 * n_tok, dtype=I32)
    word = (a // 2) | (((a % 2) * n_tok + a // 2) << ROW_SRC_BITS)
    plan = jnp.zeros((n_blk * MOE_TILE,), I32).at[dest].set(word)
    nused = (pend[-1] // MOE_TILE).astype(I32)
    blk = jnp.arange(n_blk, dtype=I32)
    first_row = jnp.minimum(blk, nused - 1) * MOE_TILE
    blk_e = jnp.sum((first_row[:, None] >= pend[None, :]).astype(I32), axis=1)
    blk_e = jnp.minimum(blk_e, N_EXPERTS - 1)
    cnt = jnp.clip(pstart[blk_e] + counts[blk_e] - blk * MOE_TILE, 0, MOE_TILE)
    cnt = jnp.where(blk < nused, cnt, 0).astype(I32)
    return plan, blk_e, cnt, nused.reshape(1), n_blk


def _rope_tables(pos):
    half = HEAD_DIM // 2
    inv = 1.0 / (ROPE_THETA ** (jnp.arange(half, dtype=F32) * (2.0 / HEAD_DIM)))
    ang = pos.astype(F32)[:, None] * inv[None, :]
    c = jnp.cos(ang)
    s = jnp.sin(ang)
    return jnp.concatenate([c, c, c, c], axis=1), jnp.concatenate([-s, s, -s, s], axis=1)


def _cmp_rope_tables(nc):
    blk = jnp.concatenate([jnp.arange(0, nc, 2), jnp.arange(1, nc, 2)])
    return _rope_tables(blk * CMP_BLOCK + (CMP_BLOCK - 1))


def _w_in_layout(w_in):
    o1 = Q_COLS
    o2 = o1 + KV_COLS
    o3 = o2 + GATE_COLS
    pad = jnp.zeros((D_MODEL, LANES - GATE_COLS), w_in.dtype)
    return jnp.concatenate([w_in[:, :o2], w_in[:, o3:], w_in[:, o2:o3], pad], axis=1).astype(BF16)


def _block_diag(blocks):
    n, a, b = blocks.shape
    eye = jnp.eye(n, dtype=blocks.dtype)
    return (eye[:, None, :, None] * blocks[:, :, None, :]).reshape(n * a, n * b)


def _cmp_weights(w_ck, w_cv, pe_k, pe_v):
    wk = w_ck.reshape(CMP_BLOCK, HEAD_DIM, HEAD_DIM)
    wv = w_cv.reshape(CMP_BLOCK, HEAD_DIM, HEAD_DIM)
    w4 = jax.vmap(lambda a, b: _block_diag(jnp.stack([a, a, b, b])))(wk, wv).astype(BF16)
    pe4 = jnp.concatenate([pe_k, pe_k, pe_v, pe_v], axis=1)
    return w4, pe4


def kernel(x_prompt, x_sample, c_prompt, c_sample, page_table, cache_nsa_kv, state_nsa_win, state_rglru_h, state_rglru_conv, state_conformer_conv, w_mod, b_mod, ln_g, ln_b, w_in_even, w_out_even, w_cmp_k, w_cmp_v, pe_cmp_k, pe_cmp_v, rg_conv_w, rg_conv_b, rg_wa, rg_ba, rg_wx, rg_bx, rg_lam, cf_w1, cf_b1, cf_dw, cf_db, cf_ln_g, cf_ln_b, cf_w2, cf_b2, ff_w1, ff_w3, ff_w2, moe_router, moe_w1, moe_w3, moe_w2):
    bp, tp, d = x_prompt.shape
    bs, ts, _ = x_sample.shape
    npages = page_table.shape[1]
    past = npages * PAGE_SIZE
    n_p = bp * tp
    n_s = bs * ts
    n_tok = n_p + n_s
    assert d == D_MODEL and tp % ROW_TILE == 0 and tp % Q_TILE == 0 and tp >= WINDOW
    assert past % SEL_BLOCK == 0 and ts <= CMP_BLOCK and ts % SUBLANES == 0 and n_s % SUBLANES == 0
    assert n_p % MOE_TILE == 0 and n_p % n_s == 0
    tps_p = tp // ROW_TILE
    xp = x_prompt.reshape(n_p, d)
    xs = x_sample.reshape(n_s, d)
    vec = lambda v: v.reshape(1, -1)

    mod_all = _mod_all(jnp.concatenate([c_prompt, c_sample], axis=0), w_mod, b_mod)

    def mods(l, s):
        mrow = mod_all[2 * l + s]
        return mrow[:bp].reshape(bp, 1, 3 * d), jnp.repeat(mrow[bp:], ts, axis=0).reshape(1, n_s, 3 * d)

    cos_p, sin_p = _rope_tables(jnp.arange(tp))
    cos_s, sin_s = _rope_tables(jnp.tile(past + jnp.arange(ts), bs))
    ccos_p, csin_p = _cmp_rope_tables(tp // CMP_BLOCK)
    ccos_s, csin_s = _cmp_rope_tables(past // CMP_BLOCK)
    pt_flat = page_table.reshape(-1).astype(I32)
    n_even = cache_nsa_kv.shape[0]
    win_len = state_nsa_win.shape[2]
    cache_fm = cache_nsa_kv.transpose(0, 1, 3, 4, 5, 2).reshape(n_even, cache_nsa_kv.shape[1], 4 * LANES, PAGE_SIZE)
    win_fm = state_nsa_win.transpose(0, 1, 3, 4, 5, 2).reshape(n_even, bs, 2 * LANES, win_len)
    tm_c = min(Q_TILE, n_s)
    assert n_p % tm_c == 0 and n_tok % tm_c == 0
    assert n_tok < (1 << ROW_SRC_BITS) and 2 * n_tok < (1 << (31 - ROW_SRC_BITS))

    moe_w1_bf, moe_w3_bf = moe_w1.astype(BF16), moe_w3.astype(BF16)

    kv_p, kv_s, win_p, win_s, rh_p, rh_s, rc_p, rc_s, cc_p, cc_s = ([] for _ in range(10))
    for l in range(DEPTH):
        i = l // 2
        mp0, ms0 = mods(l, 0)
        mp1, ms1 = mods(l, 1)
        g0, b0, g1, b1 = vec(ln_g[l, 0]), vec(ln_b[l, 0]), vec(ln_g[l, 1]), vec(ln_b[l, 1])
        zero_bias = jnp.zeros((1, d), F32)
        if l % 2 == 0:
            w_in = _w_in_layout(w_in_even[i])
            w4, pe4 = _cmp_weights(w_cmp_k[i], w_cmp_v[i], pe_cmp_k[i], pe_cmp_v[i])
            wo_a = w_out_even[i][:NSA_WIDTH].astype(BF16)
            wo_r = w_out_even[i][NSA_WIDTH:].astype(BF16)
            wa = _block_diag(rg_wa[i]).astype(BF16)
            wx = _block_diag(rg_wx[i]).astype(BF16)
            rg_args = (rg_conv_w[i], vec(rg_conv_b[i]), wa, vec(rg_ba[i]), wx, vec(rg_bx[i]), vec(rg_lam[i]))

            q, kv4, wr, rx, ry, gl, kvt, wrt = _win_call(xp, mp0, w_in, cos_p, sin_p, ROW_TILE, tps_p, tp)
            cmp = _compress_prompt(kv4, bp, tp, w4, pe4, ccos_p, csin_p)
            o = _pattn(q, gl, cmp, kv4, kvt, wr, wrt, bp, tp)
            st = jnp.zeros((bp, RG_STATE_ROWS, RG_WIDTH), F32)
            rg, h_last, new_buf = _rglru(rx, ry, st, *rg_args, bp, tp)
            xp = _proj_ln([o, rg], [wo_a, wo_r], xp, mp0, zero_bias, g0, b0, ROW_TILE, tps_p)
            kv_p.append(kvt.reshape(bp, 4, N_KV_A, HEAD_DIM, tp).transpose(0, 4, 1, 2, 3))
            win_p.append(wrt.reshape(bp, 2, N_KV_A, HEAD_DIM, tp)[..., tp - min(WINDOW, tp):]
                         .transpose(0, 4, 1, 2, 3))
            rh_p.append(h_last.reshape(bp, RG_WIDTH))
            rc_p.append(new_buf)

            q, kv4, wr, rx, ry, gl = _win_call(xs, ms0, w_in, cos_s, sin_s, n_s, 1)
            cmp = _compress_sample(pt_flat, cache_fm, i, bs, npages, w4, pe4, ccos_s, csin_s)
            o = _sattn(pt_flat, cache_fm, i, q, gl, cmp, kv4, wr, win_fm, bs, ts, npages)
            st = jnp.concatenate([state_rglru_h[i][:, None, :],
                                  jnp.zeros((bs, RG_STATE_ROWS - RG_CONV, RG_WIDTH), F32),
                                  state_rglru_conv[i]], axis=1)
            rg, h_last, new_buf = _rglru(rx, ry, st, *rg_args, bs, ts)
            xs = _proj_ln([o, rg], [wo_a, wo_r], xs, ms0, zero_bias, g0, b0, n_s, 1)
            kv_s.append(kv4.reshape(bs, ts, 4, N_KV_A, HEAD_DIM))
            wfull = jnp.concatenate([state_nsa_win[i], wr.reshape(bs, ts, 2, N_KV_A, HEAD_DIM)], axis=1)
            win_s.append(wfull[:, wfull.shape[1] - min(WINDOW, wfull.shape[1]):])
            rh_s.append(h_last.reshape(bs, RG_WIDTH))
            rc_s.append(new_buf)

            w1, w3, w2 = ff_w1[i].astype(BF16), ff_w3[i].astype(BF16), ff_w2[i].astype(BF16)
            xp = _ffn(xp, mp1, w1, w3, w2, g1, b1, ROW_TILE, tps_p)
            xs = _ffn(xs, ms1, w1, w3, w2, g1, b1, n_s, 1)
        else:
            cw1 = cf_w1[i].astype(BF16)
            cw2 = cf_w2[i].astype(BF16)
            dw = jnp.broadcast_to(cf_dw[i][:, None, :], (CF_KERNEL, SUBLANES, d))
            conv_args = (dw, vec(cf_db[i]), vec(cf_ln_g[i]), vec(cf_ln_b[i]))

            glu = _cf_in(xp, mp0, cw1, vec(cf_b1[i]), ROW_TILE, tps_p)
            per = CONV_TILE // CF_HALO
            z = _cf_conv(glu, glu, lambda bi, ti: (jnp.maximum((bi * (tp // CONV_TILE) + ti) * per - 1, 0), 0),
                         *conv_args, bp, tp, CONV_TILE, True)
            xp = _proj_ln([z], [cw2], xp, mp0, vec(cf_b2[i]), g0, b0, ROW_TILE, tps_p)
            cc_p.append(glu.reshape(bp, tp, d)[:, tp - (CF_KERNEL - 1):])

            glu = _cf_in(xs, ms0, cw1, vec(cf_b1[i]), n_s, 1)
            halo = jnp.concatenate([jnp.zeros((bs, CF_HALO - (CF_KERNEL - 1), d), F32),
                                    state_conformer_conv[i]], axis=1).reshape(bs * CF_HALO, d)
            z = _cf_conv(glu, halo, lambda bi, ti: (bi, 0), *conv_args, bs, ts, ts, False)
            xs = _proj_ln([z], [cw2], xs, ms0, vec(cf_b2[i]), g0, b0, n_s, 1)
            cc_s.append(jnp.concatenate([state_conformer_conv[i], glu.reshape(bs, ts, d)],
                                        axis=1)[:, -(CF_KERNEL - 1):])

            rw = jnp.concatenate([moe_router[i], jnp.zeros((d, LANES - N_EXPERTS), F32)], axis=1)
            h_all, ids_p, gates_p = _router(xp, mp1, rw, ROW_TILE, tps_p, n_tok, 0, None)
            h_all, ids_s, gates_s = _router(xs, ms1, rw, n_s, 1, n_tok, n_p, h_all)
            plan, blk_e, cnt, nused, n_blk = _route_plan(jnp.concatenate([ids_p, ids_s], axis=0), n_tok)
            ys = _experts(blk_e, plan, cnt, nused, h_all, n_tok, moe_w1_bf, moe_w3_bf, moe_w2, i, n_blk)
            xp = _combine(ys, gates_p, xp, mp1, g1, b1, tm_c, tp // tm_c, 0, n_tok)
            xs = _combine(ys, gates_s, xs, ms1, g1, b1, tm_c, 1, n_p, n_tok)
    return (xp.reshape(bp, tp, d), xs.reshape(bs, ts, d), jnp.stack(kv_p), jnp.stack(kv_s),
            jnp.stack(win_p), jnp.stack(win_s), jnp.stack(rh_p), jnp.stack(rh_s),
            jnp.stack(rc_p), jnp.stack(rc_s), jnp.stack(cc_p), jnp.stack(cc_s))
```

```python
import functools

import jax
import jax.numpy as jnp
from jax import lax
from jax.experimental import pallas as pl
from jax.experimental.pallas import tpu as pltpu

F32 = jnp.float32
BF16 = jnp.bfloat16
I32 = jnp.int32

D_MODEL = 1024
HEAD_DIM = 64
N_HEADS_A = 8
N_KV_A = 2
HEADS_PER_GROUP = N_HEADS_A // N_KV_A
GROUP_COLS = N_KV_A * HEAD_DIM
CMP_BLOCK = 32
SEL_BLOCK = 64
TOP_N = 16
WINDOW = 512
ROPE_THETA = 10000.0
PAGE_SIZE = 128
RG_WIDTH = 512
RG_BLOCKS = 8
RG_CONV = 4
RG_C = 8.0
NSA_WIDTH = 512
Q_COLS = NSA_WIDTH
KV_COLS = 6 * GROUP_COLS
GATE_COLS = 3 * N_HEADS_A
CF_KERNEL = 31
CF_HALO = 32
N_EXPERTS = 8
DEPTH = 4
ALPHA = (2.0 * DEPTH) ** 0.25
LN_EPS = 1e-5
NEG = -1e30
FORCE_SCORE = 1e4
ATTN_SCALE = HEAD_DIM ** -0.5

LANES = 128
SUBLANES = 8
VMEM_LIMIT = 56 * 1024 * 1024

ROW_TILE = 512
Q_TILE = 256
CONV_TILE = 256
CONV_SUB = 32
MOE_TILE = 512
FF_CHUNK = 256
MOE_FF_CHUNK = 896
MOE_STEPS = 4
MOE_DMA_STEPS = 2


def _cparams(n_axes, **kw):
    return pltpu.CompilerParams(dimension_semantics=("arbitrary",) * n_axes,
                                vmem_limit_bytes=VMEM_LIMIT, **kw)


def _const_spec(shape, single_buffer=False):
    n = len(shape)
    if single_buffer:
        return pl.BlockSpec(shape, lambda *a: (0,) * n, pipeline_mode=pl.Buffered(1))
    return pl.BlockSpec(shape, lambda *a: (0,) * n)


def _mod_spec(mod3, tm, tps):
    if mod3.shape[1] == 1:
        return pl.BlockSpec((1, 1, mod3.shape[2]), lambda i: (i // tps, 0, 0))
    return pl.BlockSpec((1, tm, mod3.shape[2]), lambda i: (0, i, 0))


BF16_SUBLANES = 16


def _matmul_input_dtype(block_rows):
    return BF16 if block_rows % BF16_SUBLANES == 0 else F32


def _dot(a, b):
    return jnp.dot(a, b, preferred_element_type=F32)


def _dot_nt(a, b):
    return lax.dot_general(a, b, (((1,), (1,)), ((), ())), preferred_element_type=F32)


def _layer_norm(y, g, b):
    mu = jnp.mean(y, axis=-1, keepdims=True)
    yc = y - mu
    var = jnp.mean(yc * yc, axis=-1, keepdims=True)
    return yc * lax.rsqrt(var + LN_EPS) * g + b


def _post_norm(x, mod, out, g, b):
    gate = mod[:, 2 * D_MODEL:]
    return _layer_norm(ALPHA * x + (1.0 + gate) * out, g, b)


def _modulate(x, mod):
    return x * (1.0 + mod[:, D_MODEL:2 * D_MODEL]) + mod[:, :D_MODEL]


def _silu(x):
    return x * jax.nn.sigmoid(x)


def _rope128(v, cos, sin_signed):
    lane = lax.broadcasted_iota(I32, v.shape, 1)
    from_hi = pltpu.roll(v, LANES - HEAD_DIM // 2, 1)
    from_lo = pltpu.roll(v, HEAD_DIM // 2, 1)
    swapped = jnp.where((lane & (HEAD_DIM - 1)) < HEAD_DIM // 2, from_hi, from_lo)
    return v * cos + swapped * sin_signed


def _rope(v, cos, sin_signed):
    k = v.shape[1] // LANES
    parts = [_rope128(v[:, i * LANES:(i + 1) * LANES], cos, sin_signed) for i in range(k)]
    return parts[0] if k == 1 else jnp.concatenate(parts, axis=1)


def _softmax_parts(parts):
    masked = [jnp.where(m, s, NEG) for s, m in parts]
    mx = functools.reduce(jnp.maximum, [jnp.max(s, axis=-1, keepdims=True) for s in masked])
    es = [jnp.where(m, jnp.exp(s - mx), 0.0) for s, (_, m) in zip(masked, parts)]
    den = functools.reduce(jnp.add, [jnp.sum(e, axis=-1, keepdims=True) for e in es])
    return es, 1.0 / jnp.maximum(den, 1e-30)


def _attend(qb, pieces):
    es, inv = _softmax_parts([(_dot(qb, k) if fm else _dot_nt(qb, k), m) for k, _, m, fm in pieces])
    o = functools.reduce(jnp.add, [_dot_nt(e.astype(BF16), v) if fm else _dot(e.astype(BF16), v)
                                   for e, (_, v, _, fm) in zip(es, pieces)])
    return o * inv


def _topk_mask(vals, n_valid, kk):
    lane = lax.broadcasted_iota(I32, vals.shape, 1)
    rank = jnp.zeros(vals.shape, I32)
    for m in range(n_valid):
        col = vals[:, m:m + 1]
        later = jnp.where(lane > m, 1, 0)
        rank = rank + jnp.where(col > vals, 1, jnp.where(col == vals, later, 0))
    return (rank < kk) & (lane < n_valid)


def _mod_kernel(c_ref, w_ref, b_ref, o_ref):
    s = _silu(c_ref[...]).astype(BF16)
    o_ref[0] = _dot(s, w_ref[0].astype(BF16)) + b_ref[0]


def _mod_all(c_all, w_mod, b_mod):
    n = c_all.shape[0]
    nl = w_mod.shape[0] * w_mod.shape[1]
    w = w_mod.reshape(nl, D_MODEL, 3 * D_MODEL)
    b = b_mod.reshape(nl, 1, 3 * D_MODEL)
    return pl.pallas_call(
        _mod_kernel, grid=(nl, 3),
        in_specs=[pl.BlockSpec((n, D_MODEL), lambda l, j: (0, 0)),
                  pl.BlockSpec((1, D_MODEL, D_MODEL), lambda l, j: (l, 0, j)),
                  pl.BlockSpec((1, 1, D_MODEL), lambda l, j: (l, 0, j))],
        out_specs=pl.BlockSpec((1, n, D_MODEL), lambda l, j: (l, 0, j)),
        out_shape=jax.ShapeDtypeStruct((nl, n, 3 * D_MODEL), F32),
        compiler_params=_cparams(2), name="mod_all")(c_all, w, b)


_O_KV = Q_COLS
_O_RX = _O_KV + KV_COLS
_O_RY = _O_RX + RG_WIDTH
_O_GL = _O_RY + RG_WIDTH
W_IN_COLS = _O_GL + LANES


def _win_kernel(x_ref, mod_ref, w_ref, cos_ref, sin_ref, q_ref, kv_ref, wr_ref, rx_ref, ry_ref, gl_ref,
                *fm_refs):
    h = _modulate(x_ref[...], mod_ref[0]).astype(BF16)
    u = _dot(h, w_ref[...])
    cos = cos_ref[...]
    sin = sin_ref[...]
    q_ref[...] = _rope(u[:, :Q_COLS], cos, sin)
    c = _O_KV
    kv = jnp.concatenate([u[:, c:c + 2 * LANES],
                          _rope128(u[:, c + 2 * LANES:c + 3 * LANES], cos, sin),
                          u[:, c + 3 * LANES:c + 4 * LANES]], axis=1)
    wr = jnp.concatenate([_rope128(u[:, c + 4 * LANES:c + 5 * LANES], cos, sin),
                          u[:, c + 5 * LANES:c + 6 * LANES]], axis=1)
    kv_ref[...] = kv
    wr_ref[...] = wr
    rx_ref[...] = u[:, _O_RX:_O_RY]
    ry_ref[...] = u[:, _O_RY:_O_GL]
    gl_ref[...] = u[:, _O_GL:]
    if fm_refs:
        kvt_ref, wrt_ref = fm_refs
        kvt_ref[...] = kv.T
        wrt_ref[...] = wr.T


def _win_call(x2d, mod3, w, cos, sin, tm, tps, t_len=None):
    m = x2d.shape[0]
    row = lambda width: pl.BlockSpec((tm, width), lambda i: (i, 0))
    widths = (Q_COLS, 4 * LANES, 2 * LANES, RG_WIDTH, RG_WIDTH, LANES)
    out_specs = [row(wd) for wd in widths]
    out_shape = [jax.ShapeDtypeStruct((m, wd), F32) for wd in widths]
    if t_len is not None:
        for feat in (4 * LANES, 2 * LANES):
            out_specs.append(pl.BlockSpec((feat, tm), lambda i: (i // tps, i % tps)))
            out_shape.append(jax.ShapeDtypeStruct((m // t_len * feat, t_len), F32))
    return pl.pallas_call(
        _win_kernel, grid=(m // tm,),
        in_specs=[row(D_MODEL), _mod_spec(mod3, tm, tps), _const_spec(w.shape),
                  pl.BlockSpec((tm, LANES), lambda i: (i % tps, 0)),
                  pl.BlockSpec((tm, LANES), lambda i: (i % tps, 0))],
        out_specs=out_specs, out_shape=out_shape,
        compiler_params=_cparams(1), name="mixer_in")(x2d, mod3, w, cos, sin)


def _compress_core(k_ref, v_ref, w4_ref, pe4_ref, cos_ref, sin_ref, out_ref, nb):
    half = nb // 2
    acc = jnp.zeros((nb, 2 * LANES), F32)
    for l in range(CMP_BLOCK):
        even = pl.ds(l, half, stride=2 * CMP_BLOCK)
        odd = pl.ds(CMP_BLOCK + l, half, stride=2 * CMP_BLOCK)
        xk = jnp.concatenate([k_ref[even, :], k_ref[odd, :]], axis=0)
        xv = jnp.concatenate([v_ref[even, :], v_ref[odd, :]], axis=0)
        x = (jnp.concatenate([xk, xv], axis=1) + pe4_ref[pl.ds(l, 1), :]).astype(BF16)
        acc = acc + _dot(x, w4_ref[l])
    out_ref[:, 0:LANES] = _rope128(acc[:, 0:LANES], cos_ref[...], sin_ref[...])
    out_ref[:, LANES:2 * LANES] = acc[:, LANES:2 * LANES]


def _compress_prompt_kernel(k_ref, v_ref, w4_ref, pe4_ref, cos_ref, sin_ref, out_ref, *, nb):
    _compress_core(k_ref, v_ref, w4_ref, pe4_ref, cos_ref, sin_ref, out_ref, nb)


def _compress_prompt(kv4, bsz, t, w4, pe4, ccos, csin):
    nb = t // CMP_BLOCK
    return pl.pallas_call(
        functools.partial(_compress_prompt_kernel, nb=nb), grid=(bsz,),
        in_specs=[pl.BlockSpec((t, LANES), lambda b: (b, 0)), pl.BlockSpec((t, LANES), lambda b: (b, 1)),
                  _const_spec(w4.shape), _const_spec(pe4.shape),
                  _const_spec(ccos.shape), _const_spec(csin.shape)],
        out_specs=pl.BlockSpec((nb, 2 * LANES), lambda b: (b, 0)),
        out_shape=jax.ShapeDtypeStruct((bsz * nb, 2 * LANES), F32),
        compiler_params=_cparams(1), name="compress_prompt")(kv4, kv4, w4, pe4, ccos, csin)


def _prefetch_pages(pt_ref, cache_ref, sems, layer, feat0, npages, dst_fn):
    b = pl.program_id(0)

    def copy(seq, j):
        slot = seq % 2
        src = cache_ref.at[layer, pt_ref[seq * npages + j], pl.ds(feat0, 2 * LANES), :]
        return pltpu.make_async_copy(src, dst_fn(slot, j), sems.at[slot])

    def start_seq(seq):
        def body(j, c):
            copy(seq, j).start()
            return c
        lax.fori_loop(0, npages, body, 0)

    @pl.when(b == 0)
    def _():
        start_seq(b)

    @pl.when(b + 1 < pl.num_programs(0))
    def _():
        start_seq(b + 1)

    def wait(j, c):
        copy(b, j).wait()
        return c
    lax.fori_loop(0, npages, wait, 0)
    return b % 2


def _compress_sample_kernel(pt_ref, cache_ref, w4_ref, pe4_ref, cos_ref, sin_ref, out_ref, raw, kbuf, vbuf, sems,
                            *, layer, npages):
    def dst(slot, j):
        return raw.at[slot, pl.ds(pl.multiple_of(j * 2 * LANES, 2 * LANES), 2 * LANES), :]
    slot = _prefetch_pages(pt_ref, cache_ref, sems, layer, 0, npages, dst)

    def to_token_major(j, c):
        r0 = pl.multiple_of(j * 2 * LANES, 2 * LANES)
        t0 = pl.multiple_of(j * PAGE_SIZE, PAGE_SIZE)
        kbuf[pl.ds(t0, PAGE_SIZE), :] = raw[slot, pl.ds(r0, LANES), :].T
        vbuf[pl.ds(t0, PAGE_SIZE), :] = raw[slot, pl.ds(r0 + LANES, LANES), :].T
        return c
    lax.fori_loop(0, npages, to_token_major, 0, unroll=4)
    _compress_core(kbuf, vbuf, w4_ref, pe4_ref, cos_ref, sin_ref, out_ref, npages * PAGE_SIZE // CMP_BLOCK)


def _compress_sample(pt_flat, cache_fm, layer, bsz, npages, w4, pe4, ccos, csin):
    past = npages * PAGE_SIZE
    nb = past // CMP_BLOCK
    grid_spec = pltpu.PrefetchScalarGridSpec(
        num_scalar_prefetch=1, grid=(bsz,),
        in_specs=[pl.BlockSpec(memory_space=pl.ANY),
                  _const_spec(w4.shape), _const_spec(pe4.shape),
                  _const_spec(ccos.shape), _const_spec(csin.shape)],
        out_specs=pl.BlockSpec((nb, 2 * LANES), lambda b, pt: (b, 0)),
        scratch_shapes=[pltpu.VMEM((2, npages * 2 * LANES, PAGE_SIZE), F32),
                        pltpu.VMEM((past, LANES), F32), pltpu.VMEM((past, LANES), F32),
                        pltpu.SemaphoreType.DMA((2,))])
    return pl.pallas_call(
        functools.partial(_compress_sample_kernel, layer=layer, npages=npages),
        grid_spec=grid_spec,
        out_shape=jax.ShapeDtypeStruct((bsz * nb, 2 * LANES), F32),
        compiler_params=_cparams(1, disable_bounds_checks=True),
        name="compress_sample")(pt_flat, cache_fm, w4, pe4, ccos, csin)


def _cmp_positions(nc):
    n = lax.broadcasted_iota(I32, (1, nc), 1)
    half = nc // 2
    blk = jnp.where(n < half, 2 * n, 2 * (n - half) + 1)
    return blk * CMP_BLOCK + (CMP_BLOCK - 1)


def _softmax_cols(s, mask):
    s = jnp.where(mask, s, NEG)
    e = jnp.where(mask, jnp.exp(s - jnp.max(s, axis=0, keepdims=True)), 0.0)
    return e, 1.0 / jnp.maximum(jnp.sum(e, axis=0, keepdims=True), 1e-30)


def _pattn_kernel(q_ref, gl_ref, cmp_ref, ks_ref, vst_ref, kw0_ref, kw1_ref, kw2_ref, vwt0_ref, vwt1_ref,
                  vwt2_ref, o_ref, ks_bf, vst_bf, bias_ref, acc_ref, ot_ref, *, tq, t_len):
    ti = pl.program_id(1)
    nc = t_len // CMP_BLOCK
    nsel = t_len // SEL_BLOCK
    half = nc // 2
    blk_per_chunk = tq // SEL_BLOCK
    q0 = ti * tq

    @pl.when(ti == 0)
    def _():
        for g in range(N_KV_A):
            ks_bf[g] = ks_ref[:, g * HEAD_DIM:(g + 1) * HEAD_DIM].astype(BF16)
        vst_bf[...] = vst_ref[...].astype(BF16)

    qj = lax.broadcasted_iota(I32, (1, tq), 1)
    ki = lax.broadcasted_iota(I32, (tq, 1), 0)
    qpos = q0 + qj
    qs = (q_ref[...] * ATTN_SCALE).astype(BF16)
    gates_t = jax.nn.sigmoid(gl_ref[...]).T
    n = lax.broadcasted_iota(I32, (nc, 1), 0)
    cpos = jnp.where(n < half, 2 * n, 2 * (n - half) + 1) * CMP_BLOCK + (CMP_BLOCK - 1)
    m_c = cpos <= qpos
    blk = lax.broadcasted_iota(I32, (nsel, 1), 0)
    cur = qpos // SEL_BLOCK
    forced = (blk == cur) | (blk == 0)
    future = blk > cur
    later_blk = [jnp.where(blk > m, 1, 0) for m in range(nsel)]
    diag_bias = jnp.where(ki <= qj, 0.0, NEG)
    w_masks = []
    for k in range(3):
        dpos = qj - ki + (2 - k) * tq
        w_masks.append((dpos >= 0) & (dpos < WINDOW) & (ti + k - 2 >= 0))
    w_bias = jnp.where(jnp.concatenate(w_masks, axis=0), 0.0, NEG)
    kw_refs = (kw0_ref, kw1_ref, kw2_ref)
    vwt_refs = (vwt0_ref, vwt1_ref, vwt2_ref)

    for g in range(N_KV_A):
        ck = slice(g * HEAD_DIM, (g + 1) * HEAD_DIM)
        kc = cmp_ref[:, ck].astype(BF16)
        vc = cmp_ref[:, LANES + g * HEAD_DIM:LANES + (g + 1) * HEAD_DIM].astype(BF16)
        heads = [g * HEADS_PER_GROUP + r for r in range(HEADS_PER_GROUP)]
        q_heads = [qs[:, h * HEAD_DIM:(h + 1) * HEAD_DIM] for h in heads]
        o_cmp = []
        imp = jnp.zeros((nc, tq), F32)
        for s in [_dot_nt(kc, qh) for qh in q_heads]:
            e, inv = _softmax_cols(s, m_c)
            p = e * inv
            o_cmp.append(lax.dot_general(vc, p.astype(BF16), (((0,), (0,)), ((), ())),
                                         preferred_element_type=F32))
            imp = imp + p
        imp = imp[:half] + imp[half:]
        vals = jnp.where(forced, FORCE_SCORE, jnp.where(future, -1.0, imp))
        rank = jnp.zeros((nsel, tq), I32)
        for m in range(nsel):
            row = vals[m:m + 1, :]
            rank = rank + jnp.where(row > vals, 1, jnp.where(row == vals, later_blk[m], 0))
        bias_ref[g] = jnp.where(rank < min(TOP_N, nsel), 0.0, NEG)

        kw = jnp.concatenate([r[:, ck] for r in kw_refs], axis=0).astype(BF16)
        vwt = [r[ck, :].astype(BF16) for r in vwt_refs]

        def block_bias(c):
            rows = [jnp.broadcast_to(bias_ref[g, pl.ds(c * blk_per_chunk + j, 1), :], (SEL_BLOCK, tq))
                    for j in range(blk_per_chunk)]
            return jnp.concatenate(rows, axis=0)

        def chunk(c, carry, extra_bias=None):
            r0 = pl.multiple_of(c * tq, tq)
            k_chunk = ks_bf[g, pl.ds(r0, tq), :]
            v_chunk = vst_bf[ck, pl.ds(r0, tq)]
            bias = block_bias(c)
            if extra_bias is not None:
                bias = bias + extra_bias
            scores = [_dot_nt(k_chunk, qh) for qh in q_heads]
            out, probs, alphas = [], [], []
            for r in range(HEADS_PER_GROUP):
                m_run, l_run = carry[r]
                s = scores[r] + bias
                m_new = jnp.maximum(m_run, jnp.max(s, axis=0, keepdims=True))
                alpha = jnp.exp(m_run - m_new)
                p = jnp.exp(s - m_new)
                out.append((m_new, alpha * l_run + jnp.sum(p, axis=0, keepdims=True)))
                probs.append(p.astype(BF16))
                alphas.append(alpha)
            for r in range(HEADS_PER_GROUP):
                acc_ref[r] = alphas[r] * acc_ref[r] + _dot(v_chunk, probs[r])
            return tuple(out)

        acc_ref[...] = jnp.zeros(acc_ref.shape, F32)
        init = tuple((jnp.full((1, tq), -jnp.inf, F32), jnp.zeros((1, tq), F32)) for _ in heads)
        stats = chunk(ti, lax.fori_loop(0, ti, chunk, init), diag_bias)

        w_scores = [_dot_nt(kw, qh) + w_bias for qh in q_heads]
        w_probs = [jnp.exp(s - jnp.max(s, axis=0, keepdims=True)) for s in w_scores]
        for r, h in enumerate(heads):
            o_s = acc_ref[r] * (1.0 / jnp.maximum(stats[r][1], 1e-30))
            e = w_probs[r]
            inv = 1.0 / jnp.maximum(jnp.sum(e, axis=0, keepdims=True), 1e-30)
            e = e.astype(BF16)
            o_w = functools.reduce(jnp.add, [_dot(vwt[k], e[k * tq:(k + 1) * tq]) for k in range(3)]) * inv
            ot_ref[h * HEAD_DIM:(h + 1) * HEAD_DIM, :] = (
                gates_t[3 * h:3 * h + 1, :] * o_cmp[r] + gates_t[3 * h + 1:3 * h + 2, :] * o_s
                + gates_t[3 * h + 2:3 * h + 3, :] * o_w)
    o_ref[...] = ot_ref[...].T.astype(o_ref.dtype)


def _block_expand(nblk, nkeys):
    return (jnp.arange(nkeys)[None, :] // SEL_BLOCK == jnp.arange(nblk)[:, None]).astype(BF16)


def _pattn(q, gl, cmp, kv4, kvt, wr, wrt, bsz, t):
    tq = Q_TILE
    nt = t // tq
    nc = t // CMP_BLOCK
    nsel = t // SEL_BLOCK
    row = lambda width: pl.BlockSpec((tq, width), lambda b, i: (b * nt + i, 0))
    kw = lambda back: pl.BlockSpec((tq, LANES), lambda b, i: (b * nt + jnp.maximum(i - back, 0), 0))
    vwt = lambda back: pl.BlockSpec((LANES, tq), lambda b, i: (b * 2 + 1, jnp.maximum(i - back, 0)))
    return pl.pallas_call(
        functools.partial(_pattn_kernel, tq=tq, t_len=t), grid=(bsz, nt),
        in_specs=[row(Q_COLS), row(LANES),
                  pl.BlockSpec((nc, 2 * LANES), lambda b, i: (b, 0)),
                  pl.BlockSpec((t, LANES), lambda b, i: (b, 2)),
                  pl.BlockSpec((LANES, t), lambda b, i: (b * 4 + 3, 0)),
                  kw(2), kw(1), kw(0), vwt(2), vwt(1), vwt(0)],
        out_specs=row(NSA_WIDTH),
        out_shape=jax.ShapeDtypeStruct((bsz * t, NSA_WIDTH), _matmul_input_dtype(tq)),
        scratch_shapes=[pltpu.VMEM((N_KV_A, t, HEAD_DIM), BF16), pltpu.VMEM((LANES, t), BF16),
                        pltpu.VMEM((N_KV_A, nsel, tq), F32), pltpu.VMEM((HEADS_PER_GROUP, HEAD_DIM, tq), F32),
                        pltpu.VMEM((NSA_WIDTH, tq), F32)],
        compiler_params=_cparams(2), name="nsa_prompt")(q, gl, cmp, kv4, kvt, wr, wr, wr, wrt, wrt, wrt)


def _sattn_kernel(pt_ref, cache_ref, q_ref, gl_ref, cmp_ref, kvn_ref, wrn_ref, win_ref, e_ref, o_ref,
                  buf, sems, *, layer, npages, ts):
    past = npages * PAGE_SIZE

    def dst(slot, j):
        return buf.at[slot, :, pl.ds(pl.multiple_of(j * PAGE_SIZE, PAGE_SIZE), PAGE_SIZE)]
    slot = _prefetch_pages(pt_ref, cache_ref, sems, layer, 2 * LANES, npages, dst)
    nc = past // CMP_BLOCK
    nblk_past = past // SEL_BLOCK
    nsel = nblk_past + 1
    lanes_sel = 2 * nblk_past
    rows = HEADS_PER_GROUP * ts
    tok1 = lax.broadcasted_iota(I32, (ts, 1), 0)
    tok = jnp.concatenate([tok1] * HEADS_PER_GROUP, axis=0)
    qpos1 = past + tok1
    qpos = past + tok
    qs = q_ref[...] * ATTN_SCALE
    gates = jax.nn.sigmoid(gl_ref[...])
    m_c = _cmp_positions(nc) <= qpos
    blk = lax.broadcasted_iota(I32, (1, lanes_sel), 1)
    cur = qpos1 // SEL_BLOCK
    forced = (blk == cur) | (blk == 0)
    future = blk > cur
    m_past = lax.broadcasted_iota(I32, (1, past), 1) <= qpos
    tkey = lax.broadcasted_iota(I32, (1, ts), 1)
    m_new = tkey <= tok
    win_len = win_ref.shape[1]
    wpos = past - win_len + lax.broadcasted_iota(I32, (1, win_len), 1)
    dpos = qpos - wpos
    m_wstate = (dpos >= 0) & (dpos < WINDOW) & (wpos >= 0)
    dnew = tok - tkey
    m_wnew = (dnew >= 0) & (dnew < WINDOW)
    for g in range(N_KV_A):
        ck = slice(g * HEAD_DIM, (g + 1) * HEAD_DIM)
        cv = slice(LANES + g * HEAD_DIM, LANES + (g + 1) * HEAD_DIM)
        heads = [g * HEADS_PER_GROUP + r for r in range(HEADS_PER_GROUP)]
        qg = jnp.concatenate([qs[:, h * HEAD_DIM:(h + 1) * HEAD_DIM] for h in heads], axis=0).astype(BF16)
        kc = cmp_ref[:, ck].astype(BF16)
        vc = cmp_ref[:, cv].astype(BF16)
        (e,), inv = _softmax_parts([(_dot_nt(qg, kc), m_c)])
        p = e * inv
        o_c = _dot(p.astype(BF16), vc)
        imp = p[0:ts]
        for r in range(1, HEADS_PER_GROUP):
            imp = imp + p[r * ts:(r + 1) * ts]
        imp = imp[:, :nc // 2] + imp[:, nc // 2:]
        imp = jnp.concatenate([imp, jnp.zeros((ts, lanes_sel - nc // 2), F32)], axis=1)
        vals = jnp.where(forced, FORCE_SCORE, jnp.where(future, -1.0, imp))
        sel = jnp.where(_topk_mask(vals, nsel, min(TOP_N, nsel)), 1.0, 0.0)
        sel = jnp.concatenate([sel] * HEADS_PER_GROUP, axis=0)
        sel_keys = _dot(sel[:, :nblk_past].astype(BF16), e_ref[...])
        m_s_past = (sel_keys > 0.5) & m_past
        m_s_new = (sel[:, nblk_past:nblk_past + 1] > 0.5) & m_new
        kn = kvn_ref[:, 2 * LANES:4 * LANES]
        o_s = _attend(qg, [(buf[slot, ck, :].astype(BF16), buf[slot, cv, :].astype(BF16), m_s_past, True),
                           (kn[:, ck].astype(BF16), kn[:, cv].astype(BF16), m_s_new, False)])
        o_w = _attend(qg, [(win_ref[ck, :].astype(BF16), win_ref[cv, :].astype(BF16), m_wstate, True),
                           (wrn_ref[:, ck].astype(BF16), wrn_ref[:, cv].astype(BF16), m_wnew, False)])
        for r, h in enumerate(heads):
            rs = slice(r * ts, (r + 1) * ts)
            o_ref[:, h * HEAD_DIM:(h + 1) * HEAD_DIM] = (
                gates[:, 3 * h:3 * h + 1] * o_c[rs] + gates[:, 3 * h + 1:3 * h + 2] * o_s[rs]
                + gates[:, 3 * h + 2:3 * h + 3] * o_w[rs])


def _sattn(pt_flat, cache_fm, layer, q, gl, cmp, kv4, wr, win_fm, bsz, ts, npages):
    past = npages * PAGE_SIZE
    nc = past // CMP_BLOCK
    win_len = win_fm.shape[3]
    expand = _block_expand(past // SEL_BLOCK, past)
    row = lambda width: pl.BlockSpec((ts, width), lambda b, pt: (b, 0))
    grid_spec = pltpu.PrefetchScalarGridSpec(
        num_scalar_prefetch=1, grid=(bsz,),
        in_specs=[pl.BlockSpec(memory_space=pl.ANY), row(Q_COLS), row(LANES),
                  pl.BlockSpec((nc, 2 * LANES), lambda b, pt: (b, 0)),
                  row(4 * LANES), row(2 * LANES),
                  pl.BlockSpec((None, None, 2 * LANES, win_len), lambda b, pt: (layer, b, 0, 0)),
                  _const_spec(expand.shape)],
        out_specs=row(NSA_WIDTH),
        scratch_shapes=[pltpu.VMEM((2, 2 * LANES, past), F32), pltpu.SemaphoreType.DMA((2,))])
    return pl.pallas_call(
        functools.partial(_sattn_kernel, layer=layer, npages=npages, ts=ts),
        grid_spec=grid_spec,
        out_shape=jax.ShapeDtypeStruct((bsz * ts, NSA_WIDTH), F32),
        compiler_params=_cparams(1, disable_bounds_checks=True),
        name="nsa_sample")(pt_flat, cache_fm, q, gl, cmp, kv4, wr, win_fm, expand)


RG_STATE_ROWS = SUBLANES
RG_ROWS = 256


def _rglru_kernel(rx_ref, ry_ref, st_ref, cw_ref, cb_ref, wa_ref, ba_ref, wx_ref, bx_ref, lam_ref,
                  out_ref, hl_ref, nb_ref, xs, a_s, b_s, *, t_len):
    ch = min(RG_ROWS, t_len)
    xs[0:RG_STATE_ROWS, :] = st_ref[0]
    xs[RG_STATE_ROWS:RG_STATE_ROWS + t_len, :] = rx_ref[...]
    lam = lam_ref[...]
    softplus_neg_lam = jnp.maximum(-lam, 0.0) + jnp.log1p(jnp.exp(-jnp.abs(lam)))
    sub = lax.broadcasted_iota(I32, (ch, RG_WIDTH), 0) & (SUBLANES - 1)
    for c in range(t_len // ch):
        r0 = c * ch
        xc = cb_ref[...]
        for j in range(RG_CONV):
            xc = xc + cw_ref[j:j + 1, :] * xs[r0 + RG_STATE_ROWS - (RG_CONV - 1) + j:
                                              r0 + RG_STATE_ROWS - (RG_CONV - 1) + j + ch, :]
        xb = xc.astype(BF16)
        r = jax.nn.sigmoid(_dot(xb, wa_ref[...]) + ba_ref[...])
        i = jax.nn.sigmoid(_dot(xb, wx_ref[...]) + bx_ref[...])
        log_a = -RG_C * r * softplus_neg_lam
        a = jnp.exp(log_a)
        one_minus_a2 = -jnp.tanh(log_a) * (jnp.exp(2.0 * log_a) + 1.0)
        bb = jnp.sqrt(one_minus_a2) * (i * xc)
        for s in (1, 2, 4):
            ok = sub >= s
            a_prev = pltpu.roll(a, s, 0)
            b_prev = pltpu.roll(bb, s, 0)
            bb = jnp.where(ok, a * b_prev + bb, bb)
            a = jnp.where(ok, a * a_prev, a)
        a_s[r0:r0 + ch, :] = a
        b_s[r0:r0 + ch, :] = bb

    def step(k, h):
        r0 = pl.multiple_of(k * SUBLANES, SUBLANES)
        hk = b_s[pl.ds(r0, SUBLANES), :] + a_s[pl.ds(r0, SUBLANES), :] * h
        b_s[pl.ds(r0, SUBLANES), :] = hk
        return jnp.broadcast_to(hk[SUBLANES - 1:SUBLANES, :], (SUBLANES, RG_WIDTH))

    h0 = jnp.broadcast_to(st_ref[0, 0:1, :], (SUBLANES, RG_WIDTH))
    h_fin = lax.fori_loop(0, t_len // SUBLANES, step, h0)
    hl_ref[0] = h_fin[0:1, :]
    nb_ref[0] = xs[RG_STATE_ROWS + t_len - (RG_CONV - 1):RG_STATE_ROWS + t_len, :]
    for c in range(t_len // ch):
        r0 = c * ch
        out_ref[r0:r0 + ch, :] = (b_s[r0:r0 + ch, :] * jax.nn.gelu(ry_ref[r0:r0 + ch, :])).astype(out_ref.dtype)


def _rglru(rx, ry, st, cw, cb, wa, ba, wx, bx, lam, bsz, t):
    row = pl.BlockSpec((t, RG_WIDTH), lambda b: (b, 0))
    vec = _const_spec((1, RG_WIDTH))
    return pl.pallas_call(
        functools.partial(_rglru_kernel, t_len=t), grid=(bsz,),
        in_specs=[row, row, pl.BlockSpec((1, RG_STATE_ROWS, RG_WIDTH), lambda b: (b, 0, 0)),
                  _const_spec(cw.shape), vec, _const_spec(wa.shape), vec, _const_spec(wx.shape), vec, vec],
        out_specs=[row, pl.BlockSpec((1, 1, RG_WIDTH), lambda b: (b, 0, 0)),
                   pl.BlockSpec((1, RG_CONV - 1, RG_WIDTH), lambda b: (b, 0, 0))],
        out_shape=[jax.ShapeDtypeStruct((bsz * t, RG_WIDTH), _matmul_input_dtype(t)),
                   jax.ShapeDtypeStruct((bsz, 1, RG_WIDTH), F32),
                   jax.ShapeDtypeStruct((bsz, RG_CONV - 1, RG_WIDTH), F32)],
        scratch_shapes=[pltpu.VMEM((RG_STATE_ROWS + t, RG_WIDTH), F32),
                        pltpu.VMEM((t, RG_WIDTH), F32), pltpu.VMEM((t, RG_WIDTH), F32)],
        compiler_params=_cparams(1), name="rglru")(rx, ry, st, cw, cb, wa, ba, wx, bx, lam)


def _proj_ln_kernel(*refs, n_in):
    a_refs = refs[:n_in]
    w_refs = refs[n_in:2 * n_in]
    x_ref, mod_ref, bias_ref, g_ref, b_ref, o_ref = refs[2 * n_in:]
    out = bias_ref[...]
    for a_ref, w_ref in zip(a_refs, w_refs):
        out = out + _dot(a_ref[...].astype(BF16), w_ref[...])
    o_ref[...] = _post_norm(x_ref[...], mod_ref[0], out, g_ref[...], b_ref[...])


def _proj_ln(a_list, w_list, x2d, mod3, bias, g, b, tm, tps):
    m = x2d.shape[0]
    n_in = len(a_list)
    row = lambda width: pl.BlockSpec((tm, width), lambda i: (i, 0))
    vec = _const_spec((1, D_MODEL))
    return pl.pallas_call(
        functools.partial(_proj_ln_kernel, n_in=n_in), grid=(m // tm,),
        in_specs=[row(a.shape[1]) for a in a_list] + [_const_spec(w.shape) for w in w_list]
        + [row(D_MODEL), _mod_spec(mod3, tm, tps), vec, vec, vec],
        out_specs=row(D_MODEL),
        out_shape=jax.ShapeDtypeStruct((m, D_MODEL), F32),
        compiler_params=_cparams(1), name="proj_postnorm")(*a_list, *w_list, x2d, mod3, bias, g, b)


def _ffn_kernel(x_ref, mod_ref, w1_ref, w3_ref, w2_ref, g_ref, b_ref, o_ref, *, ff):
    x = x_ref[...]
    mod = mod_ref[0]
    h = _modulate(x, mod).astype(BF16)
    acc = jnp.zeros(x.shape, F32)
    for c in range(ff // FF_CHUNK):
        cs = slice(c * FF_CHUNK, (c + 1) * FF_CHUNK)
        z = _silu(_dot(h, w1_ref[:, cs])) * _dot(h, w3_ref[:, cs])
        acc = acc + _dot(z.astype(BF16), w2_ref[cs, :])
    o_ref[...] = _post_norm(x, mod, acc, g_ref[...], b_ref[...])


def _ffn(x2d, mod3, w1, w3, w2, g, b, tm, tps):
    m = x2d.shape[0]
    row = pl.BlockSpec((tm, D_MODEL), lambda i: (i, 0))
    vec = _const_spec((1, D_MODEL))
    return pl.pallas_call(
        functools.partial(_ffn_kernel, ff=w1.shape[1]), grid=(m // tm,),
        in_specs=[row, _mod_spec(mod3, tm, tps), _const_spec(w1.shape, True), _const_spec(w3.shape, True),
                  _const_spec(w2.shape, True), vec, vec],
        out_specs=row,
        out_shape=jax.ShapeDtypeStruct((m, D_MODEL), F32),
        compiler_params=_cparams(1), name="dense_ffn")(x2d, mod3, w1, w3, w2, g, b)


def _cf_in_kernel(x_ref, mod_ref, w_ref, b_ref, o_ref):
    h = _modulate(x_ref[...], mod_ref[0]).astype(BF16)
    u = _dot(h, w_ref[...]) + b_ref[...]
    o_ref[...] = u[:, :D_MODEL] * jax.nn.sigmoid(u[:, D_MODEL:])


def _cf_in(x2d, mod3, w, bias, tm, tps):
    m = x2d.shape[0]
    row = pl.BlockSpec((tm, D_MODEL), lambda i: (i, 0))
    return pl.pallas_call(
        _cf_in_kernel, grid=(m // tm,),
        in_specs=[row, _mod_spec(mod3, tm, tps), _const_spec(w.shape), _const_spec(bias.shape)],
        out_specs=row,
        out_shape=jax.ShapeDtypeStruct((m, D_MODEL), F32),
        compiler_params=_cparams(1), name="conformer_in")(x2d, mod3, w, bias)


def _cf_conv_kernel(x_ref, halo_ref, dw_ref, db_ref, g_ref, b_ref, z_ref, s_ref, *, tt, zero_first):
    halo = halo_ref[...]
    if zero_first:
        halo = jnp.where(pl.program_id(1) == 0, 0.0, halo)
    s_ref[0, 0:CF_HALO, :] = halo
    s_ref[0, CF_HALO:CF_HALO + tt, :] = x_ref[...]
    n_sh = CF_HALO + tt - SUBLANES
    for k in range(1, SUBLANES):
        s_ref[k, 0:n_sh, :] = s_ref[0, k:k + n_sh, :]
    sub = min(CONV_SUB, tt)
    first = CF_HALO - (CF_KERNEL - 1)
    for c in range(tt // sub):
        r0 = c * sub
        y = jnp.broadcast_to(db_ref[...], (sub // SUBLANES, SUBLANES, D_MODEL))
        for j in range(CF_KERNEL):
            k = (first + j) % SUBLANES
            a0 = r0 + first + j - k
            y = y + dw_ref[j] * s_ref[k, a0:a0 + sub, :].reshape(sub // SUBLANES, SUBLANES, D_MODEL)
        z = _silu(_layer_norm(y.reshape(sub, D_MODEL), g_ref[...], b_ref[...]))
        z_ref[r0:r0 + sub, :] = z.astype(z_ref.dtype)


def _cf_conv(glu, halo_src, halo_map, dw, db, g, b, bsz, t, tt, zero_first):
    nt = t // tt
    vec = _const_spec((1, D_MODEL))
    return pl.pallas_call(
        functools.partial(_cf_conv_kernel, tt=tt, zero_first=zero_first), grid=(bsz, nt),
        in_specs=[pl.BlockSpec((tt, D_MODEL), lambda bi, ti: (bi * nt + ti, 0)),
                  pl.BlockSpec((CF_HALO, D_MODEL), halo_map),
                  _const_spec(dw.shape), vec, vec, vec],
        out_specs=pl.BlockSpec((tt, D_MODEL), lambda bi, ti: (bi * nt + ti, 0)),
        out_shape=jax.ShapeDtypeStruct((bsz * t, D_MODEL), _matmul_input_dtype(tt)),
        scratch_shapes=[pltpu.VMEM((SUBLANES, CF_HALO + tt, D_MODEL), F32)],
        compiler_params=_cparams(2), name="conformer_conv")(glu, halo_src, dw, db, g, b)


def _router_kernel(x_ref, mod_ref, rw_ref, h_ref, ids_ref, gates_ref, *, n_real):
    h = _modulate(x_ref[...], mod_ref[0])
    hz = jnp.where(pl.program_id(0) < n_real, h, 0.0)
    for s in range(ROW_PARTS):
        h_ref[pl.ds(s, h.shape[0], stride=ROW_PARTS), :] = hz[:, s * LANES:(s + 1) * LANES]
    logits = lax.dot_general(h, rw_ref[...], (((1,), (0,)), ((), ())), precision=lax.Precision.HIGHEST,
                             preferred_element_type=F32)
    lane = lax.broadcasted_iota(I32, logits.shape, 1)
    logits = jnp.where(lane < N_EXPERTS, logits, -jnp.inf)
    m1 = jnp.max(logits, axis=-1, keepdims=True)
    i1 = jnp.min(jnp.where(logits == m1, lane, LANES), axis=-1, keepdims=True)
    rest = jnp.where(lane == i1, -jnp.inf, logits)
    m2 = jnp.max(rest, axis=-1, keepdims=True)
    i2 = jnp.min(jnp.where(rest == m2, lane, LANES), axis=-1, keepdims=True)
    e2 = jnp.exp(m2 - m1)
    inv = 1.0 / (1.0 + e2)
    col = lax.broadcasted_iota(I32, ids_ref.shape, 1)
    ids_ref[...] = jnp.where(col == 0, i1, jnp.where(col == 1, i2, 0))
    gates_ref[...] = jnp.where(col == 0, inv, jnp.where(col == 1, e2 * inv, 0.0))


def _router(x2d, mod3, rw, tm, tps, n_total, row_off, h_prev):
    m = x2d.shape[0]
    n_real = m // tm
    off = row_off // tm
    if h_prev is None:
        steps = -(-n_total // tm)
        clamp = lambda i: jnp.minimum(i, n_real - 1)
    else:
        steps = n_real
        clamp = lambda i: i
    if mod3.shape[1] == 1:
        mspec = pl.BlockSpec((1, 1, mod3.shape[2]), lambda i: (clamp(i) // tps, 0, 0))
    else:
        mspec = pl.BlockSpec((1, tm, mod3.shape[2]), lambda i: (0, clamp(i), 0))
    small = lambda: pl.BlockSpec((tm, SUBLANES), lambda i: (clamp(i), 0))
    in_specs = [pl.BlockSpec((tm, D_MODEL), lambda i: (clamp(i), 0)), mspec, _const_spec(rw.shape)]
    args = [x2d, mod3, rw]
    kern = functools.partial(_router_kernel, n_real=n_real)
    aliases = {}
    if h_prev is not None:
        in_specs.append(pl.BlockSpec(memory_space=pl.ANY))
        args.append(h_prev)
        aliases = {3: 0}
        kern = lambda x, md, rw_, hp, h, ids, gt: _router_kernel(x, md, rw_, h, ids, gt, n_real=n_real)
    return pl.pallas_call(
        kern, grid=(steps,), in_specs=in_specs,
        out_specs=[pl.BlockSpec((tm * ROW_PARTS, LANES), lambda i: (i + off, 0)), small(), small()],
        out_shape=[jax.ShapeDtypeStruct((n_total * ROW_PARTS, LANES), F32), jax.ShapeDtypeStruct((m, SUBLANES), I32),
                   jax.ShapeDtypeStruct((m, SUBLANES), F32)],
        input_output_aliases=aliases,
        compiler_params=_cparams(1), name="moe_router")(*args)


ROW_SRC_BITS = 15
ROW_PARTS = D_MODEL // LANES


DMA_GROUP = 8


def _for_rows(n, fn):
    full = n // DMA_GROUP

    def group(gi, c):
        for u in range(DMA_GROUP):
            fn(gi * DMA_GROUP + u)
        return c
    lax.fori_loop(0, full, group, 0)
    if isinstance(n, int):
        for r in range(full * DMA_GROUP, n):
            fn(r)
    else:
        for u in range(DMA_GROUP - 1):
            r = full * DMA_GROUP + u

            @pl.when(r < n)
            def _():
                fn(r)


def _experts_kernel(blk_e_ref, plan_ref, cnt_ref, nused_ref, h_ref, w1_ref, w3_ref, w2_ref, y_ref,
                    xs, xb, acc, ybuf, gsem, ssem, *, n_dst):
    i = pl.program_id(0)
    j = pl.program_id(1)
    nused = nused_ref[0]
    used = i < nused
    last = j == pl.num_programs(1) - 1
    src_mask = (1 << ROW_SRC_BITS) - 1
    rows_per_step = MOE_TILE // MOE_DMA_STEPS

    def tile_rows(row):
        return pl.ds(pl.multiple_of(row * ROW_PARTS, ROW_PARTS), ROW_PARTS)

    def gather_row(blk, r):
        tok = plan_ref[blk * MOE_TILE + r] & src_mask
        pltpu.make_async_copy(h_ref.at[tile_rows(tok), :], xs.at[blk % 2, tile_rows(r), :], gsem.at[blk % 2]).start()

    def wait_gather(slot):
        pltpu.make_async_copy(h_ref.at[pl.ds(0, MOE_TILE * ROW_PARTS), :], xs.at[slot], gsem.at[slot]).wait()

    def scatter_row(blk, n_valid, r):
        dst = plan_ref[blk * MOE_TILE + r] >> ROW_SRC_BITS
        dst = jnp.where(r < n_valid, dst, n_dst + r)
        pltpu.make_async_copy(ybuf.at[tile_rows(r), :], y_ref.at[tile_rows(dst), :], ssem).start()

    def wait_scatter():
        pltpu.make_async_copy(ybuf, y_ref.at[pl.ds(0, MOE_TILE * ROW_PARTS), :], ssem).wait()

    def part(s):
        return pl.ds(s, MOE_TILE, stride=ROW_PARTS)

    @pl.when(used & (j == 0))
    def _():
        @pl.when(i == 0)
        def _():
            _for_rows(MOE_TILE, lambda r: gather_row(i, r))
            ybuf[...] = jnp.zeros(ybuf.shape, F32)
        wait_gather(i % 2)
        for s in range(ROW_PARTS):
            xb[:, s * LANES:(s + 1) * LANES] = xs[i % 2, part(s), :].astype(BF16)
        acc[...] = jnp.zeros(acc.shape, F32)

    def compute():
        x = xb[...]
        z = _silu(_dot(x, w1_ref[...])) * _dot(x, w3_ref[...].astype(BF16))
        acc[...] += _dot(z.astype(BF16), w2_ref[...].astype(BF16))

    @pl.when(used & (j < MOE_DMA_STEPS))
    def _():
        nxt = jnp.minimum(i + 1, nused - 1)
        prev = jnp.maximum(i - 1, 0)
        n_prev = jnp.where(i >= 1, cnt_ref[prev], 0)
        r0 = j * rows_per_step
        for u in range(rows_per_step):
            gather_row(nxt, r0 + u)
            scatter_row(prev, n_prev, r0 + u)
        compute()

    @pl.when(used & (j >= MOE_DMA_STEPS))
    def _():
        compute()

    @pl.when(used & last)
    def _():
        wait_scatter()
        for s in range(ROW_PARTS):
            ybuf[part(s), :] = acc[:, s * LANES:(s + 1) * LANES]

        @pl.when(i == nused - 1)
        def _():
            _for_rows(MOE_TILE, lambda r: scatter_row(i, cnt_ref[i], r))
            wait_scatter()
            wait_gather(i % 2)


def _experts(blk_e, plan, cnt, nused, h_all, n_tok, w1, w3, w2, layer, n_blk):
    ff = w1.shape[3]
    nj = ff // MOE_FF_CHUNK

    def jj(i, j, nu):
        return jnp.where(i < nu[0], j, nj - 1)

    wspec = lambda shape, imap: pl.BlockSpec(shape, imap)
    up = (None, None, D_MODEL, MOE_FF_CHUNK)
    grid_spec = pltpu.PrefetchScalarGridSpec(
        num_scalar_prefetch=4, grid=(n_blk, nj),
        in_specs=[pl.BlockSpec(memory_space=pl.ANY),
                  wspec(up, lambda i, j, be, pn, ct, nu: (layer, be[i], 0, jj(i, j, nu))),
                  wspec(up, lambda i, j, be, pn, ct, nu: (layer, be[i], 0, jj(i, j, nu))),
                  wspec((None, None, MOE_FF_CHUNK, D_MODEL),
                        lambda i, j, be, pn, ct, nu: (layer, be[i], jj(i, j, nu), 0))],
        out_specs=pl.BlockSpec(memory_space=pl.ANY),
        scratch_shapes=[pltpu.VMEM((2, MOE_TILE * ROW_PARTS, LANES), F32), pltpu.VMEM((MOE_TILE, D_MODEL), BF16),
                        pltpu.VMEM((MOE_TILE, D_MODEL), F32), pltpu.VMEM((MOE_TILE * ROW_PARTS, LANES), F32),
                        pltpu.SemaphoreType.DMA((2,)), pltpu.SemaphoreType.DMA(())])
    assert nj == MOE_STEPS and ff == nj * MOE_FF_CHUNK
    return pl.pallas_call(
        functools.partial(_experts_kernel, n_dst=2 * n_tok), grid_spec=grid_spec,
        out_shape=jax.ShapeDtypeStruct(((2 * n_tok + MOE_TILE) * ROW_PARTS, LANES), F32),
        compiler_params=_cparams(2, disable_bounds_checks=True),
        name="moe_experts")(blk_e, plan, cnt, nused, h_all, w1, w3, w2)


def _combine_kernel(y0_ref, y1_ref, gates_ref, x_ref, mod_ref, g_ref, b_ref, o_ref):
    tm = x_ref.shape[0]

    def rows(y_ref):
        return jnp.concatenate([y_ref[pl.ds(s, tm, stride=ROW_PARTS), :] for s in range(ROW_PARTS)], axis=1)

    gates = gates_ref[...]
    out = gates[:, 0:1] * rows(y0_ref) + gates[:, 1:2] * rows(y1_ref)
    o_ref[...] = _post_norm(x_ref[...], mod_ref[0], out, g_ref[...], b_ref[...])


def _combine(ys, gates, x2d, mod3, g, b, tm, tps, tok_off, n_tok):
    m = x2d.shape[0]
    off0 = tok_off // tm
    off1 = (n_tok + tok_off) // tm
    row = lambda width: pl.BlockSpec((tm, width), lambda i: (i, 0))
    vec = _const_spec((1, D_MODEL))
    return pl.pallas_call(
        _combine_kernel, grid=(m // tm,),
        in_specs=[pl.BlockSpec((tm * ROW_PARTS, LANES), lambda i: (i + off0, 0)),
                  pl.BlockSpec((tm * ROW_PARTS, LANES), lambda i: (i + off1, 0)),
                  row(SUBLANES), row(D_MODEL), _mod_spec(mod3, tm, tps), vec, vec],
        out_specs=row(D_MODEL),
        out_shape=jax.ShapeDtypeStruct((m, D_MODEL), F32),
        compiler_params=_cparams(1), name="moe_combine")(ys, ys, gates, x2d, mod3, g, b)


def _route_plan(ids_all, n_tok):
    fe = ids_all[:, :2].reshape(-1)
    onehot = (fe[:, None] == jnp.arange(N_EXPERTS, dtype=I32)[None, :]).astype(I32)
    csum = jnp.cumsum(onehot, axis=0)
    rank = jnp.take_along_axis(csum, fe[:, None], axis=1)[:, 0] - 1
    counts = csum[-1]
    padded = (counts + MOE_TILE - 1) // MOE_TILE * MOE_TILE
    pend = jnp.cumsum(padded)
    pstart = pend - padded
    dest = (pstart[fe] + rank).astype(I32)
    n_blk = -(-(2 * n_tok) // MOE_TILE) + N_EXPERTS
    a = jnp.arange(2 * n_tok, dtype=I32)
    word = (a // 2) | (((a % 2) * n_tok + a // 2) << ROW_SRC_BITS)
    plan = jnp.zeros((n_blk * MOE_TILE,), I32).at[dest].set(word)
    nused = (pend[-1] // MOE_TILE).astype(I32)
    blk = jnp.arange(n_blk, dtype=I32)
    first_row = jnp.minimum(blk, nused - 1) * MOE_TILE
    blk_e = jnp.sum((first_row[:, None] >= pend[None, :]).astype(I32), axis=1)
    blk_e = jnp.minimum(blk_e, N_EXPERTS - 1)
    cnt = jnp.clip(pstart[blk_e] + counts[blk_e] - blk * MOE_TILE, 0, MOE_TILE)
    cnt = jnp.where(blk < nused, cnt, 0).astype(I32)
    return plan, blk_e, cnt, nused.reshape(1), n_blk


def _rope_tables(pos):
    half = HEAD_DIM // 2
    inv = 1.0 / (ROPE_THETA ** (jnp.arange(half, dtype=F32) * (2.0 / HEAD_DIM)))
    ang = pos.astype(F32)[:, None] * inv[None, :]
    c = jnp.cos(ang)
    s = jnp.sin(ang)
    return jnp.concatenate([c, c, c, c], axis=1), jnp.concatenate([-s, s, -s, s], axis=1)


def _cmp_rope_tables(nc):
    blk = jnp.concatenate([jnp.arange(0, nc, 2), jnp.arange(1, nc, 2)])
    return _rope_tables(blk * CMP_BLOCK + (CMP_BLOCK - 1))


def _w_in_layout(w_in):
    o1 = Q_COLS
    o2 = o1 + KV_COLS
    o3 = o2 + GATE_COLS
    pad = jnp.zeros((D_MODEL, LANES - GATE_COLS), w_in.dtype)
    return jnp.concatenate([w_in[:, :o2], w_in[:, o3:], w_in[:, o2:o3], pad], axis=1).astype(BF16)


def _block_diag(blocks):
    n, a, b = blocks.shape
    eye = jnp.eye(n, dtype=blocks.dtype)
    return (eye[:, None, :, None] * blocks[:, :, None, :]).reshape(n * a, n * b)


def _cmp_weights(w_ck, w_cv, pe_k, pe_v):
    wk = w_ck.reshape(CMP_BLOCK, HEAD_DIM, HEAD_DIM)
    wv = w_cv.reshape(CMP_BLOCK, HEAD_DIM, HEAD_DIM)
    w4 = jax.vmap(lambda a, b: _block_diag(jnp.stack([a, a, b, b])))(wk, wv).astype(BF16)
    pe4 = jnp.concatenate([pe_k, pe_k, pe_v, pe_v], axis=1)
    return w4, pe4


def kernel(x_prompt, x_sample, c_prompt, c_sample, page_table, cache_nsa_kv, state_nsa_win, state_rglru_h, state_rglru_conv, state_conformer_conv, w_mod, b_mod, ln_g, ln_b, w_in_even, w_out_even, w_cmp_k, w_cmp_v, pe_cmp_k, pe_cmp_v, rg_conv_w, rg_conv_b, rg_wa, rg_ba, rg_wx, rg_bx, rg_lam, cf_w1, cf_b1, cf_dw, cf_db, cf_ln_g, cf_ln_b, cf_w2, cf_b2, ff_w1, ff_w3, ff_w2, moe_router, moe_w1, moe_w3, moe_w2):
    bp, tp, d = x_prompt.shape
    bs, ts, _ = x_sample.shape
    npages = page_table.shape[1]
    past = npages * PAGE_SIZE
    n_p = bp * tp
    n_s = bs * ts
    n_tok = n_p + n_s
    assert d == D_MODEL and tp % ROW_TILE == 0 and tp % Q_TILE == 0 and tp >= WINDOW
    assert past % SEL_BLOCK == 0 and ts <= CMP_BLOCK and ts % SUBLANES == 0 and n_s % SUBLANES == 0
    assert n_p % MOE_TILE == 0 and n_p % n_s == 0
    tps_p = tp // ROW_TILE
    xp = x_prompt.reshape(n_p, d)
    xs = x_sample.reshape(n_s, d)
    vec = lambda v: v.reshape(1, -1)

    mod_all = _mod_all(jnp.concatenate([c_prompt, c_sample], axis=0), w_mod, b_mod)

    def mods(l, s):
        mrow = mod_all[2 * l + s]
        return mrow[:bp].reshape(bp, 1, 3 * d), jnp.repeat(mrow[bp:], ts, axis=0).reshape(1, n_s, 3 * d)

    cos_p, sin_p = _rope_tables(jnp.arange(tp))
    cos_s, sin_s = _rope_tables(jnp.tile(past + jnp.arange(ts), bs))
    ccos_p, csin_p = _cmp_rope_tables(tp // CMP_BLOCK)
    ccos_s, csin_s = _cmp_rope_tables(past // CMP_BLOCK)
    pt_flat = page_table.reshape(-1).astype(I32)
    n_even = cache_nsa_kv.shape[0]
    win_len = state_nsa_win.shape[2]
    cache_fm = cache_nsa_kv.transpose(0, 1, 3, 4, 5, 2).reshape(n_even, cache_nsa_kv.shape[1], 4 * LANES, PAGE_SIZE)
    win_fm = state_nsa_win.transpose(0, 1, 3, 4, 5, 2).reshape(n_even, bs, 2 * LANES, win_len)
    tm_c = min(Q_TILE, n_s)
    assert n_p % tm_c == 0 and n_tok % tm_c == 0
    assert n_tok < (1 << ROW_SRC_BITS) and 2 * n_tok < (1 << (31 - ROW_SRC_BITS))

    moe_w1_bf = moe_w1.astype(BF16)

    kv_p, kv_s, win_p, win_s, rh_p, rh_s, rc_p, rc_s, cc_p, cc_s = ([] for _ in range(10))
    for l in range(DEPTH):
        i = l // 2
        mp0, ms0 = mods(l, 0)
        mp1, ms1 = mods(l, 1)
        g0, b0, g1, b1 = vec(ln_g[l, 0]), vec(ln_b[l, 0]), vec(ln_g[l, 1]), vec(ln_b[l, 1])
        zero_bias = jnp.zeros((1, d), F32)
        if l % 2 == 0:
            w_in = _w_in_layout(w_in_even[i])
            w4, pe4 = _cmp_weights(w_cmp_k[i], w_cmp_v[i], pe_cmp_k[i], pe_cmp_v[i])
            wo_a = w_out_even[i][:NSA_WIDTH].astype(BF16)
            wo_r = w_out_even[i][NSA_WIDTH:].astype(BF16)
            wa = _block_diag(rg_wa[i]).astype(BF16)
            wx = _block_diag(rg_wx[i]).astype(BF16)
            rg_args = (rg_conv_w[i], vec(rg_conv_b[i]), wa, vec(rg_ba[i]), wx, vec(rg_bx[i]), vec(rg_lam[i]))

            q, kv4, wr, rx, ry, gl, kvt, wrt = _win_call(xp, mp0, w_in, cos_p, sin_p, ROW_TILE, tps_p, tp)
            cmp = _compress_prompt(kv4, bp, tp, w4, pe4, ccos_p, csin_p)
            o = _pattn(q, gl, cmp, kv4, kvt, wr, wrt, bp, tp)
            st = jnp.zeros((bp, RG_STATE_ROWS, RG_WIDTH), F32)
            rg, h_last, new_buf = _rglru(rx, ry, st, *rg_args, bp, tp)
            xp = _proj_ln([o, rg], [wo_a, wo_r], xp, mp0, zero_bias, g0, b0, ROW_TILE, tps_p)
            kv_p.append(kvt.reshape(bp, 4, N_KV_A, HEAD_DIM, tp).transpose(0, 4, 1, 2, 3))
            win_p.append(wrt.reshape(bp, 2, N_KV_A, HEAD_DIM, tp)[..., tp - min(WINDOW, tp):]
                         .transpose(0, 4, 1, 2, 3))
            rh_p.append(h_last.reshape(bp, RG_WIDTH))
            rc_p.append(new_buf)

            q, kv4, wr, rx, ry, gl = _win_call(xs, ms0, w_in, cos_s, sin_s, n_s, 1)
            cmp = _compress_sample(pt_flat, cache_fm, i, bs, npages, w4, pe4, ccos_s, csin_s)
            o = _sattn(pt_flat, cache_fm, i, q, gl, cmp, kv4, wr, win_fm, bs, ts, npages)
            st = jnp.concatenate([state_rglru_h[i][:, None, :],
                                  jnp.zeros((bs, RG_STATE_ROWS - RG_CONV, RG_WIDTH), F32),
                                  state_rglru_conv[i]], axis=1)
            rg, h_last, new_buf = _rglru(rx, ry, st, *rg_args, bs, ts)
            xs = _proj_ln([o, rg], [wo_a, wo_r], xs, ms0, zero_bias, g0, b0, n_s, 1)
            kv_s.append(kv4.reshape(bs, ts, 4, N_KV_A, HEAD_DIM))
            wfull = jnp.concatenate([state_nsa_win[i], wr.reshape(bs, ts, 2, N_KV_A, HEAD_DIM)], axis=1)
            win_s.append(wfull[:, wfull.shape[1] - min(WINDOW, wfull.shape[1]):])
            rh_s.append(h_last.reshape(bs, RG_WIDTH))
            rc_s.append(new_buf)

            w1, w3, w2 = ff_w1[i].astype(BF16), ff_w3[i].astype(BF16), ff_w2[i].astype(BF16)
            xp = _ffn(xp, mp1, w1, w3, w2, g1, b1, ROW_TILE, tps_p)
            xs = _ffn(xs, ms1, w1, w3, w2, g1, b1, n_s, 1)
        else:
            cw1 = cf_w1[i].astype(BF16)
            cw2 = cf_w2[i].astype(BF16)
            dw = jnp.broadcast_to(cf_dw[i][:, None, :], (CF_KERNEL, SUBLANES, d))
            conv_args = (dw, vec(cf_db[i]), vec(cf_ln_g[i]), vec(cf_ln_b[i]))

            glu = _cf_in(xp, mp0, cw1, vec(cf_b1[i]), ROW_TILE, tps_p)
            per = CONV_TILE // CF_HALO
            z = _cf_conv(glu, glu, lambda bi, ti: (jnp.maximum((bi * (tp // CONV_TILE) + ti) * per - 1, 0), 0),
                         *conv_args, bp, tp, CONV_TILE, True)
            xp = _proj_ln([z], [cw2], xp, mp0, vec(cf_b2[i]), g0, b0, ROW_TILE, tps_p)
            cc_p.append(glu.reshape(bp, tp, d)[:, tp - (CF_KERNEL - 1):])

            glu = _cf_in(xs, ms0, cw1, vec(cf_b1[i]), n_s, 1)
            halo = jnp.concatenate([jnp.zeros((bs, CF_HALO - (CF_KERNEL - 1), d), F32),
                                    state_conformer_conv[i]], axis=1).reshape(bs * CF_HALO, d)
            z = _cf_conv(glu, halo, lambda bi, ti: (bi, 0), *conv_args, bs, ts, ts, False)
            xs = _proj_ln([z], [cw2], xs, ms0, vec(cf_b2[i]), g0, b0, n_s, 1)
            cc_s.append(jnp.concatenate([state_conformer_conv[i], glu.reshape(bs, ts, d)],
                                        axis=1)[:, -(CF_KERNEL - 1):])

            rw = jnp.concatenate([moe_router[i], jnp.zeros((d, LANES - N_EXPERTS), F32)], axis=1)
            h_all, ids_p, gates_p = _router(xp, mp1, rw, ROW_TILE, tps_p, n_tok, 0, None)
            h_all, ids_s, gates_s = _router(xs, ms1, rw, n_s, 1, n_tok, n_p, h_all)
            plan, blk_e, cnt, nused, n_blk = _route_plan(jnp.concatenate([ids_p, ids_s], axis=0), n_tok)
            ys = _experts(blk_e, plan, cnt, nused, h_all, n_tok, moe_w1_bf, moe_w3, moe_w2, i, n_blk)
            xp = _combine(ys, gates_p, xp, mp1, g1, b1, tm_c, tp // tm_c, 0, n_tok)
            xs = _combine(ys, gates_s, xs, ms1, g1, b1, tm_c, 1, n_p, n_tok)
    return (xp.reshape(bp, tp, d), xs.reshape(bs, ts, d), jnp.stack(kv_p), jnp.stack(kv_s),
            jnp.stack(win_p), jnp.stack(win_s), jnp.stack(rh_p), jnp.stack(rh_s),
            jnp.stack(rc_p), jnp.stack(rc_s), jnp.stack(cc_p), jnp.stack(cc_s))
```
